```python
import jax, jax.numpy as jnp
from jax import lax
import numpy as np

D_MODEL = 1024
BATCH = 4
SEQ = 4096
DEPTH = 2
DEC_BATCH = 16
DEC_SEQ = 32
PAST_LEN = 1024

CHUNK = 64
HEAD_DIM = 64
MIX_WIDTH = D_MODEL
N_HEADS_A = MIX_WIDTH // (2 * HEAD_DIM)
N_HEADS_B = MIX_WIDTH // (2 * HEAD_DIM)
WIDTH_A = N_HEADS_A * HEAD_DIM
WIDTH_B = N_HEADS_B * HEAD_DIM
BAND_CHUNKS = 8
BAND_PAST = BAND_CHUNKS * CHUNK
MAX_REL = 128
SB_BLOCK = 128
N_KEYS = 128
N_EXPERTS = N_KEYS * N_KEYS
PEER_HEADS = 8
PEER_TOPK = 16
PEER_KEY_DIM = 256
PEER_BLOCK = 256
DEEPNORM_ALPHA = (2 * DEPTH) ** 0.25
DEEPNORM_BETA = (8 * DEPTH) ** -0.25
NORM_EPS = 1e-5
NEG_INF = -1e30

kernel_name = "hybrid_stickbreak_chunkband_peer_stream"


def layer_norm(x, g, b):
    xf = x.astype(jnp.float32)
    mu = jnp.mean(xf, axis=-1, keepdims=True)
    var = jnp.mean(jnp.square(xf - mu), axis=-1, keepdims=True)
    y = (xf - mu) * lax.rsqrt(var + NORM_EPS) * g.astype(jnp.float32) + b.astype(jnp.float32)
    return y.astype(x.dtype)


def project_heads(x, w_in):
    bsz, s, _ = x.shape
    qkv = x @ w_in
    cuts = [WIDTH_A, 2 * WIDTH_A, 3 * WIDTH_A, 3 * WIDTH_A + WIDTH_B, 3 * WIDTH_A + 2 * WIDTH_B]
    qa, ka, va, qb, kb, vb = jnp.split(qkv, cuts, axis=-1)
    ra = lambda t: t.reshape(bsz, s, N_HEADS_A, HEAD_DIM)
    rb = lambda t: t.reshape(bsz, s, N_HEADS_B, HEAD_DIM)
    return ra(qa), ra(ka), ra(va), rb(qb), rb(kb), rb(vb)


def stick_breaking_weights(z, mask):
    log_keep = jnp.where(mask, jax.nn.log_sigmoid(-z), 0.0)
    suffix = lax.cumsum(log_keep, axis=z.ndim - 1, reverse=True)
    log_w = jax.nn.log_sigmoid(z) + (suffix - log_keep)
    return jnp.where(mask, jnp.exp(log_w), 0.0)


def sb_attn_prompt(q, k, v):
    bsz, s, h, dh = q.shape
    scale = dh ** -0.5
    key_pos = jnp.arange(s)

    def block(i):
        qb = lax.dynamic_slice_in_dim(q, i * SB_BLOCK, SB_BLOCK, axis=1)
        z = jnp.einsum('bqhd,bkhd->bhqk', qb, k, preferred_element_type=jnp.float32) * scale
        q_pos = i * SB_BLOCK + jnp.arange(SB_BLOCK)
        mask = key_pos[None, :] < q_pos[:, None]
        w = stick_breaking_weights(z, mask)
        return jnp.einsum('bhqk,bkhd->bqhd', w.astype(v.dtype), v)

    out = lax.map(block, jnp.arange(s // SB_BLOCK))
    return jnp.transpose(out, (1, 0, 2, 3, 4)).reshape(bsz, s, h, dh)


def sb_attn_sample(q, k_all, v_all, past):
    sd, dh = q.shape[1], q.shape[3]
    z = jnp.einsum('bqhd,bkhd->bhqk', q, k_all, preferred_element_type=jnp.float32) * (dh ** -0.5)
    q_pos = past + jnp.arange(sd)
    mask = jnp.arange(k_all.shape[1])[None, :] < q_pos[:, None]
    w = stick_breaking_weights(z, mask)
    return jnp.einsum('bhqk,bkhd->bqhd', w.astype(v_all.dtype), v_all)


def rel_bias_lookup(rel_bias, dist):
    return rel_bias[:, jnp.clip(dist, -MAX_REL, MAX_REL) + MAX_REL].astype(jnp.float32)


def band_attn_prompt(q, k, v, rel_bias):
    bsz, s, h, dh = q.shape
    nc = s // CHUNK
    band = BAND_PAST + CHUNK
    pad = ((0, 0), (BAND_PAST, 0), (0, 0), (0, 0))
    kp = jnp.pad(k, pad)
    vp = jnp.pad(v, pad)
    idx = (jnp.arange(nc) * CHUNK)[:, None] + jnp.arange(band)[None, :]
    kb = kp[:, idx]
    vb = vp[:, idx]
    qc = q.reshape(bsz, nc, CHUNK, h, dh)
    dist = jnp.arange(CHUNK)[:, None] + BAND_PAST - jnp.arange(band)[None, :]
    bias = rel_bias_lookup(rel_bias, dist)
    valid = idx >= BAND_PAST
    sc = jnp.einsum('bcqhd,bckhd->bhcqk', qc, kb, preferred_element_type=jnp.float32) * (dh ** -0.5)
    sc = sc + bias[None, :, None]
    sc = jnp.where(valid[None, None, :, None, :], sc, NEG_INF)
    p = jax.nn.softmax(sc, axis=-1)
    out = jnp.einsum('bhcqk,bckhd->bcqhd', p.astype(v.dtype), vb)
    return out.reshape(bsz, s, h, dh)


def band_attn_sample(q, k_all, v_all, rel_bias, nb):
    sd, dh = q.shape[1], q.shape[3]
    dist = jnp.arange(sd)[:, None] + nb - jnp.arange(k_all.shape[1])[None, :]
    bias = rel_bias_lookup(rel_bias, dist)
    sc = jnp.einsum('bqhd,bkhd->bhqk', q, k_all, preferred_element_type=jnp.float32) * (dh ** -0.5)
    p = jax.nn.softmax(sc + bias[None], axis=-1)
    return jnp.einsum('bhqk,bkhd->bqhd', p.astype(v_all.dtype), v_all)


def head_rms(o, g):
    bsz, s, h, dh = o.shape
    of = o.astype(jnp.float32)
    of = of * lax.rsqrt(jnp.mean(jnp.square(of), axis=-1, keepdims=True) + NORM_EPS)
    return (of.reshape(bsz, s, h * dh) * g.astype(jnp.float32)).astype(o.dtype)


def merge_heads(oa, ob, g_a, g_b, w_out):
    cat = jnp.concatenate([head_rms(oa, g_a), head_rms(ob, g_b)], axis=-1)
    return cat @ w_out


def peer_ffn(x, w_query, sub_keys, u_table, v_table):
    lead = x.shape[:-1]
    xt = x.reshape(-1, D_MODEL)
    t = xt.shape[0]
    nblk = -(-t // PEER_BLOCK)
    xt = jnp.pad(xt, ((0, nblk * PEER_BLOCK - t), (0, 0)))
    half = PEER_KEY_DIM // 2

    def block(xb):
        q = (xb @ w_query).reshape(PEER_BLOCK, PEER_HEADS, PEER_KEY_DIM)
        s1 = jnp.einsum('thd,nd->thn', q[..., :half], sub_keys[0], preferred_element_type=jnp.float32)
        s2 = jnp.einsum('thd,nd->thn', q[..., half:], sub_keys[1], preferred_element_type=jnp.float32)
        v1, i1 = lax.top_k(s1, PEER_TOPK)
        v2, i2 = lax.top_k(s2, PEER_TOPK)
        cand = (v1[..., :, None] + v2[..., None, :]).reshape(PEER_BLOCK, PEER_HEADS, PEER_TOPK * PEER_TOPK)
        score, c = lax.top_k(cand, PEER_TOPK)
        e = (jnp.take_along_axis(i1, c // PEER_TOPK, axis=-1) * N_KEYS
             + jnp.take_along_axis(i2, c % PEER_TOPK, axis=-1))
        g = jax.nn.softmax(score, axis=-1)
        u = u_table[e]
        act = jax.nn.gelu(jnp.einsum('td,thkd->thk', xb, u, preferred_element_type=jnp.float32))
        return jnp.einsum('thk,thkd->td', (g * act).astype(v_table.dtype), v_table[e])

    out = lax.map(block, xt.reshape(nblk, PEER_BLOCK, D_MODEL))
    return out.reshape(-1, D_MODEL)[:t].reshape(*lead, D_MODEL).astype(x.dtype)


def setup_inputs(seed: int = 0) -> dict:
    key = jax.random.key(seed)
    ks = jax.random.split(key, 20)
    nb = min(BAND_PAST, PAST_LEN)
    nrm = jax.random.normal
    f32 = jnp.float32
    in_cols = 3 * WIDTH_A + 3 * WIDTH_B
    col_scale = jnp.concatenate([
        jnp.ones((2 * WIDTH_A,), f32), jnp.full((WIDTH_A,), DEEPNORM_BETA, f32),
        jnp.ones((2 * WIDTH_B,), f32), jnp.full((WIDTH_B,), DEEPNORM_BETA, f32)])
    return {
        "x_prompt": nrm(ks[0], (BATCH, SEQ, D_MODEL), f32),
        "x_sample": nrm(ks[1], (DEC_BATCH, DEC_SEQ, D_MODEL), f32),
        "cache_sb_k": nrm(ks[2], (DEPTH, DEC_BATCH, PAST_LEN, N_HEADS_A, HEAD_DIM), f32),
        "cache_sb_v": nrm(ks[3], (DEPTH, DEC_BATCH, PAST_LEN, N_HEADS_A, HEAD_DIM), f32) * DEEPNORM_BETA,
        "cache_band_k": nrm(ks[4], (DEPTH, DEC_BATCH, nb, N_HEADS_B, HEAD_DIM), f32),
        "cache_band_v": nrm(ks[5], (DEPTH, DEC_BATCH, nb, N_HEADS_B, HEAD_DIM), f32) * DEEPNORM_BETA,
        "w_in": nrm(ks[6], (DEPTH, D_MODEL, in_cols), f32) * (D_MODEL ** -0.5) * col_scale,
        "w_out": nrm(ks[7], (DEPTH, MIX_WIDTH, D_MODEL), f32) * (MIX_WIDTH ** -0.5) * DEEPNORM_BETA,
        "gn_a": 1.0 + 0.1 * nrm(ks[8], (DEPTH, WIDTH_A), f32),
        "gn_b": 1.0 + 0.1 * nrm(ks[9], (DEPTH, WIDTH_B), f32),
        "rel_bias": 0.2 * nrm(ks[10], (DEPTH, N_HEADS_B, 2 * MAX_REL + 1), f32),
        "ln1_g": 1.0 + 0.1 * nrm(ks[11], (DEPTH, D_MODEL), f32),
        "ln1_b": 0.02 * nrm(ks[12], (DEPTH, D_MODEL), f32),
        "peer_query": nrm(ks[13], (DEPTH, D_MODEL, PEER_HEADS * PEER_KEY_DIM), f32) * (D_MODEL ** -0.5),
        "peer_subkeys": nrm(ks[14], (DEPTH, 2, N_KEYS, PEER_KEY_DIM // 2), f32) * ((PEER_KEY_DIM // 2) ** -0.5),
        "peer_u": nrm(ks[15], (DEPTH, N_EXPERTS, D_MODEL), f32) * (D_MODEL ** -0.5) * DEEPNORM_BETA,
        "peer_v": nrm(ks[16], (DEPTH, N_EXPERTS, D_MODEL), f32) * DEEPNORM_BETA,
        "ln2_g": 1.0 + 0.1 * nrm(ks[17], (DEPTH, D_MODEL), f32),
        "ln2_b": 0.02 * nrm(ks[18], (DEPTH, D_MODEL), f32),
    }


def reference(x_prompt, x_sample, cache_sb_k, cache_sb_v, cache_band_k, cache_band_v,
              w_in, w_out, gn_a, gn_b, rel_bias, ln1_g, ln1_b,
              peer_query, peer_subkeys, peer_u, peer_v, ln2_g, ln2_b):
    xp, xs = x_prompt, x_sample
    past = cache_sb_k.shape[2]
    nb_sample = cache_band_k.shape[2]
    nb_prompt = min(BAND_PAST, xp.shape[1])
    p_sb_k, p_sb_v, p_bd_k, p_bd_v = [], [], [], []
    s_sb_k, s_sb_v, s_bd_k, s_bd_v = [], [], [], []
    for l in range(DEPTH):
        qa, ka, va, qb, kb, vb = project_heads(xp, w_in[l])
        oa = sb_attn_prompt(qa, ka, va)
        ob = band_attn_prompt(qb, kb, vb, rel_bias[l])
        xp = layer_norm(DEEPNORM_ALPHA * xp + merge_heads(oa, ob, gn_a[l], gn_b[l], w_out[l]), ln1_g[l], ln1_b[l])
        xp = layer_norm(DEEPNORM_ALPHA * xp + peer_ffn(xp, peer_query[l], peer_subkeys[l], peer_u[l], peer_v[l]),
                        ln2_g[l], ln2_b[l])
        p_sb_k.append(ka)
        p_sb_v.append(va)
        p_bd_k.append(kb[:, -nb_prompt:])
        p_bd_v.append(vb[:, -nb_prompt:])
        qa, ka, va, qb, kb, vb = project_heads(xs, w_in[l])
        ka_all = jnp.concatenate([cache_sb_k[l], ka], axis=1)
        va_all = jnp.concatenate([cache_sb_v[l], va], axis=1)
        oa = sb_attn_sample(qa, ka_all, va_all, past)
        kb_all = jnp.concatenate([cache_band_k[l], kb], axis=1)
        vb_all = jnp.concatenate([cache_band_v[l], vb], axis=1)
        ob = band_attn_sample(qb, kb_all, vb_all, rel_bias[l], nb_sample)
        xs = layer_norm(DEEPNORM_ALPHA * xs + merge_heads(oa, ob, gn_a[l], gn_b[l], w_out[l]), ln1_g[l], ln1_b[l])
        xs = layer_norm(DEEPNORM_ALPHA * xs + peer_ffn(xs, peer_query[l], peer_subkeys[l], peer_u[l], peer_v[l]),
                        ln2_g[l], ln2_b[l])
        s_sb_k.append(ka)
        s_sb_v.append(va)
        s_bd_k.append(kb_all[:, -nb_sample:])
        s_bd_v.append(vb_all[:, -nb_sample:])
    return (xp, xs,
            jnp.stack(p_sb_k), jnp.stack(p_sb_v), jnp.stack(p_bd_k), jnp.stack(p_bd_v),
            jnp.stack(s_sb_k), jnp.stack(s_sb_v), jnp.stack(s_bd_k), jnp.stack(s_bd_v))
```

```python
import functools

import jax
import jax.numpy as jnp
import numpy as np
from jax import lax
from jax.experimental import pallas as pl
from jax.experimental.pallas import tpu as pltpu

F32 = jnp.float32
BF16 = jnp.bfloat16

HEAD_DIM = 64
N_HEADS = 8
WIDTH = N_HEADS * HEAD_DIM
CHUNK = 64
BAND_CHUNKS = 8
BAND_PAST = BAND_CHUNKS * CHUNK
MAX_REL = 128
N_KEYS = 128
PEER_HEADS = 8
PEER_TOPK = 16
NORM_EPS = 1e-5
NEG_INF = -1e30

LANES = 128
VMEM_LIMIT_BYTES = 56 * 1024 * 1024

SB_DEAD_LOG = -104.0
SB_BLOCK = 128
BAND_QBLOCK = 256
ROW_TILE = 512
PEER_TOKEN_TILE = 512
PEER_EXPERT_BLOCK = 512
PEER_LANE_TILE = 256
PAD_SCORE = -3.0e38


def _params(*sem):
    return pltpu.CompilerParams(dimension_semantics=sem, vmem_limit_bytes=VMEM_LIMIT_BYTES)


def _nt_dot(a, b):
    return lax.dot_general(a, b, (((1,), (1,)), ((), ())), preferred_element_type=F32)


def _dot(a, b):
    return jnp.dot(a, b, preferred_element_type=F32)


def _layer_norm(y, g, b):
    mu = jnp.mean(y, axis=-1, keepdims=True)
    d = y - mu
    var = jnp.mean(d * d, axis=-1, keepdims=True)
    return d * lax.rsqrt(var + NORM_EPS) * g + b


def _head_rms(o, gain):
    ms = jnp.mean(o * o, axis=-1, keepdims=True)
    return o * lax.rsqrt(ms + NORM_EPS) * gain


def _proj_kernel(x_ref, w_ref, qa_ref, ka_ref, va_ref, kab_ref, vab_ref,
                 qb_ref, kb_ref, vb_ref, kbb_ref, vbb_ref):
    xb = x_ref[...].astype(BF16)
    scale = HEAD_DIM ** -0.5

    def group(g):
        return _dot(xb, w_ref[:, g * WIDTH:(g + 1) * WIDTH])

    qa_ref[...] = (group(0) * scale).astype(BF16)
    k = group(1)
    ka_ref[...] = k
    kab_ref[...] = k.astype(BF16)
    v = group(2)
    va_ref[...] = v
    vab_ref[...] = v.astype(BF16)
    qb_ref[...] = (group(3) * scale).astype(BF16)
    k = group(4)
    kb_ref[...] = k
    kbb_ref[...] = k.astype(BF16)
    v = group(5)
    vb_ref[...] = v
    vbb_ref[...] = v.astype(BF16)


def _project(x, w_bf):
    t, d = x.shape
    tm = min(ROW_TILE, t)
    assert t % tm == 0
    row = lambda i: (i, 0)
    f32o = jax.ShapeDtypeStruct((t, WIDTH), F32)
    bf16o = jax.ShapeDtypeStruct((t, WIDTH), BF16)
    blk = pl.BlockSpec((tm, WIDTH), row)
    return pl.pallas_call(
        _proj_kernel,
        grid=(t // tm,),
        in_specs=[pl.BlockSpec((tm, d), row), pl.BlockSpec(w_bf.shape, lambda i: (0, 0))],
        out_specs=[blk] * 10,
        out_shape=[bf16o, f32o, f32o, bf16o, bf16o, bf16o, f32o, f32o, bf16o, bf16o],
        compiler_params=_params("arbitrary"),
        name="proj",
    )(x, w_bf)


def _suffix_matrix(kb):
    kp = np.arange(kb)
    m = (kp[:, None] > kp[None, :]).astype(np.float32)
    one = np.concatenate([np.ones((kb, LANES), np.float32), m], axis=1)
    return jnp.asarray(np.concatenate([one, one], axis=0), dtype=BF16)


def _sb_block(z, mask, carry, mm):
    kb = z.shape[1]
    sp = jnp.maximum(z, 0.0) + jnp.log(1.0 + jnp.exp(-jnp.abs(z)))
    log_keep = -sp
    if mask is not None:
        log_keep = jnp.where(mask, log_keep, 0.0)
    hi = log_keep.astype(BF16)
    lo = (log_keep - hi.astype(F32)).astype(BF16)
    r = _dot(jnp.concatenate([hi, lo], axis=1), mm)
    total = r[:, :LANES]
    log_w = (z - sp) + r[:, LANES:LANES + kb]
    if carry is not None:
        log_w = log_w + carry
    w = jnp.exp(log_w)
    if mask is not None:
        w = jnp.where(mask, w, 0.0)
    return w, total


def _sb_prompt_kernel(q_ref, k_ref, v_ref, g_ref, mm_ref, o_ref):
    i = pl.program_id(1)
    r = q_ref.shape[0]
    row = lax.broadcasted_iota(jnp.int32, (r, r), 0)
    col = lax.broadcasted_iota(jnp.int32, (r, r), 1)
    diag = col < row
    mm = mm_ref[...]
    outs = []
    for h in range(N_HEADS):
        hs = slice(h * HEAD_DIM, (h + 1) * HEAD_DIM)
        qh = q_ref[:, hs]

        def block(j, carry, mask, hs=hs, qh=qh):
            off = pl.multiple_of(j * r, r)
            kj = k_ref[pl.ds(off, r), hs]
            vj = v_ref[pl.ds(off, r), hs]
            w, total = _sb_block(_nt_dot(qh, kj), mask, carry, mm)
            return _dot(w.astype(BF16), vj), total

        acc0, carry0 = block(i, None, diag)

        def cond(s):
            j, _, _, cmax = s
            return jnp.logical_and(j >= 0, cmax > SB_DEAD_LOG)

        def body(s, block=block):
            j, carry, acc, _ = s
            pv, total = block(j, carry, None)
            carry = carry + total
            return j - 1, carry, acc + pv, jnp.max(carry)

        _, _, acc, _ = lax.while_loop(cond, body, (i - 1, carry0, acc0, jnp.max(carry0)))
        outs.append(_head_rms(acc, g_ref[:, hs]))
    o_ref[...] = jnp.concatenate(outs, axis=1).astype(BF16)


def _sb_prompt(q, k, v, gain, batch, seq):
    r = SB_BLOCK
    assert seq % r == 0
    nq = seq // r
    return pl.pallas_call(
        _sb_prompt_kernel,
        grid=(batch, nq),
        in_specs=[
            pl.BlockSpec((r, WIDTH), lambda b, i: (b * nq + i, 0)),
            pl.BlockSpec((seq, WIDTH), lambda b, i: (b, 0)),
            pl.BlockSpec((seq, WIDTH), lambda b, i: (b, 0)),
            pl.BlockSpec((1, WIDTH), lambda b, i: (0, 0)),
            pl.BlockSpec((2 * r, LANES + r), lambda b, i: (0, 0)),
        ],
        out_specs=pl.BlockSpec((r, WIDTH), lambda b, i: (b * nq + i, 0)),
        out_shape=jax.ShapeDtypeStruct((batch * seq, WIDTH), BF16),
        compiler_params=_params("arbitrary", "arbitrary"),
        name="sb_prompt",
    )(q, k, v, gain, _suffix_matrix(r))


def _band_prompt_kernel(q_ref, k0_ref, k1_ref, k2_ref, v0_ref, v1_ref, v2_ref, bias_ref, g_ref, o_ref):
    i = pl.program_id(1)
    k_refs = (k0_ref, k1_ref, k2_ref)
    v_refs = (v0_ref, v1_ref, v2_ref)
    outs = []
    for h in range(N_HEADS):
        hs = slice(h * HEAD_DIM, (h + 1) * HEAD_DIM)
        qh = q_ref[:, hs]
        scs = []
        for w in range(3):
            sc = _nt_dot(qh, k_refs[w][:, hs]) + bias_ref[h, w]
            if w < 2:
                sc = jnp.where(i >= 2 - w, sc, NEG_INF)
            scs.append(sc)
        m = jnp.maximum(jnp.maximum(jnp.max(scs[0], axis=-1, keepdims=True),
                                    jnp.max(scs[1], axis=-1, keepdims=True)),
                        jnp.max(scs[2], axis=-1, keepdims=True))
        es = [jnp.exp(sc - m) for sc in scs]
        den = (jnp.sum(es[0], axis=-1, keepdims=True) + jnp.sum(es[1], axis=-1, keepdims=True)
               + jnp.sum(es[2], axis=-1, keepdims=True))
        acc = (_dot(es[0].astype(BF16), v_refs[0][:, hs]) + _dot(es[1].astype(BF16), v_refs[1][:, hs])
               + _dot(es[2].astype(BF16), v_refs[2][:, hs]))
        outs.append(_head_rms(acc / den, g_ref[:, hs]))
    o_ref[...] = jnp.concatenate(outs, axis=1).astype(BF16)


def _band_prompt_bias(rel_bias):
    qb = BAND_QBLOCK
    ql = jnp.arange(qb)[:, None]
    kl = jnp.arange(3 * qb)[None, :]
    dist = ql + 2 * qb - kl
    bias = rel_bias[:, jnp.clip(dist, -MAX_REL, MAX_REL) + MAX_REL].astype(F32)
    kc = kl // CHUNK - (2 * qb // CHUNK)
    qc = ql // CHUNK
    valid = (kc <= qc) & (kc >= qc - BAND_CHUNKS)
    bias = jnp.where(valid[None], bias, NEG_INF)
    return bias.reshape(N_HEADS, qb, 3, qb).transpose(0, 2, 1, 3)


def _band_prompt(q, k, v, bias, gain, batch, seq):
    qb = BAND_QBLOCK
    assert seq % qb == 0 and 2 * qb == BAND_PAST and qb % CHUNK == 0
    nq = seq // qb
    kspec = lambda back: pl.BlockSpec((qb, WIDTH), lambda b, i: (b * nq + jnp.maximum(i - back, 0), 0))
    return pl.pallas_call(
        _band_prompt_kernel,
        grid=(batch, nq),
        in_specs=[
            pl.BlockSpec((qb, WIDTH), lambda b, i: (b * nq + i, 0)),
            kspec(2), kspec(1), kspec(0), kspec(2), kspec(1), kspec(0),
            pl.BlockSpec(bias.shape, lambda b, i: (0, 0, 0, 0)),
            pl.BlockSpec((1, WIDTH), lambda b, i: (0, 0)),
        ],
        out_specs=pl.BlockSpec((qb, WIDTH), lambda b, i: (b * nq + i, 0)),
        out_shape=jax.ShapeDtypeStruct((batch * seq, WIDTH), BF16),
        compiler_params=_params("arbitrary", "arbitrary"),
        name="band_prompt",
    )(q, k, k, k, v, v, v, bias, gain)


def _sample_attn_kernel(qa_ref, kan_ref, van_ref, cak_ref, cav_ref,
                        qb_ref, kbn_ref, vbn_ref, kbf_ref, vbf_ref, cbk_ref, cbv_ref,
                        biasp_ref, biasn_ref, ga_ref, gb_ref, mmn_ref, mmp_ref,
                        oa_ref, ob_ref, nbk_ref, nbv_ref):
    sd = qa_ref.shape[0]
    past = cak_ref.shape[1]
    nb = cbk_ref.shape[1]
    row = lax.broadcasted_iota(jnp.int32, (sd, sd), 0)
    col = lax.broadcasted_iota(jnp.int32, (sd, sd), 1)
    diag = col < row
    cak = cak_ref[0].astype(BF16)
    cav = cav_ref[0].astype(BF16)
    cbk = cbk_ref[0].astype(BF16)
    cbv = cbv_ref[0].astype(BF16)
    mmn = mmn_ref[...]
    mmp = mmp_ref[...]
    outs_a, outs_b = [], []
    for h in range(N_HEADS):
        hs = slice(h * HEAD_DIM, (h + 1) * HEAD_DIM)
        qh = qa_ref[:, hs]
        w, carry = _sb_block(_nt_dot(qh, kan_ref[:, hs]), diag, None, mmn)
        acc = _dot(w.astype(BF16), van_ref[:, hs])
        for j in range(past // SB_BLOCK - 1, -1, -1):
            ks = slice(j * SB_BLOCK, (j + 1) * SB_BLOCK)
            w, total = _sb_block(_nt_dot(qh, cak[ks, hs]), None, carry, mmp)
            acc = acc + _dot(w.astype(BF16), cav[ks, hs])
            carry = carry + total
        outs_a.append(_head_rms(acc, ga_ref[:, hs]))
        qh = qb_ref[:, hs]
        sp = _nt_dot(qh, cbk[:, hs]) + biasp_ref[h]
        sn = _nt_dot(qh, kbn_ref[:, hs]) + biasn_ref[h]
        m = jnp.maximum(jnp.max(sp, axis=-1, keepdims=True), jnp.max(sn, axis=-1, keepdims=True))
        ep = jnp.exp(sp - m)
        en = jnp.exp(sn - m)
        den = jnp.sum(ep, axis=-1, keepdims=True) + jnp.sum(en, axis=-1, keepdims=True)
        acc = _dot(ep.astype(BF16), cbv[:, hs]) + _dot(en.astype(BF16), vbn_ref[:, hs])
        outs_b.append(_head_rms(acc / den, gb_ref[:, hs]))
    oa_ref[...] = jnp.concatenate(outs_a, axis=1).astype(BF16)
    ob_ref[...] = jnp.concatenate(outs_b, axis=1).astype(BF16)
    nbk_ref[0, :nb - sd, :] = cbk_ref[0, sd:, :]
    nbk_ref[0, nb - sd:, :] = kbf_ref[...]
    nbv_ref[0, :nb - sd, :] = cbv_ref[0, sd:, :]
    nbv_ref[0, nb - sd:, :] = vbf_ref[...]


def _sample_bias(rel_bias, sd, nb):
    dist = jnp.arange(sd)[:, None] + nb - jnp.arange(nb + sd)[None, :]
    bias = rel_bias[:, jnp.clip(dist, -MAX_REL, MAX_REL) + MAX_REL].astype(F32)
    return bias[:, :, :nb], bias[:, :, nb:]


def _sample_attn(qa, kan, van, cak, cav, qb, kbn, vbn, kbf, vbf, cbk, cbv, biasp, biasn, ga, gb):
    nbatch, past, _ = cak.shape
    nb = cbk.shape[1]
    sd = qa.shape[0] // nbatch
    assert past % SB_BLOCK == 0 and sd % 8 == 0 and sd <= nb and sd <= LANES
    rows = pl.BlockSpec((sd, WIDTH), lambda b: (b, 0))
    cache = lambda n: pl.BlockSpec((1, n, WIDTH), lambda b: (b, 0, 0))
    full = lambda a: pl.BlockSpec(a.shape, lambda b: (0,) * a.ndim)
    mmn = _suffix_matrix(sd)
    mmp = _suffix_matrix(SB_BLOCK)
    return pl.pallas_call(
        _sample_attn_kernel,
        grid=(nbatch,),
        in_specs=[rows, rows, rows, cache(past), cache(past),
                  rows, rows, rows, rows, rows, cache(nb), cache(nb),
                  full(biasp), full(biasn), full(ga), full(gb), full(mmn), full(mmp)],
        out_specs=[rows, rows, cache(nb), cache(nb)],
        out_shape=[jax.ShapeDtypeStruct((nbatch * sd, WIDTH), BF16)] * 2
                  + [jax.ShapeDtypeStruct((nbatch, nb, WIDTH), F32)] * 2,
        compiler_params=_params("arbitrary"),
        name="sample_attn",
    )(qa, kan, van, cak, cav, qb, kbn, vbn, kbf, vbf, cbk, cbv, biasp, biasn, ga, gb, mmn, mmp)


def _merge_kernel(ca_ref, cb_ref, x_ref, woa_ref, wob_ref, g_ref, b_ref, wq_ref, sk_ref,
                  x1_ref, x1t_ref, st_ref, *, alpha):
    mix = _dot(ca_ref[...], woa_ref[...]) + _dot(cb_ref[...], wob_ref[...])
    x1 = _layer_norm(alpha * x_ref[...] + mix, g_ref[...], b_ref[...])
    x1_ref[...] = x1
    x1t_ref[...] = x1.T.astype(BF16)
    qp = _dot(x1.astype(BF16), wq_ref[...])
    half = sk_ref.shape[2]
    for h in range(PEER_HEADS):
        for j in range(2):
            c0 = (2 * h + j) * half
            qh = qp[:, c0:c0 + half].astype(BF16)
            st_ref[(2 * h + j) * N_KEYS:(2 * h + j + 1) * N_KEYS, :] = _nt_dot(sk_ref[j], qh)


def _merge(ca, cb, x, woa, wob, g, b, wq, sk, alpha):
    t, d = x.shape
    tm = min(ROW_TILE, t)
    assert t % tm == 0
    nscore = PEER_HEADS * 2 * N_KEYS
    row = lambda i: (i, 0)
    colb = lambda i: (0, i)
    full = lambda a: pl.BlockSpec(a.shape, lambda i: (0,) * a.ndim)
    return pl.pallas_call(
        functools.partial(_merge_kernel, alpha=alpha),
        grid=(t // tm,),
        in_specs=[pl.BlockSpec((tm, WIDTH), row), pl.BlockSpec((tm, WIDTH), row), pl.BlockSpec((tm, d), row),
                  full(woa), full(wob), full(g), full(b), full(wq), full(sk)],
        out_specs=[pl.BlockSpec((tm, d), row), pl.BlockSpec((d, tm), colb), pl.BlockSpec((nscore, tm), colb)],
        out_shape=[jax.ShapeDtypeStruct((t, d), F32), jax.ShapeDtypeStruct((d, t), BF16),
                   jax.ShapeDtypeStruct((nscore, t), F32)],
        compiler_params=_params("arbitrary"),
        name="merge",
    )(ca, cb, x, woa, wob, g, b, wq, sk)


def _cmp_exchange(v, i, j):
    a, b = v[i], v[j]
    v[i] = jnp.maximum(a, b)
    v[j] = jnp.minimum(a, b)


def _sort16_desc(v):
    v = list(v)
    n = len(v)
    k = 2
    while k <= n:
        j = k // 2
        while j >= 1:
            for i in range(n):
                l = i ^ j
                if l > i:
                    if (i & k) == 0:
                        _cmp_exchange(v, i, l)
                    else:
                        _cmp_exchange(v, l, i)
            j //= 2
        k *= 2
    return v


def _merge_top16(a, b):
    n = len(a)
    top = [jnp.maximum(a[i], b[n - 1 - i]) for i in range(n)]
    out = [jnp.minimum(a[i], b[n - 1 - i]) for i in range(n)]
    while len(out) > 1:
        out = [jnp.maximum(out[2 * i], out[2 * i + 1]) for i in range(len(out) // 2)]
    j = n // 2
    while j >= 1:
        for i in range(n):
            if (i & j) == 0:
                _cmp_exchange(top, i, i + j)
        j //= 2
    return top, out[0]


def _top16_and_next(vals):
    groups = [_sort16_desc(vals[g:g + PEER_TOPK]) for g in range(0, len(vals), PEER_TOPK)]
    nxt = None
    while len(groups) > 1:
        merged = []
        for g in range(0, len(groups), 2):
            top, left = _merge_top16(groups[g], groups[g + 1])
            merged.append(top)
            nxt = left if nxt is None else jnp.maximum(nxt, left)
        groups = merged
    return groups[0], nxt


def _gelu_tanh(x):
    return 0.5 * x * (1.0 + jnp.tanh(0.7978845608028654 * (x + 0.044715 * (x * x * x))))


def _peer_kernel(st_ref, xt_ref, u_ref, vt_ref, x1_ref, g_ref, b_ref, o_ref,
                 eb_s, th_s, ea_s, row_s, ht_s, wt_s, acc_s, *, alpha):
    c = pl.program_id(1)
    nc = pl.num_programs(1)
    tt = xt_ref.shape[1]
    ng = tt // LANES
    ch = u_ref.shape[0]
    k1 = PEER_TOPK + 1

    @pl.when(c == 0)
    def _thresholds():
        def head(h, carry):
            base = pl.multiple_of(h * 2 * N_KEYS, 2 * N_KEYS)
            s1 = st_ref[pl.ds(base, N_KEYS), :]
            s2 = st_ref[pl.ds(base + N_KEYS, N_KEYS), :]
            s1r = s1.reshape(N_KEYS, ng, LANES)
            s2r = s2.reshape(N_KEYS, ng, LANES)
            top_a, next_a = _top16_and_next([s1r[a] for a in range(N_KEYS)])
            top_b, next_b = _top16_and_next([s2r[a] for a in range(N_KEYS)])
            la = top_a + [next_a]
            lb = top_b + [next_b]
            cands = [la[i - 1] + lb[j - 1] for i in range(1, k1 + 1) for j in range(1, k1 + 1) if i * j <= k1]
            pad = jnp.full_like(cands[0], PAD_SCORE)
            cands = cands + [pad] * (-len(cands) % PEER_TOPK)
            top_c, next_c = _top16_and_next(cands)
            tau = 0.5 * (top_c[PEER_TOPK - 1] + next_c)
            den = jnp.ones_like(tau)
            for cv in top_c[1:]:
                den = den + jnp.exp(cv - top_c[0])
            vals = (tau, la[0], lb[0], 1.0 / den)
            for r, val in enumerate(vals):
                for g in range(ng):
                    row_s[r:r + 1, g * LANES:(g + 1) * LANES] = val[g:g + 1, :]
            tau_r = row_s[0:1, :]
            m1_r = row_s[1:2, :]
            m2_r = row_s[2:3, :]
            iz_r = row_s[3:4, :]
            th_s[h] = tau_r - s1
            ea_s[h] = jnp.exp(s1 - m1_r) * iz_r
            eb_s[h] = jnp.exp(s2 - m2_r)
            return carry

        lax.fori_loop(0, PEER_HEADS, head, 0)

    ht_s[...] = _dot(u_ref[...], xt_ref[...])
    a0 = c * (ch // N_KEYS)
    lt = min(PEER_LANE_TILE, tt)

    def first_key(al, carry):
        a = a0 + al
        roff = pl.multiple_of(al * N_KEYS, N_KEYS)
        for ts in range(tt // lt):
            ls = slice(ts * lt, (ts + 1) * lt)
            gate = jnp.zeros((N_KEYS, lt), F32)
            for h in range(PEER_HEADS):
                th = th_s[h, pl.ds(a, 1), ls]
                ea = ea_s[h, pl.ds(a, 1), ls]
                s2 = st_ref[(2 * h + 1) * N_KEYS:(2 * h + 2) * N_KEYS, ls]
                gate = gate + jnp.where(s2 >= th, eb_s[h, :, ls], 0.0) * ea
            act = _gelu_tanh(ht_s[pl.ds(roff, N_KEYS), ls])
            wt_s[pl.ds(roff, N_KEYS), ls] = (gate * act).astype(BF16)
        return carry

    lax.fori_loop(0, ch // N_KEYS, first_key, 0)
    contrib = _dot(vt_ref[...], wt_s[...])

    @pl.when(c == 0)
    def _init():
        acc_s[...] = contrib

    @pl.when(c > 0)
    def _accumulate():
        acc_s[...] += contrib

    @pl.when(c == nc - 1)
    def _finish():
        y = alpha * x1_ref[...] + acc_s[...].T
        o_ref[...] = _layer_norm(y, g_ref[...], b_ref[...])


def _peer(st, xt, u_bf, vt_bf, x1, g, b, alpha, tt):
    t, d = x1.shape
    ne = u_bf.shape[0]
    ch = PEER_EXPERT_BLOCK
    assert t % tt == 0 and tt % LANES == 0 and ne % ch == 0 and ne == N_KEYS * N_KEYS
    nscore = st.shape[0]
    return pl.pallas_call(
        functools.partial(_peer_kernel, alpha=alpha),
        grid=(t // tt, ne // ch),
        in_specs=[
            pl.BlockSpec((nscore, tt), lambda i, c: (0, i)),
            pl.BlockSpec((d, tt), lambda i, c: (0, i)),
            pl.BlockSpec((ch, d), lambda i, c: (c, 0)),
            pl.BlockSpec((d, ch), lambda i, c: (0, c)),
            pl.BlockSpec((tt, d), lambda i, c: (i, 0)),
            pl.BlockSpec((1, d), lambda i, c: (0, 0)),
            pl.BlockSpec((1, d), lambda i, c: (0, 0)),
        ],
        out_specs=pl.BlockSpec((tt, d), lambda i, c: (i, 0)),
        out_shape=jax.ShapeDtypeStruct((t, d), F32),
        scratch_shapes=[
            pltpu.VMEM((PEER_HEADS, N_KEYS, tt), F32),
            pltpu.VMEM((PEER_HEADS, N_KEYS, tt), F32),
            pltpu.VMEM((PEER_HEADS, N_KEYS, tt), F32),
            pltpu.VMEM((8, tt), F32),
            pltpu.VMEM((ch, tt), F32),
            pltpu.VMEM((ch, tt), BF16),
            pltpu.VMEM((d, tt), F32),
        ],
        compiler_params=_params("arbitrary", "arbitrary"),
        name="peer",
    )(st, xt, u_bf, vt_bf, x1, g, b)


def kernel(x_prompt, x_sample, cache_sb_k, cache_sb_v, cache_band_k, cache_band_v, w_in, w_out, gn_a, gn_b,
           rel_bias, ln1_g, ln1_b, peer_query, peer_subkeys, peer_u, peer_v, ln2_g, ln2_b):
    batch, seq, d = x_prompt.shape
    dec_batch, dec_seq, _ = x_sample.shape
    depth = w_in.shape[0]
    past = cache_sb_k.shape[2]
    nb = cache_band_k.shape[2]
    assert w_in.shape[2] == 6 * WIDTH and w_out.shape[1] == 2 * WIDTH
    assert seq >= BAND_PAST and nb == BAND_PAST
    alpha = float((2 * depth) ** 0.25)
    tp = batch * seq
    ts = dec_batch * dec_seq

    xp = x_prompt.reshape(tp, d)
    xs = x_sample.reshape(ts, d)
    row2 = lambda a: a.reshape(1, -1)
    heads = lambda a, n, s: a.reshape(n, s, N_HEADS, HEAD_DIM)
    outs = [[] for _ in range(8)]
    for l in range(depth):
        w_in_bf = w_in[l].astype(BF16)
        woa = w_out[l, :WIDTH].astype(BF16)
        wob = w_out[l, WIDTH:].astype(BF16)
        wq = peer_query[l].astype(BF16)
        sk = peer_subkeys[l].astype(BF16)
        u_bf = peer_u[l].astype(BF16)
        vt_bf = peer_v[l].T.astype(BF16)
        ga, gb = row2(gn_a[l]), row2(gn_b[l])
        g1, b1, g2, b2 = row2(ln1_g[l]), row2(ln1_b[l]), row2(ln2_g[l]), row2(ln2_b[l])

        qa, ka, va, kab, vab, qb, kb, vb, kbb, vbb = _project(xp, w_in_bf)
        ca = _sb_prompt(qa, kab, vab, ga, batch, seq)
        cb = _band_prompt(qb, kbb, vbb, _band_prompt_bias(rel_bias[l]), gb, batch, seq)
        x1, x1t, st = _merge(ca, cb, xp, woa, wob, g1, b1, wq, sk, alpha)
        xp = _peer(st, x1t, u_bf, vt_bf, x1, g2, b2, alpha, min(PEER_TOKEN_TILE, tp))
        outs[0].append(heads(ka, batch, seq))
        outs[1].append(heads(va, batch, seq))
        outs[2].append(heads(kb, batch, seq)[:, seq - BAND_PAST:])
        outs[3].append(heads(vb, batch, seq)[:, seq - BAND_PAST:])

        qa, ka, va, kab, vab, qb, kb, vb, kbb, vbb = _project(xs, w_in_bf)
        biasp, biasn = _sample_bias(rel_bias[l], dec_seq, nb)
        ca, cb, nbk, nbv = _sample_attn(
            qa, kab, vab, cache_sb_k[l].reshape(dec_batch, past, WIDTH), cache_sb_v[l].reshape(dec_batch, past, WIDTH),
            qb, kbb, vbb, kb, vb, cache_band_k[l].reshape(dec_batch, nb, WIDTH),
            cache_band_v[l].reshape(dec_batch, nb, WIDTH), biasp, biasn, ga, gb)
        x1, x1t, st = _merge(ca, cb, xs, woa, wob, g1, b1, wq, sk, alpha)
        xs = _peer(st, x1t, u_bf, vt_bf, x1, g2, b2, alpha, min(PEER_TOKEN_TILE, ts))
        outs[4].append(heads(ka, dec_batch, dec_seq))
        outs[5].append(heads(va, dec_batch, dec_seq))
        outs[6].append(heads(nbk, dec_batch, nb))
        outs[7].append(heads(nbv, dec_batch, nb))

    return (xp.reshape(batch, seq, d), xs.reshape(dec_batch, dec_seq, d)) + tuple(jnp.stack(o) for o in outs)
```

```python
import functools

import jax
import jax.numpy as jnp
import numpy as np
from jax import lax
from jax.experimental import pallas as pl
from jax.experimental.pallas import tpu as pltpu

F32 = jnp.float32
BF16 = jnp.bfloat16

HEAD_DIM = 64
N_HEADS = 8
WIDTH = N_HEADS * HEAD_DIM
CHUNK = 64
BAND_CHUNKS = 8
BAND_PAST = BAND_CHUNKS * CHUNK
MAX_REL = 128
N_KEYS = 128
PEER_HEADS = 8
PEER_TOPK = 16
NORM_EPS = 1e-5
NEG_INF = -1e30

LANES = 128
SUBLANES = 8
MXU_DEPTH = 256
VMEM_LIMIT_BYTES = 56 * 1024 * 1024

SB_DEAD_LOG = -104.0
SB_BLOCK = 128
BAND_QBLOCK = 256
ROW_TILE = 512
PEER_TOKEN_TILE = 512
PEER_EXPERT_BLOCK = 512
PEER_PIPE_FILL = 2
PEER_LANE_TILE = 256
PAD_SCORE = -3.0e38


def _params(*sem):
    return pltpu.CompilerParams(dimension_semantics=sem, vmem_limit_bytes=VMEM_LIMIT_BYTES)


def _nt_dot(a, b):
    return lax.dot_general(a, b, (((1,), (1,)), ((), ())), preferred_element_type=F32)


def _dot(a, b):
    return jnp.dot(a, b, preferred_element_type=F32)


def _layer_norm(y, g, b):
    mu = jnp.mean(y, axis=-1, keepdims=True)
    d = y - mu
    var = jnp.mean(d * d, axis=-1, keepdims=True)
    return d * lax.rsqrt(var + NORM_EPS) * g + b


def _head_rms(o, gain):
    ms = jnp.mean(o * o, axis=-1, keepdims=True)
    return o * lax.rsqrt(ms + NORM_EPS) * gain


def _proj_kernel(x_ref, w_ref, qa_ref, ka_ref, va_ref, kab_ref, vab_ref,
                 qb_ref, kb_ref, vb_ref, kbb_ref, vbb_ref):
    xb = x_ref[...].astype(BF16)
    scale = HEAD_DIM ** -0.5

    def group(g):
        return _dot(xb, w_ref[:, g * WIDTH:(g + 1) * WIDTH])

    qa_ref[...] = (group(0) * scale).astype(BF16)
    k = group(1)
    ka_ref[...] = k
    kab_ref[...] = k.astype(BF16)
    v = group(2)
    va_ref[...] = v
    vab_ref[...] = v.astype(BF16)
    qb_ref[...] = (group(3) * scale).astype(BF16)
    k = group(4)
    kb_ref[...] = k
    kbb_ref[...] = k.astype(BF16)
    v = group(5)
    vb_ref[...] = v
    vbb_ref[...] = v.astype(BF16)


def _project(x, w_bf):
    t, d = x.shape
    tm = min(ROW_TILE, t)
    assert t % tm == 0
    row = lambda i: (i, 0)
    f32o = jax.ShapeDtypeStruct((t, WIDTH), F32)
    bf16o = jax.ShapeDtypeStruct((t, WIDTH), BF16)
    blk = pl.BlockSpec((tm, WIDTH), row)
    return pl.pallas_call(
        _proj_kernel,
        grid=(t // tm,),
        in_specs=[pl.BlockSpec((tm, d), row), pl.BlockSpec(w_bf.shape, lambda i: (0, 0))],
        out_specs=[blk] * 10,
        out_shape=[bf16o, f32o, f32o, bf16o, bf16o, bf16o, f32o, f32o, bf16o, bf16o],
        compiler_params=_params("arbitrary"),
        name="proj",
    )(x, w_bf)


def _suffix_matrix(kb):
    kp = np.arange(kb)
    m = (kp[:, None] > kp[None, :]).astype(np.float32)
    one = np.concatenate([np.ones((kb, LANES), np.float32), m], axis=1)
    return jnp.asarray(np.concatenate([one, one], axis=0), dtype=BF16)


def _sb_block(z, mask, carry, mm):
    kb = z.shape[1]
    sp = jnp.maximum(z, 0.0) + jnp.log(1.0 + jnp.exp(-jnp.abs(z)))
    log_keep = -sp
    if mask is not None:
        log_keep = jnp.where(mask, log_keep, 0.0)
    hi = log_keep.astype(BF16)
    lo = (log_keep - hi.astype(F32)).astype(BF16)
    r = _dot(jnp.concatenate([hi, lo], axis=1), mm)
    total = r[:, :LANES]
    log_w = (z - sp) + r[:, LANES:LANES + kb]
    if carry is not None:
        log_w = log_w + carry
    w = jnp.exp(log_w)
    if mask is not None:
        w = jnp.where(mask, w, 0.0)
    return w, total


def _sb_prompt_kernel(q_ref, k_ref, v_ref, g_ref, mm_ref, o_ref):
    i = pl.program_id(1)
    r = q_ref.shape[0]
    row = lax.broadcasted_iota(jnp.int32, (r, r), 0)
    col = lax.broadcasted_iota(jnp.int32, (r, r), 1)
    diag = col < row
    mm = mm_ref[...]
    outs = []
    for h in range(N_HEADS):
        hs = slice(h * HEAD_DIM, (h + 1) * HEAD_DIM)
        qh = q_ref[:, hs]

        def block(j, carry, mask, hs=hs, qh=qh):
            off = pl.multiple_of(j * r, r)
            kj = k_ref[pl.ds(off, r), hs]
            vj = v_ref[pl.ds(off, r), hs]
            w, total = _sb_block(_nt_dot(qh, kj), mask, carry, mm)
            return _dot(w.astype(BF16), vj), total

        acc0, carry0 = block(i, None, diag)

        def cond(s):
            j, _, _, cmax = s
            return jnp.logical_and(j >= 0, cmax > SB_DEAD_LOG)

        def body(s, block=block):
            j, carry, acc, _ = s
            pv, total = block(j, carry, None)
            carry = carry + total
            return j - 1, carry, acc + pv, jnp.max(carry)

        _, _, acc, _ = lax.while_loop(cond, body, (i - 1, carry0, acc0, jnp.max(carry0)))
        outs.append(_head_rms(acc, g_ref[:, hs]))
    o_ref[...] = jnp.concatenate(outs, axis=1).astype(BF16)


def _sb_prompt(q, k, v, gain, batch, seq):
    r = SB_BLOCK
    assert seq % r == 0
    nq = seq // r
    return pl.pallas_call(
        _sb_prompt_kernel,
        grid=(batch, nq),
        in_specs=[
            pl.BlockSpec((r, WIDTH), lambda b, i: (b * nq + i, 0)),
            pl.BlockSpec((seq, WIDTH), lambda b, i: (b, 0)),
            pl.BlockSpec((seq, WIDTH), lambda b, i: (b, 0)),
            pl.BlockSpec((1, WIDTH), lambda b, i: (0, 0)),
            pl.BlockSpec((2 * r, LANES + r), lambda b, i: (0, 0)),
        ],
        out_specs=pl.BlockSpec((r, WIDTH), lambda b, i: (b * nq + i, 0)),
        out_shape=jax.ShapeDtypeStruct((batch * seq, WIDTH), BF16),
        compiler_params=_params("arbitrary", "arbitrary"),
        name="sb_prompt",
    )(q, k, v, gain, _suffix_matrix(r))


def _band_prompt_kernel(q_ref, k0_ref, k1_ref, k2_ref, v0_ref, v1_ref, v2_ref, bias_ref, g_ref, o_ref):
    i = pl.program_id(1)
    k_refs = (k0_ref, k1_ref, k2_ref)
    v_refs = (v0_ref, v1_ref, v2_ref)
    outs = []
    for h in range(N_HEADS):
        hs = slice(h * HEAD_DIM, (h + 1) * HEAD_DIM)
        qh = q_ref[:, hs]
        scs = []
        for w in range(3):
            sc = _nt_dot(qh, k_refs[w][:, hs]) + bias_ref[h, w]
            if w < 2:
                sc = jnp.where(i >= 2 - w, sc, NEG_INF)
            scs.append(sc)
        m = jnp.maximum(jnp.maximum(jnp.max(scs[0], axis=-1, keepdims=True),
                                    jnp.max(scs[1], axis=-1, keepdims=True)),
                        jnp.max(scs[2], axis=-1, keepdims=True))
        es = [jnp.exp(sc - m) for sc in scs]
        den = (jnp.sum(es[0], axis=-1, keepdims=True) + jnp.sum(es[1], axis=-1, keepdims=True)
               + jnp.sum(es[2], axis=-1, keepdims=True))
        acc = (_dot(es[0].astype(BF16), v_refs[0][:, hs]) + _dot(es[1].astype(BF16), v_refs[1][:, hs])
               + _dot(es[2].astype(BF16), v_refs[2][:, hs]))
        outs.append(_head_rms(acc / den, g_ref[:, hs]))
    o_ref[...] = jnp.concatenate(outs, axis=1).astype(BF16)


def _toeplitz_bias(rel_bias, n, ncols, offset):
    period = n + ncols + 1
    m = jnp.arange(period)
    shift = jnp.where(m < ncols, m, m - period)
    vec = rel_bias[:, jnp.clip(offset - shift, -MAX_REL, MAX_REL) + MAX_REL].astype(F32)
    rows = jnp.tile(vec, (1, n))[:, :n * (period - 1)].reshape(rel_bias.shape[0], n, period - 1)
    return rows[:, :, :ncols]


def _band_prompt_bias(rel_bias):
    qb = BAND_QBLOCK
    bias = _toeplitz_bias(rel_bias, qb, 3 * qb, 2 * qb)
    kc = jnp.arange(3 * qb)[None, :] // CHUNK - (2 * qb // CHUNK)
    qc = jnp.arange(qb)[:, None] // CHUNK
    valid = (kc <= qc) & (kc >= qc - BAND_CHUNKS)
    bias = jnp.where(valid[None], bias, NEG_INF)
    return bias.reshape(N_HEADS, qb, 3, qb).transpose(0, 2, 1, 3)


def _band_prompt(q, k, v, bias, gain, batch, seq):
    qb = BAND_QBLOCK
    assert seq % qb == 0 and 2 * qb == BAND_PAST and qb % CHUNK == 0
    nq = seq // qb
    kspec = lambda back: pl.BlockSpec((qb, WIDTH), lambda b, i: (b * nq + jnp.maximum(i - back, 0), 0))
    return pl.pallas_call(
        _band_prompt_kernel,
        grid=(batch, nq),
        in_specs=[
            pl.BlockSpec((qb, WIDTH), lambda b, i: (b * nq + i, 0)),
            kspec(2), kspec(1), kspec(0), kspec(2), kspec(1), kspec(0),
            pl.BlockSpec(bias.shape, lambda b, i: (0, 0, 0, 0)),
            pl.BlockSpec((1, WIDTH), lambda b, i: (0, 0)),
        ],
        out_specs=pl.BlockSpec((qb, WIDTH), lambda b, i: (b * nq + i, 0)),
        out_shape=jax.ShapeDtypeStruct((batch * seq, WIDTH), BF16),
        compiler_params=_params("arbitrary", "arbitrary"),
        name="band_prompt",
    )(q, k, k, k, v, v, v, bias, gain)


def _sample_attn_kernel(qa_ref, kan_ref, van_ref, cak_ref, cav_ref,
                        qb_ref, kbn_ref, vbn_ref, kbf_ref, vbf_ref, cbk_ref, cbv_ref,
                        biasp_ref, biasn_ref, ga_ref, gb_ref, mmn_ref, mmp_ref,
                        oa_ref, ob_ref, nbk_ref, nbv_ref):
    sd = qa_ref.shape[0]
    past = cak_ref.shape[1]
    nb = cbk_ref.shape[1]
    row = lax.broadcasted_iota(jnp.int32, (sd, sd), 0)
    col = lax.broadcasted_iota(jnp.int32, (sd, sd), 1)
    diag = col < row
    cak = cak_ref[0].astype(BF16)
    cav = cav_ref[0].astype(BF16)
    cbk = cbk_ref[0].astype(BF16)
    cbv = cbv_ref[0].astype(BF16)
    mmn = mmn_ref[...]
    mmp = mmp_ref[...]
    outs_a, outs_b = [], []
    for h in range(N_HEADS):
        hs = slice(h * HEAD_DIM, (h + 1) * HEAD_DIM)
        qh = qa_ref[:, hs]
        w, carry = _sb_block(_nt_dot(qh, kan_ref[:, hs]), diag, None, mmn)
        acc = _dot(w.astype(BF16), van_ref[:, hs])
        for j in range(past // SB_BLOCK - 1, -1, -1):
            ks = slice(j * SB_BLOCK, (j + 1) * SB_BLOCK)
            w, total = _sb_block(_nt_dot(qh, cak[ks, hs]), None, carry, mmp)
            acc = acc + _dot(w.astype(BF16), cav[ks, hs])
            carry = carry + total
        outs_a.append(_head_rms(acc, ga_ref[:, hs]))
        qh = qb_ref[:, hs]
        sp = _nt_dot(qh, cbk[:, hs]) + biasp_ref[h]
        sn = _nt_dot(qh, kbn_ref[:, hs]) + biasn_ref[h]
        m = jnp.maximum(jnp.max(sp, axis=-1, keepdims=True), jnp.max(sn, axis=-1, keepdims=True))
        ep = jnp.exp(sp - m)
        en = jnp.exp(sn - m)
        den = jnp.sum(ep, axis=-1, keepdims=True) + jnp.sum(en, axis=-1, keepdims=True)
        acc = _dot(ep.astype(BF16), cbv[:, hs]) + _dot(en.astype(BF16), vbn_ref[:, hs])
        outs_b.append(_head_rms(acc / den, gb_ref[:, hs]))
    oa_ref[...] = jnp.concatenate(outs_a, axis=1).astype(BF16)
    ob_ref[...] = jnp.concatenate(outs_b, axis=1).astype(BF16)
    nbk_ref[0, :nb - sd, :] = cbk_ref[0, sd:, :]
    nbk_ref[0, nb - sd:, :] = kbf_ref[...]
    nbv_ref[0, :nb - sd, :] = cbv_ref[0, sd:, :]
    nbv_ref[0, nb - sd:, :] = vbf_ref[...]


def _sample_bias(rel_bias, sd, nb):
    bias = _toeplitz_bias(rel_bias, sd, nb + sd, nb)
    return bias[:, :, :nb], bias[:, :, nb:]


def _sample_attn(qa, kan, van, cak, cav, qb, kbn, vbn, kbf, vbf, cbk, cbv, biasp, biasn, ga, gb):
    nbatch, past, _ = cak.shape
    nb = cbk.shape[1]
    sd = qa.shape[0] // nbatch
    assert past % SB_BLOCK == 0 and sd % 8 == 0 and sd <= nb and sd <= LANES
    rows = pl.BlockSpec((sd, WIDTH), lambda b: (b, 0))
    cache = lambda n: pl.BlockSpec((1, n, WIDTH), lambda b: (b, 0, 0))
    full = lambda a: pl.BlockSpec(a.shape, lambda b: (0,) * a.ndim)
    mmn = _suffix_matrix(sd)
    mmp = _suffix_matrix(SB_BLOCK)
    return pl.pallas_call(
        _sample_attn_kernel,
        grid=(nbatch,),
        in_specs=[rows, rows, rows, cache(past), cache(past),
                  rows, rows, rows, rows, rows, cache(nb), cache(nb),
                  full(biasp), full(biasn), full(ga), full(gb), full(mmn), full(mmp)],
        out_specs=[rows, rows, cache(nb), cache(nb)],
        out_shape=[jax.ShapeDtypeStruct((nbatch * sd, WIDTH), BF16)] * 2
                  + [jax.ShapeDtypeStruct((nbatch, nb, WIDTH), F32)] * 2,
        compiler_params=_params("arbitrary"),
        name="sample_attn",
    )(qa, kan, van, cak, cav, qb, kbn, vbn, kbf, vbf, cbk, cbv, biasp, biasn, ga, gb, mmn, mmp)


def _merge_kernel(ca_ref, cb_ref, x_ref, woa_ref, wob_ref, g_ref, b_ref, wq_ref, sk_ref,
                  x1_ref, x1t_ref, st_ref, *, alpha):
    mix = _dot(ca_ref[...], woa_ref[...]) + _dot(cb_ref[...], wob_ref[...])
    x1 = _layer_norm(alpha * x_ref[...] + mix, g_ref[...], b_ref[...])
    x1_ref[...] = x1
    x1t_ref[...] = x1.T.astype(BF16)
    qp = _dot(x1.astype(BF16), wq_ref[...])
    half = sk_ref.shape[2]
    for h in range(PEER_HEADS):
        for j in range(2):
            c0 = (2 * h + j) * half
            qh = qp[:, c0:c0 + half].astype(BF16)
            st_ref[(2 * h + j) * N_KEYS:(2 * h + j + 1) * N_KEYS, :] = _nt_dot(sk_ref[j], qh)


def _merge(ca, cb, x, woa, wob, g, b, wq, sk, alpha):
    t, d = x.shape
    tm = min(ROW_TILE, t)
    assert t % tm == 0
    nscore = PEER_HEADS * 2 * N_KEYS
    row = lambda i: (i, 0)
    colb = lambda i: (0, i)
    full = lambda a: pl.BlockSpec(a.shape, lambda i: (0,) * a.ndim)
    return pl.pallas_call(
        functools.partial(_merge_kernel, alpha=alpha),
        grid=(t // tm,),
        in_specs=[pl.BlockSpec((tm, WIDTH), row), pl.BlockSpec((tm, WIDTH), row), pl.BlockSpec((tm, d), row),
                  full(woa), full(wob), full(g), full(b), full(wq), full(sk)],
        out_specs=[pl.BlockSpec((tm, d), row), pl.BlockSpec((d, tm), colb), pl.BlockSpec((nscore, tm), colb)],
        out_shape=[jax.ShapeDtypeStruct((t, d), F32), jax.ShapeDtypeStruct((d, t), BF16),
                   jax.ShapeDtypeStruct((nscore, t), F32)],
        compiler_params=_params("arbitrary"),
        name="merge",
    )(ca, cb, x, woa, wob, g, b, wq, sk)


def _cmp_exchange(v, i, j):
    a, b = v[i], v[j]
    v[i] = jnp.maximum(a, b)
    v[j] = jnp.minimum(a, b)


def _sort16_desc(v):
    v = list(v)
    n = len(v)
    k = 2
    while k <= n:
        j = k // 2
        while j >= 1:
            for i in range(n):
                l = i ^ j
                if l > i:
                    if (i & k) == 0:
                        _cmp_exchange(v, i, l)
                    else:
                        _cmp_exchange(v, l, i)
            j //= 2
        k *= 2
    return v


def _merge_top16(a, b):
    n = len(a)
    top = [jnp.maximum(a[i], b[n - 1 - i]) for i in range(n)]
    out = [jnp.minimum(a[i], b[n - 1 - i]) for i in range(n)]
    while len(out) > 1:
        out = [jnp.maximum(out[2 * i], out[2 * i + 1]) for i in range(len(out) // 2)]
    j = n // 2
    while j >= 1:
        for i in range(n):
            if (i & j) == 0:
                _cmp_exchange(top, i, i + j)
        j //= 2
    return top, out[0]


def _top16_and_next(vals):
    groups = [_sort16_desc(vals[g:g + PEER_TOPK]) for g in range(0, len(vals), PEER_TOPK)]
    nxt = None
    while len(groups) > 1:
        merged = []
        for g in range(0, len(groups), 2):
            top, left = _merge_top16(groups[g], groups[g + 1])
            merged.append(top)
            nxt = left if nxt is None else jnp.maximum(nxt, left)
        groups = merged
    return groups[0], nxt


def _gelu_tanh(x):
    return 0.5 * x * (1.0 + jnp.tanh(0.7978845608028654 * (x + 0.044715 * (x * x * x))))


def _peer_gate_kernel(st_ref, th_ref, ea_ref, eb_ref, row_s):
    tt = st_ref.shape[1]
    ng = tt // LANES
    k1 = PEER_TOPK + 1

    def head(h, carry):
        base = pl.multiple_of(h * 2 * N_KEYS, 2 * N_KEYS)
        s1 = st_ref[pl.ds(base, N_KEYS), :]
        s2 = st_ref[pl.ds(base + N_KEYS, N_KEYS), :]
        s1r = s1.reshape(N_KEYS, ng, LANES)
        s2r = s2.reshape(N_KEYS, ng, LANES)
        top_a, next_a = _top16_and_next([s1r[a] for a in range(N_KEYS)])
        top_b, next_b = _top16_and_next([s2r[a] for a in range(N_KEYS)])
        la = top_a + [next_a]
        lb = top_b + [next_b]
        cands = [la[i - 1] + lb[j - 1] for i in range(1, k1 + 1) for j in range(1, k1 + 1) if i * j <= k1]
        pad = jnp.full_like(cands[0], PAD_SCORE)
        cands = cands + [pad] * (-len(cands) % PEER_TOPK)
        top_c, next_c = _top16_and_next(cands)
        tau = 0.5 * (top_c[PEER_TOPK - 1] + next_c)
        den = jnp.ones_like(tau)
        for cv in top_c[1:]:
            den = den + jnp.exp(cv - top_c[0])
        vals = (tau, la[0], lb[0], 1.0 / den)
        for r, val in enumerate(vals):
            for g in range(ng):
                row_s[r:r + 1, g * LANES:(g + 1) * LANES] = val[g:g + 1, :]
        tau_r = row_s[0:1, :]
        m1_r = row_s[1:2, :]
        m2_r = row_s[2:3, :]
        iz_r = row_s[3:4, :]
        th_ref[:, pl.ds(h, 1), :] = (tau_r - s1)[:, None, :]
        ea_ref[:, pl.ds(h, 1), :] = (jnp.exp(s1 - m1_r) * iz_r)[:, None, :]
        eb = jnp.exp(s2 - m2_r)
        eb_rows = pl.ds(pl.multiple_of(h * N_KEYS, N_KEYS), N_KEYS)
        if tt > LANES:
            eb_ref[eb_rows, LANES:] = eb[:, :tt - LANES]
            eb_ref[eb_rows, :LANES] = eb[:, tt - LANES:]
        else:
            eb_ref[eb_rows, :] = eb
        return carry

    lax.fori_loop(0, PEER_HEADS, head, 0)


def _peer_gates(st, tt):
    nscore, t = st.shape
    assert t % tt == 0 and tt % LANES == 0
    rows = jax.ShapeDtypeStruct((N_KEYS, PEER_HEADS, t), F32)
    return pl.pallas_call(
        _peer_gate_kernel,
        grid=(t // tt,),
        in_specs=[pl.BlockSpec((nscore, tt), lambda i: (0, i))],
        out_specs=[pl.BlockSpec((N_KEYS, PEER_HEADS, tt), lambda i: (0, 0, i)),
                   pl.BlockSpec((N_KEYS, PEER_HEADS, tt), lambda i: (0, 0, i)),
                   pl.BlockSpec((PEER_HEADS * N_KEYS, tt), lambda i: (0, i))],
        out_shape=[rows, rows, jax.ShapeDtypeStruct((PEER_HEADS * N_KEYS, t), F32)],
        scratch_shapes=[pltpu.VMEM((SUBLANES, tt), F32)],
        compiler_params=_params("arbitrary"),
        name="peer_gates",
    )(st)


def _peer_kernel(st_ref, eb_ref, th_ref, ea_ref, xt_ref, u_ref, vt_ref, x1_ref, g_ref, b_ref, o_ref,
                 ht0_s, ht1_s, wt0_s, wt1_s, acc_s, *, alpha):
    c = pl.program_id(1)
    nc = pl.num_programs(1)
    tt = xt_ref.shape[1]
    ch = u_ref.shape[0]

    @pl.when(c == 0)
    def _clear():
        ht1_s[...] = jnp.zeros(ht1_s.shape, F32)
        wt0_s[...] = jnp.zeros(wt0_s.shape, BF16)
        wt1_s[...] = jnp.zeros(wt1_s.shape, BF16)
        acc_s[...] = jnp.zeros(acc_s.shape, F32)

    ablocks = ch // N_KEYS
    lt = PEER_LANE_TILE

    def stages(ht_new, ht_old, wt_new, wt_old):
        d = vt_ref.shape[0]
        npiece = ablocks // 2
        nlv = lt // LANES
        nr2 = N_KEYS // (2 * SUBLANES)
        assert nlv == 2
        for ts in range(tt // lt):
            ls = slice(ts * lt, (ts + 1) * lt)
            for unit in range(npiece * nlv):
                ap, tv = unit // nlv, unit % nlv
                piece = unit // 2
                if unit % 2 == 0:
                    rows = slice(piece * (ch // npiece), (piece + 1) * (ch // npiece))
                    lhs_ref, rhs_ref, kdim = u_ref, xt_ref, u_ref.shape[1]
                else:
                    rows = slice(piece * (d // npiece), (piece + 1) * (d // npiece))
                    lhs_ref, rhs_ref, kdim = vt_ref, wt_old, ch
                nk = kdim // MXU_DEPTH
                part = None
                g = ts * nlv + tv
                lv = slice(g * LANES, (g + 1) * LANES)
                ge = (g + 1) % (tt // LANES)
                le = slice(ge * LANES, (ge + 1) * LANES)
                als = (2 * ap, 2 * ap + 1)
                th = [[jnp.broadcast_to(th_ref[al, h:h + 1, lv], (SUBLANES, LANES)) for h in range(PEER_HEADS)] for al in als]
                ea = [[jnp.broadcast_to(ea_ref[al, h:h + 1, lv], (SUBLANES, LANES)) for h in range(PEER_HEADS)] for al in als]
                for r2 in range(nr2):
                    if r2 % (nr2 // nk) == 0:
                        kb = slice(r2 // (nr2 // nk) * MXU_DEPTH, (r2 // (nr2 // nk) + 1) * MXU_DEPTH)
                        dk = _dot(lhs_ref[rows, kb], rhs_ref[kb, ls])
                        part = dk if part is None else part + dk
                    gates = [[jnp.zeros((SUBLANES, LANES), F32) for _ in range(2)] for _ in als]
                    for k in range(2):
                        r = 2 * r2 + k
                        for h in range(PEER_HEADS):
                            s2 = st_ref[(2 * h + 1) * N_KEYS + r * SUBLANES:(2 * h + 1) * N_KEYS + (r + 1) * SUBLANES, lv]
                            eb = eb_ref[h * N_KEYS + r * SUBLANES:h * N_KEYS + (r + 1) * SUBLANES, le]
                            for i in range(2):
                                gates[i][k] = gates[i][k] + jnp.where(s2 >= th[i][h], eb, 0.0) * ea[i][h]
                    for i, al in enumerate(als):
                        rs = slice(al * N_KEYS + r2 * 2 * SUBLANES, al * N_KEYS + (r2 + 1) * 2 * SUBLANES)
                        gate = jnp.concatenate(gates[i], axis=0)
                        wt_new[rs, lv] = (gate * _gelu_tanh(ht_old[rs, lv])).astype(BF16)
                if unit % 2 == 0:
                    ht_new[rows, ls] = part
                else:
                    acc_s[rows, ls] += part

    @pl.when(c % 2 == 0)
    def _even():
        stages(ht0_s, ht1_s, wt1_s, wt0_s)

    @pl.when(c % 2 == 1)
    def _odd():
        stages(ht1_s, ht0_s, wt0_s, wt1_s)

    @pl.when(c == nc - 1)
    def _finish():
        y = alpha * x1_ref[...] + acc_s[...].T
        o_ref[...] = _layer_norm(y, g_ref[...], b_ref[...])


def _peer(st, eb, th, ea, xt, u_bf, vt_bf, x1, g, b, alpha, tt):
    t, d = x1.shape
    ne = u_bf.shape[0]
    ch = PEER_EXPERT_BLOCK
    assert t % tt == 0 and tt % PEER_LANE_TILE == 0 and ne % ch == 0 and ne == N_KEYS * N_KEYS
    nscore = st.shape[0]
    nblk = ne // ch
    ablocks = ch // N_KEYS
    gated = lambda i, c: (jnp.clip(c - 1, 0, nblk - 1), 0, i)
    return pl.pallas_call(
        functools.partial(_peer_kernel, alpha=alpha),
        grid=(t // tt, nblk + PEER_PIPE_FILL),
        in_specs=[
            pl.BlockSpec((nscore, tt), lambda i, c: (0, i)),
            pl.BlockSpec((PEER_HEADS * N_KEYS, tt), lambda i, c: (0, i)),
            pl.BlockSpec((ablocks, PEER_HEADS, tt), gated),
            pl.BlockSpec((ablocks, PEER_HEADS, tt), gated),
            pl.BlockSpec((d, tt), lambda i, c: (0, i)),
            pl.BlockSpec((ch, d), lambda i, c: (jnp.minimum(c, nblk - 1), 0)),
            pl.BlockSpec((d, ch), lambda i, c: (0, jnp.clip(c - PEER_PIPE_FILL, 0, nblk - 1))),
            pl.BlockSpec((tt, d), lambda i, c: (i, 0)),
            pl.BlockSpec((1, d), lambda i, c: (0, 0)),
            pl.BlockSpec((1, d), lambda i, c: (0, 0)),
        ],
        out_specs=pl.BlockSpec((tt, d), lambda i, c: (i, 0)),
        out_shape=jax.ShapeDtypeStruct((t, d), F32),
        scratch_shapes=[
            pltpu.VMEM((ch, tt), F32),
            pltpu.VMEM((ch, tt), F32),
            pltpu.VMEM((ch, tt), BF16),
            pltpu.VMEM((ch, tt), BF16),
            pltpu.VMEM((d, tt), F32),
        ],
        compiler_params=_params("arbitrary", "arbitrary"),
        name="peer",
    )(st, eb, th, ea, xt, u_bf, vt_bf, x1, g, b)


def kernel(x_prompt, x_sample, cache_sb_k, cache_sb_v, cache_band_k, cache_band_v, w_in, w_out, gn_a, gn_b,
           rel_bias, ln1_g, ln1_b, peer_query, peer_subkeys, peer_u, peer_v, ln2_g, ln2_b):
    batch, seq, d = x_prompt.shape
    dec_batch, dec_seq, _ = x_sample.shape
    depth = w_in.shape[0]
    past = cache_sb_k.shape[2]
    nb = cache_band_k.shape[2]
    assert w_in.shape[2] == 6 * WIDTH and w_out.shape[1] == 2 * WIDTH
    assert seq >= BAND_PAST and nb == BAND_PAST
    alpha = float((2 * depth) ** 0.25)
    tp = batch * seq
    ts = dec_batch * dec_seq

    xp = x_prompt.reshape(tp, d)
    xs = x_sample.reshape(ts, d)
    row2 = lambda a: a.reshape(1, -1)
    heads = lambda a, n, s: a.reshape(n, s, N_HEADS, HEAD_DIM)
    outs = [[] for _ in range(8)]
    for l in range(depth):
        w_in_bf = w_in[l].astype(BF16)
        woa = w_out[l, :WIDTH].astype(BF16)
        wob = w_out[l, WIDTH:].astype(BF16)
        wq = peer_query[l].astype(BF16)
        sk = peer_subkeys[l].astype(BF16)
        u_bf = peer_u[l].astype(BF16)
        vt_bf = peer_v[l].T.astype(BF16)
        ga, gb = row2(gn_a[l]), row2(gn_b[l])
        g1, b1, g2, b2 = row2(ln1_g[l]), row2(ln1_b[l]), row2(ln2_g[l]), row2(ln2_b[l])

        qa, ka, va, kab, vab, qb, kb, vb, kbb, vbb = _project(xp, w_in_bf)
        ca = _sb_prompt(qa, kab, vab, ga, batch, seq)
        cb = _band_prompt(qb, kbb, vbb, _band_prompt_bias(rel_bias[l]), gb, batch, seq)
        x1, x1t, st = _merge(ca, cb, xp, woa, wob, g1, b1, wq, sk, alpha)
        tt = min(PEER_TOKEN_TILE, tp)
        th, ea, eb = _peer_gates(st, tt)
        xp = _peer(st, eb, th, ea, x1t, u_bf, vt_bf, x1, g2, b2, alpha, tt)
        outs[0].append(heads(ka, batch, seq))
        outs[1].append(heads(va, batch, seq))
        outs[2].append(heads(kb, batch, seq)[:, seq - BAND_PAST:])
        outs[3].append(heads(vb, batch, seq)[:, seq - BAND_PAST:])

        qa, ka, va, kab, vab, qb, kb, vb, kbb, vbb = _project(xs, w_in_bf)
        biasp, biasn = _sample_bias(rel_bias[l], dec_seq, nb)
        ca, cb, nbk, nbv = _sample_attn(
            qa, kab, vab, cache_sb_k[l].reshape(dec_batch, past, WIDTH), cache_sb_v[l].reshape(dec_batch, past, WIDTH),
            qb, kbb, vbb, kb, vb, cache_band_k[l].reshape(dec_batch, nb, WIDTH),
            cache_band_v[l].reshape(dec_batch, nb, WIDTH), biasp, biasn, ga, gb)
        x1, x1t, st = _merge(ca, cb, xs, woa, wob, g1, b1, wq, sk, alpha)
        tt = min(PEER_TOKEN_TILE, ts)
        th, ea, eb = _peer_gates(st, tt)
        xs = _peer(st, eb, th, ea, x1t, u_bf, vt_bf, x1, g2, b2, alpha, tt)
        outs[4].append(heads(ka, dec_batch, dec_seq))
        outs[5].append(heads(va, dec_batch, dec_seq))
        outs[6].append(heads(nbk, dec_batch, nb))
        outs[7].append(heads(nbv, dec_batch, nb))

    return (xp.reshape(batch, seq, d), xs.reshape(dec_batch, dec_seq, d)) + tuple(jnp.stack(o) for o in outs)
```

```python
import functools

import jax
import jax.numpy as jnp
import numpy as np
from jax import lax
from jax.experimental import pallas as pl
from jax.experimental.pallas import tpu as pltpu

F32 = jnp.float32
BF16 = jnp.bfloat16

HEAD_DIM = 64
N_HEADS = 8
WIDTH = N_HEADS * HEAD_DIM
CHUNK = 64
BAND_CHUNKS = 8
BAND_PAST = BAND_CHUNKS * CHUNK
MAX_REL = 128
N_KEYS = 128
PEER_HEADS = 8
PEER_TOPK = 16
NORM_EPS = 1e-5
NEG_INF = -1e30

LANES = 128
SUBLANES = 8
MXU_DEPTH = 256
VMEM_LIMIT_BYTES = 56 * 1024 * 1024

SB_DEAD_LOG = -104.0
SB_BLOCK = 128
BAND_QBLOCK = 256
ROW_TILE = 512
PEER_TOKEN_TILE = 512
PEER_GATE_TILE = 1024
PEER_EXPERT_BLOCK = 512
PEER_PIPE_FILL = 2
PEER_LANE_TILE = 256
PAD_SCORE = -3.0e38


def _params(*sem):
    return pltpu.CompilerParams(dimension_semantics=sem, vmem_limit_bytes=VMEM_LIMIT_BYTES)


def _nt_dot(a, b):
    return lax.dot_general(a, b, (((1,), (1,)), ((), ())), preferred_element_type=F32)


def _dot(a, b):
    return jnp.dot(a, b, preferred_element_type=F32)


def _layer_norm(y, g, b):
    mu = jnp.mean(y, axis=-1, keepdims=True)
    d = y - mu
    var = jnp.mean(d * d, axis=-1, keepdims=True)
    return d * lax.rsqrt(var + NORM_EPS) * g + b


def _head_rms(o, gain):
    ms = jnp.mean(o * o, axis=-1, keepdims=True)
    return o * lax.rsqrt(ms + NORM_EPS) * gain


def _proj_kernel(x_ref, w_ref, qa_ref, ka_ref, va_ref, kab_ref, vab_ref,
                 qb_ref, kb_ref, vb_ref, kbb_ref, vbb_ref):
    xb = x_ref[...].astype(BF16)
    scale = HEAD_DIM ** -0.5

    def group(g):
        return _dot(xb, w_ref[:, g * WIDTH:(g + 1) * WIDTH])

    qa_ref[...] = (group(0) * scale).astype(BF16)
    k = group(1)
    ka_ref[...] = k
    kab_ref[...] = k.astype(BF16)
    v = group(2)
    va_ref[...] = v
    vab_ref[...] = v.astype(BF16)
    qb_ref[...] = (group(3) * scale).astype(BF16)
    k = group(4)
    kb_ref[...] = k
    kbb_ref[...] = k.astype(BF16)
    v = group(5)
    vb_ref[...] = v
    vbb_ref[...] = v.astype(BF16)


def _project(x, w_bf):
    t, d = x.shape
    tm = min(ROW_TILE, t)
    assert t % tm == 0
    row = lambda i: (i, 0)
    f32o = jax.ShapeDtypeStruct((t, WIDTH), F32)
    bf16o = jax.ShapeDtypeStruct((t, WIDTH), BF16)
    blk = pl.BlockSpec((tm, WIDTH), row)
    return pl.pallas_call(
        _proj_kernel,
        grid=(t // tm,),
        in_specs=[pl.BlockSpec((tm, d), row), pl.BlockSpec(w_bf.shape, lambda i: (0, 0))],
        out_specs=[blk] * 10,
        out_shape=[bf16o, f32o, f32o, bf16o, bf16o, bf16o, f32o, f32o, bf16o, bf16o],
        compiler_params=_params("arbitrary"),
        name="proj",
    )(x, w_bf)


def _suffix_matrix(kb):
    kp = np.arange(kb)
    m = (kp[:, None] > kp[None, :]).astype(np.float32)
    one = np.concatenate([np.ones((kb, LANES), np.float32), m], axis=1)
    return jnp.asarray(np.concatenate([one, one], axis=0), dtype=BF16)


def _sb_log_terms(z, mask):
    sp = jnp.maximum(z, 0.0) + jnp.log(1.0 + jnp.exp(-jnp.abs(z)))
    log_keep = -sp
    if mask is not None:
        log_keep = jnp.where(mask, log_keep, 0.0)
    hi = log_keep.astype(BF16)
    lo = (log_keep - hi.astype(F32)).astype(BF16)
    return z - sp, jnp.concatenate([hi, lo], axis=1)


def _sb_weights(log_sig, sums, carry, mask):
    kb = log_sig.shape[1]
    log_w = log_sig + sums[:, LANES:LANES + kb]
    if carry is not None:
        log_w = log_w + carry
    w = jnp.exp(log_w)
    if mask is not None:
        w = jnp.where(mask, w, 0.0)
    return w, sums[:, :LANES]


def _sb_prompt_kernel(q_ref, k_ref, v_ref, g_ref, mm_ref, o_ref, carry_s, acc_s):
    i = pl.program_id(1)
    r = q_ref.shape[0]
    row = lax.broadcasted_iota(jnp.int32, (r, r), 0)
    col = lax.broadcasted_iota(jnp.int32, (r, r), 1)
    diag = col < row
    heads = [slice(h * HEAD_DIM, (h + 1) * HEAD_DIM) for h in range(N_HEADS)]

    def key_block(j, first):
        off = pl.multiple_of(j * r, r)
        q = q_ref[...]
        kj = k_ref[pl.ds(off, r), :]
        vj = v_ref[pl.ds(off, r), :]
        mm = mm_ref[...]
        mask = diag if first else None
        zs = [_nt_dot(q[:, hs], kj[:, hs]) for hs in heads]
        terms = [_sb_log_terms(z, mask) for z in zs]
        sums = [_dot(halves, mm) for _, halves in terms]
        ws = [_sb_weights(terms[h][0], sums[h], None if first else carry_s[h], mask) for h in range(N_HEADS)]
        pvs = [_dot(ws[h][0].astype(BF16), vj[:, hs]) for h, hs in enumerate(heads)]
        alive = None
        for h in range(N_HEADS):
            carry = ws[h][1] if first else carry_s[h] + ws[h][1]
            carry_s[h] = carry
            acc_s[h] = pvs[h] if first else acc_s[h] + pvs[h]
            alive = carry if alive is None else jnp.maximum(alive, carry)
        return jnp.max(alive)

    def cond(s):
        j, cmax = s
        return jnp.logical_and(j >= 0, cmax > SB_DEAD_LOG)

    def body(s):
        j, _ = s
        return j - 1, key_block(j, False)

    lax.while_loop(cond, body, (i - 1, key_block(i, True)))
    outs = [_head_rms(acc_s[h], g_ref[:, hs]) for h, hs in enumerate(heads)]
    o_ref[...] = jnp.concatenate(outs, axis=1).astype(BF16)


def _sb_prompt(q, k, v, gain, batch, seq):
    r = SB_BLOCK
    assert seq % r == 0
    nq = seq // r
    return pl.pallas_call(
        _sb_prompt_kernel,
        grid=(batch, nq),
        in_specs=[
            pl.BlockSpec((r, WIDTH), lambda b, i: (b * nq + i, 0)),
            pl.BlockSpec((seq, WIDTH), lambda b, i: (b, 0)),
            pl.BlockSpec((seq, WIDTH), lambda b, i: (b, 0)),
            pl.BlockSpec((1, WIDTH), lambda b, i: (0, 0)),
            pl.BlockSpec((2 * r, LANES + r), lambda b, i: (0, 0)),
        ],
        out_specs=pl.BlockSpec((r, WIDTH), lambda b, i: (b * nq + i, 0)),
        out_shape=jax.ShapeDtypeStruct((batch * seq, WIDTH), BF16),
        scratch_shapes=[pltpu.VMEM((N_HEADS, r, LANES), F32),
                        pltpu.VMEM((N_HEADS, r, HEAD_DIM), F32)],
        compiler_params=_params("arbitrary", "arbitrary"),
        name="sb_prompt",
    )(q, k, v, gain, _suffix_matrix(r))


def _band_prompt_kernel(q_ref, k0_ref, k1_ref, k2_ref, v0_ref, v1_ref, v2_ref, bias_ref, g_ref, o_ref):
    i = pl.program_id(1)
    k_refs = (k0_ref, k1_ref, k2_ref)
    v_refs = (v0_ref, v1_ref, v2_ref)
    outs = []
    for h in range(N_HEADS):
        hs = slice(h * HEAD_DIM, (h + 1) * HEAD_DIM)
        qh = q_ref[:, hs]
        scs = []
        for w in range(3):
            sc = _nt_dot(qh, k_refs[w][:, hs]) + bias_ref[h, w]
            if w < 2:
                sc = jnp.where(i >= 2 - w, sc, NEG_INF)
            scs.append(sc)
        m = jnp.maximum(jnp.maximum(jnp.max(scs[0], axis=-1, keepdims=True),
                                    jnp.max(scs[1], axis=-1, keepdims=True)),
                        jnp.max(scs[2], axis=-1, keepdims=True))
        es = [jnp.exp(sc - m) for sc in scs]
        den = (jnp.sum(es[0], axis=-1, keepdims=True) + jnp.sum(es[1], axis=-1, keepdims=True)
               + jnp.sum(es[2], axis=-1, keepdims=True))
        acc = (_dot(es[0].astype(BF16), v_refs[0][:, hs]) + _dot(es[1].astype(BF16), v_refs[1][:, hs])
               + _dot(es[2].astype(BF16), v_refs[2][:, hs]))
        outs.append(_head_rms(acc / den, g_ref[:, hs]))
    o_ref[...] = jnp.concatenate(outs, axis=1).astype(BF16)


def _toeplitz_bias(rel_bias, n, ncols, offset):
    period = n + ncols + 1
    m = jnp.arange(period)
    shift = jnp.where(m < ncols, m, m - period)
    vec = rel_bias[:, jnp.clip(offset - shift, -MAX_REL, MAX_REL) + MAX_REL].astype(F32)
    rows = jnp.tile(vec, (1, n))[:, :n * (period - 1)].reshape(rel_bias.shape[0], n, period - 1)
    return rows[:, :, :ncols]


def _band_prompt_bias(rel_bias):
    qb = BAND_QBLOCK
    bias = _toeplitz_bias(rel_bias, qb, 3 * qb, 2 * qb)
    kc = jnp.arange(3 * qb)[None, :] // CHUNK - (2 * qb // CHUNK)
    qc = jnp.arange(qb)[:, None] // CHUNK
    valid = (kc <= qc) & (kc >= qc - BAND_CHUNKS)
    bias = jnp.where(valid[None], bias, NEG_INF)
    return bias.reshape(N_HEADS, qb, 3, qb).transpose(0, 2, 1, 3)


def _band_prompt(q, k, v, bias, gain, batch, seq):
    qb = BAND_QBLOCK
    assert seq % qb == 0 and 2 * qb == BAND_PAST and qb % CHUNK == 0
    nq = seq // qb
    kspec = lambda back: pl.BlockSpec((qb, WIDTH), lambda b, i: (b * nq + jnp.maximum(i - back, 0), 0))
    return pl.pallas_call(
        _band_prompt_kernel,
        grid=(batch, nq),
        in_specs=[
            pl.BlockSpec((qb, WIDTH), lambda b, i: (b * nq + i, 0)),
            kspec(2), kspec(1), kspec(0), kspec(2), kspec(1), kspec(0),
            pl.BlockSpec(bias.shape, lambda b, i: (0, 0, 0, 0)),
            pl.BlockSpec((1, WIDTH), lambda b, i: (0, 0)),
        ],
        out_specs=pl.BlockSpec((qb, WIDTH), lambda b, i: (b * nq + i, 0)),
        out_shape=jax.ShapeDtypeStruct((batch * seq, WIDTH), BF16),
        compiler_params=_params("arbitrary", "arbitrary"),
        name="band_prompt",
    )(q, k, k, k, v, v, v, bias, gain)


def _sample_attn_kernel(qa_ref, kan_ref, van_ref, cak_ref, cav_ref,
                        qb_ref, kbn_ref, vbn_ref, kbf_ref, vbf_ref, cbk_ref, cbv_ref,
                        biasp_ref, biasn_ref, ga_ref, gb_ref, mmn_ref, mmp_ref,
                        oa_ref, ob_ref, nbk_ref, nbv_ref):
    sd = qa_ref.shape[0]
    past = cak_ref.shape[1]
    nb = cbk_ref.shape[1]
    row = lax.broadcasted_iota(jnp.int32, (sd, sd), 0)
    col = lax.broadcasted_iota(jnp.int32, (sd, sd), 1)
    diag = col < row
    cak = cak_ref[0].astype(BF16)
    cav = cav_ref[0].astype(BF16)
    cbk = cbk_ref[0].astype(BF16)
    cbv = cbv_ref[0].astype(BF16)
    mmn = mmn_ref[...]
    mmp = mmp_ref[...]
    heads = [slice(h * HEAD_DIM, (h + 1) * HEAD_DIM) for h in range(N_HEADS)]
    blocks = [slice(j * SB_BLOCK, (j + 1) * SB_BLOCK) for j in range(past // SB_BLOCK)]
    qa = qa_ref[...]
    z_new = [_nt_dot(qa[:, hs], kan_ref[:, hs]) for hs in heads]
    z_past = [_nt_dot(qa[:, hs], cak[:, hs]) for hs in heads]
    t_new = [_sb_log_terms(z, diag) for z in z_new]
    t_past = [[_sb_log_terms(z[:, ks], None) for ks in blocks] for z in z_past]
    s_new = [_dot(halves, mmn) for _, halves in t_new]
    s_past = [[_dot(halves, mmp) for _, halves in t] for t in t_past]
    outs_a = []
    for h, hs in enumerate(heads):
        w_new, carry = _sb_weights(t_new[h][0], s_new[h], None, diag)
        w_past = [None] * len(blocks)
        for j in reversed(range(len(blocks))):
            w_past[j], total = _sb_weights(t_past[h][j][0], s_past[h][j], carry, None)
            carry = carry + total
        acc = _dot(w_new.astype(BF16), van_ref[:, hs]) + _dot(jnp.concatenate(w_past, axis=1).astype(BF16), cav[:, hs])
        outs_a.append(_head_rms(acc, ga_ref[:, hs]))
    qb = qb_ref[...]
    sc_p = [_nt_dot(qb[:, hs], cbk[:, hs]) + biasp_ref[h] for h, hs in enumerate(heads)]
    sc_n = [_nt_dot(qb[:, hs], kbn_ref[:, hs]) + biasn_ref[h] for h, hs in enumerate(heads)]
    outs_b = []
    for h, hs in enumerate(heads):
        m = jnp.maximum(jnp.max(sc_p[h], axis=-1, keepdims=True), jnp.max(sc_n[h], axis=-1, keepdims=True))
        ep = jnp.exp(sc_p[h] - m)
        en = jnp.exp(sc_n[h] - m)
        den = jnp.sum(ep, axis=-1, keepdims=True) + jnp.sum(en, axis=-1, keepdims=True)
        acc = _dot(ep.astype(BF16), cbv[:, hs]) + _dot(en.astype(BF16), vbn_ref[:, hs])
        outs_b.append(_head_rms(acc / den, gb_ref[:, hs]))
    oa_ref[...] = jnp.concatenate(outs_a, axis=1).astype(BF16)
    ob_ref[...] = jnp.concatenate(outs_b, axis=1).astype(BF16)
    nbk_ref[0, :nb - sd, :] = cbk_ref[0, sd:, :]
    nbk_ref[0, nb - sd:, :] = kbf_ref[...]
    nbv_ref[0, :nb - sd, :] = cbv_ref[0, sd:, :]
    nbv_ref[0, nb - sd:, :] = vbf_ref[...]


def _sample_bias(rel_bias, sd, nb):
    bias = _toeplitz_bias(rel_bias, sd, nb + sd, nb)
    return bias[:, :, :nb], bias[:, :, nb:]


def _sample_attn(qa, kan, van, cak, cav, qb, kbn, vbn, kbf, vbf, cbk, cbv, biasp, biasn, ga, gb):
    nbatch, past, _ = cak.shape
    nb = cbk.shape[1]
    sd = qa.shape[0] // nbatch
    assert past % SB_BLOCK == 0 and sd % 8 == 0 and sd <= nb and sd <= LANES
    rows = pl.BlockSpec((sd, WIDTH), lambda b: (b, 0))
    cache = lambda n: pl.BlockSpec((1, n, WIDTH), lambda b: (b, 0, 0))
    full = lambda a: pl.BlockSpec(a.shape, lambda b: (0,) * a.ndim)
    mmn = _suffix_matrix(sd)
    mmp = _suffix_matrix(SB_BLOCK)
    return pl.pallas_call(
        _sample_attn_kernel,
        grid=(nbatch,),
        in_specs=[rows, rows, rows, cache(past), cache(past),
                  rows, rows, rows, rows, rows, cache(nb), cache(nb),
                  full(biasp), full(biasn), full(ga), full(gb), full(mmn), full(mmp)],
        out_specs=[rows, rows, cache(nb), cache(nb)],
        out_shape=[jax.ShapeDtypeStruct((nbatch * sd, WIDTH), BF16)] * 2
                  + [jax.ShapeDtypeStruct((nbatch, nb, WIDTH), F32)] * 2,
        compiler_params=_params("arbitrary"),
        name="sample_attn",
    )(qa, kan, van, cak, cav, qb, kbn, vbn, kbf, vbf, cbk, cbv, biasp, biasn, ga, gb, mmn, mmp)


def _merge_kernel(ca_ref, cb_ref, x_ref, woa_ref, wob_ref, g_ref, b_ref, wq_ref, sk_ref,
                  x1_ref, x1t_ref, st_ref, *, alpha):
    mix = _dot(ca_ref[...], woa_ref[...]) + _dot(cb_ref[...], wob_ref[...])
    x1 = _layer_norm(alpha * x_ref[...] + mix, g_ref[...], b_ref[...])
    x1_ref[...] = x1
    x1t_ref[...] = x1.T.astype(BF16)
    qp = _dot(x1.astype(BF16), wq_ref[...])
    half = sk_ref.shape[2]
    for h in range(PEER_HEADS):
        for j in range(2):
            c0 = (2 * h + j) * half
            qh = qp[:, c0:c0 + half].astype(BF16)
            st_ref[(2 * h + j) * N_KEYS:(2 * h + j + 1) * N_KEYS, :] = _nt_dot(sk_ref[j], qh)


def _merge(ca, cb, x, woa, wob, g, b, wq, sk, alpha):
    t, d = x.shape
    tm = min(ROW_TILE, t)
    assert t % tm == 0
    nscore = PEER_HEADS * 2 * N_KEYS
    row = lambda i: (i, 0)
    colb = lambda i: (0, i)
    full = lambda a: pl.BlockSpec(a.shape, lambda i: (0,) * a.ndim)
    return pl.pallas_call(
        functools.partial(_merge_kernel, alpha=alpha),
        grid=(t // tm,),
        in_specs=[pl.BlockSpec((tm, WIDTH), row), pl.BlockSpec((tm, WIDTH), row), pl.BlockSpec((tm, d), row),
                  full(woa), full(wob), full(g), full(b), full(wq), full(sk)],
        out_specs=[pl.BlockSpec((tm, d), row), pl.BlockSpec((d, tm), colb), pl.BlockSpec((nscore, tm), colb)],
        out_shape=[jax.ShapeDtypeStruct((t, d), F32), jax.ShapeDtypeStruct((d, t), BF16),
                   jax.ShapeDtypeStruct((nscore, t), F32)],
        compiler_params=_params("arbitrary"),
        name="merge",
    )(ca, cb, x, woa, wob, g, b, wq, sk)


def _cmp_exchange(v, i, j):
    a, b = v[i], v[j]
    v[i] = jnp.maximum(a, b)
    v[j] = jnp.minimum(a, b)


def _sort16_desc(v):
    v = list(v)
    n = len(v)
    k = 2
    while k <= n:
        j = k // 2
        while j >= 1:
            for i in range(n):
                l = i ^ j
                if l > i:
                    if (i & k) == 0:
                        _cmp_exchange(v, i, l)
                    else:
                        _cmp_exchange(v, l, i)
            j //= 2
        k *= 2
    return v


def _merge_top16(a, b):
    n = len(a)
    top = [jnp.maximum(a[i], b[n - 1 - i]) for i in range(n)]
    out = [jnp.minimum(a[i], b[n - 1 - i]) for i in range(n)]
    while len(out) > 1:
        out = [jnp.maximum(out[2 * i], out[2 * i + 1]) for i in range(len(out) // 2)]
    j = n // 2
    while j >= 1:
        for i in range(n):
            if (i & j) == 0:
                _cmp_exchange(top, i, i + j)
        j //= 2
    return top, out[0]


def _top16_and_next(vals):
    groups = [_sort16_desc(vals[g:g + PEER_TOPK]) for g in range(0, len(vals), PEER_TOPK)]
    nxt = None
    while len(groups) > 1:
        merged = []
        for g in range(0, len(groups), 2):
            top, left = _merge_top16(groups[g], groups[g + 1])
            merged.append(top)
            nxt = left if nxt is None else jnp.maximum(nxt, left)
        groups = merged
    return groups[0], nxt


def _gelu_tanh(x):
    return 0.5 * x * (1.0 + jnp.tanh(0.7978845608028654 * (x + 0.044715 * (x * x * x))))


def _peer_gate_kernel(st_ref, th_ref, ea_ref, eb_ref, row_s, *, peer_tile):
    tt = st_ref.shape[1]
    ng = tt // LANES
    k1 = PEER_TOPK + 1

    def head(h, carry):
        base = pl.multiple_of(h * 2 * N_KEYS, 2 * N_KEYS)
        s1 = st_ref[pl.ds(base, N_KEYS), :]
        s2 = st_ref[pl.ds(base + N_KEYS, N_KEYS), :]
        s1r = s1.reshape(N_KEYS, ng, LANES)
        s2r = s2.reshape(N_KEYS, ng, LANES)
        top_a, next_a = _top16_and_next([s1r[a] for a in range(N_KEYS)])
        top_b, next_b = _top16_and_next([s2r[a] for a in range(N_KEYS)])
        la = top_a + [next_a]
        lb = top_b + [next_b]
        cands = [la[i - 1] + lb[j - 1] for i in range(1, k1 + 1) for j in range(1, k1 + 1) if i * j <= k1]
        pad = jnp.full_like(cands[0], PAD_SCORE)
        cands = cands + [pad] * (-len(cands) % PEER_TOPK)
        top_c, next_c = _top16_and_next(cands)
        tau = 0.5 * (top_c[PEER_TOPK - 1] + next_c)
        den = jnp.ones_like(tau)
        for cv in top_c[1:]:
            den = den + jnp.exp(cv - top_c[0])
        vals = (tau, la[0], lb[0], 1.0 / den)
        for r, val in enumerate(vals):
            for g in range(ng):
                row_s[r:r + 1, g * LANES:(g + 1) * LANES] = val[g:g + 1, :]
        tau_r = row_s[0:1, :]
        m1_r = row_s[1:2, :]
        m2_r = row_s[2:3, :]
        iz_r = row_s[3:4, :]
        th_ref[:, pl.ds(h, 1), :] = (tau_r - s1)[:, None, :]
        ea_ref[:, pl.ds(h, 1), :] = (jnp.exp(s1 - m1_r) * iz_r)[:, None, :]
        eb = jnp.exp(s2 - m2_r)
        eb_rows = pl.ds(pl.multiple_of(h * N_KEYS, N_KEYS), N_KEYS)
        for p0 in range(0, tt, peer_tile):
            p1 = p0 + peer_tile
            eb_ref[eb_rows, p0 + LANES:p1] = eb[:, p0:p1 - LANES]
            eb_ref[eb_rows, p0:p0 + LANES] = eb[:, p1 - LANES:p1]
        return carry

    lax.fori_loop(0, PEER_HEADS, head, 0)


def _peer_gates(st, peer_tile):
    nscore, t = st.shape
    tt = PEER_GATE_TILE if t % PEER_GATE_TILE == 0 else peer_tile
    assert t % tt == 0 and tt % peer_tile == 0 and peer_tile > LANES and peer_tile % LANES == 0
    rows = jax.ShapeDtypeStruct((N_KEYS, PEER_HEADS, t), F32)
    return pl.pallas_call(
        functools.partial(_peer_gate_kernel, peer_tile=peer_tile),
        grid=(t // tt,),
        in_specs=[pl.BlockSpec((nscore, tt), lambda i: (0, i))],
        out_specs=[pl.BlockSpec((N_KEYS, PEER_HEADS, tt), lambda i: (0, 0, i)),
                   pl.BlockSpec((N_KEYS, PEER_HEADS, tt), lambda i: (0, 0, i)),
                   pl.BlockSpec((PEER_HEADS * N_KEYS, tt), lambda i: (0, i))],
        out_shape=[rows, rows, jax.ShapeDtypeStruct((PEER_HEADS * N_KEYS, t), F32)],
        scratch_shapes=[pltpu.VMEM((SUBLANES, tt), F32)],
        compiler_params=_params("arbitrary"),
        name="peer_gates",
    )(st)


def _peer_kernel(st_ref, eb_ref, th_ref, ea_ref, xt_ref, u_ref, vt_ref, x1_ref, g_ref, b_ref, o_ref,
                 ht0_s, ht1_s, wt0_s, wt1_s, acc_s, *, alpha):
    c = pl.program_id(1)
    nc = pl.num_programs(1)
    tt = xt_ref.shape[1]
    ch = u_ref.shape[0]

    @pl.when(c == 0)
    def _clear():
        ht1_s[...] = jnp.zeros(ht1_s.shape, F32)
        wt0_s[...] = jnp.zeros(wt0_s.shape, BF16)
        wt1_s[...] = jnp.zeros(wt1_s.shape, BF16)
        acc_s[...] = jnp.zeros(acc_s.shape, F32)

    ablocks = ch // N_KEYS
    lt = PEER_LANE_TILE

    def stages(ht_new, ht_old, wt_new, wt_old):
        d = vt_ref.shape[0]
        npiece = ablocks // 2
        nlv = lt // LANES
        nr2 = N_KEYS // (2 * SUBLANES)
        assert nlv == 2
        for ts in range(tt // lt):
            ls = slice(ts * lt, (ts + 1) * lt)
            for unit in range(npiece * nlv):
                ap, tv = unit // nlv, unit % nlv
                piece = unit // 2
                if unit % 2 == 0:
                    rows = slice(piece * (ch // npiece), (piece + 1) * (ch // npiece))
                    lhs_ref, rhs_ref, kdim = u_ref, xt_ref, u_ref.shape[1]
                else:
                    rows = slice(piece * (d // npiece), (piece + 1) * (d // npiece))
                    lhs_ref, rhs_ref, kdim = vt_ref, wt_old, ch
                nk = kdim // MXU_DEPTH
                part = None
                g = ts * nlv + tv
                lv = slice(g * LANES, (g + 1) * LANES)
                ge = (g + 1) % (tt // LANES)
                le = slice(ge * LANES, (ge + 1) * LANES)
                als = (2 * ap, 2 * ap + 1)
                th = [[jnp.broadcast_to(th_ref[al, h:h + 1, lv], (SUBLANES, LANES)) for h in range(PEER_HEADS)] for al in als]
                ea = [[jnp.broadcast_to(ea_ref[al, h:h + 1, lv], (SUBLANES, LANES)) for h in range(PEER_HEADS)] for al in als]
                for r2 in range(nr2):
                    if r2 % (nr2 // nk) == 0:
                        kb = slice(r2 // (nr2 // nk) * MXU_DEPTH, (r2 // (nr2 // nk) + 1) * MXU_DEPTH)
                        dk = _dot(lhs_ref[rows, kb], rhs_ref[kb, ls])
                        part = dk if part is None else part + dk
                    gates = [[jnp.zeros((SUBLANES, LANES), F32) for _ in range(2)] for _ in als]
                    for k in range(2):
                        r = 2 * r2 + k
                        for h in range(PEER_HEADS):
                            s2 = st_ref[(2 * h + 1) * N_KEYS + r * SUBLANES:(2 * h + 1) * N_KEYS + (r + 1) * SUBLANES, lv]
                            eb = eb_ref[h * N_KEYS + r * SUBLANES:h * N_KEYS + (r + 1) * SUBLANES, le]
                            for i in range(2):
                                gates[i][k] = gates[i][k] + jnp.where(s2 >= th[i][h], eb, 0.0) * ea[i][h]
                    for i, al in enumerate(als):
                        rs = slice(al * N_KEYS + r2 * 2 * SUBLANES, al * N_KEYS + (r2 + 1) * 2 * SUBLANES)
                        gate = jnp.concatenate(gates[i], axis=0)
                        wt_new[rs, lv] = (gate * _gelu_tanh(ht_old[rs, lv])).astype(BF16)
                if unit % 2 == 0:
                    ht_new[rows, ls] = part
                else:
                    acc_s[rows, ls] += part

    @pl.when(c % 2 == 0)
    def _even():
        stages(ht0_s, ht1_s, wt1_s, wt0_s)

    @pl.when(c % 2 == 1)
    def _odd():
        stages(ht1_s, ht0_s, wt0_s, wt1_s)

    @pl.when(c == nc - 1)
    def _finish():
        y = alpha * x1_ref[...] + acc_s[...].T
        o_ref[...] = _layer_norm(y, g_ref[...], b_ref[...])


def _peer(st, eb, th, ea, xt, u_bf, vt_bf, x1, g, b, alpha, tt):
    t, d = x1.shape
    ne = u_bf.shape[0]
    ch = PEER_EXPERT_BLOCK
    assert t % tt == 0 and tt % PEER_LANE_TILE == 0 and ne % ch == 0 and ne == N_KEYS * N_KEYS
    nscore = st.shape[0]
    nblk = ne // ch
    ablocks = ch // N_KEYS
    gated = lambda i, c: (jnp.clip(c - 1, 0, nblk - 1), 0, i)
    return pl.pallas_call(
        functools.partial(_peer_kernel, alpha=alpha),
        grid=(t // tt, nblk + PEER_PIPE_FILL),
        in_specs=[
            pl.BlockSpec((nscore, tt), lambda i, c: (0, i)),
            pl.BlockSpec((PEER_HEADS * N_KEYS, tt), lambda i, c: (0, i)),
            pl.BlockSpec((ablocks, PEER_HEADS, tt), gated),
            pl.BlockSpec((ablocks, PEER_HEADS, tt), gated),
            pl.BlockSpec((d, tt), lambda i, c: (0, i)),
            pl.BlockSpec((ch, d), lambda i, c: (jnp.minimum(c, nblk - 1), 0)),
            pl.BlockSpec((d, ch), lambda i, c: (0, jnp.clip(c - PEER_PIPE_FILL, 0, nblk - 1))),
            pl.BlockSpec((tt, d), lambda i, c: (i, 0)),
            pl.BlockSpec((1, d), lambda i, c: (0, 0)),
            pl.BlockSpec((1, d), lambda i, c: (0, 0)),
        ],
        out_specs=pl.BlockSpec((tt, d), lambda i, c: (i, 0)),
        out_shape=jax.ShapeDtypeStruct((t, d), F32),
        scratch_shapes=[
            pltpu.VMEM((ch, tt), F32),
            pltpu.VMEM((ch, tt), F32),
            pltpu.VMEM((ch, tt), BF16),
            pltpu.VMEM((ch, tt), BF16),
            pltpu.VMEM((d, tt), F32),
        ],
        compiler_params=_params("arbitrary", "arbitrary"),
        name="peer",
    )(st, eb, th, ea, xt, u_bf, vt_bf, x1, g, b)


def kernel(x_prompt, x_sample, cache_sb_k, cache_sb_v, cache_band_k, cache_band_v, w_in, w_out, gn_a, gn_b,
           rel_bias, ln1_g, ln1_b, peer_query, peer_subkeys, peer_u, peer_v, ln2_g, ln2_b):
    batch, seq, d = x_prompt.shape
    dec_batch, dec_seq, _ = x_sample.shape
    depth = w_in.shape[0]
    past = cache_sb_k.shape[2]
    nb = cache_band_k.shape[2]
    assert w_in.shape[2] == 6 * WIDTH and w_out.shape[1] == 2 * WIDTH
    assert seq >= BAND_PAST and nb == BAND_PAST
    alpha = float((2 * depth) ** 0.25)
    tp = batch * seq
    ts = dec_batch * dec_seq

    xp = x_prompt.reshape(tp, d)
    xs = x_sample.reshape(ts, d)
    row2 = lambda a: a.reshape(1, -1)
    heads = lambda a, n, s: a.reshape(n, s, N_HEADS, HEAD_DIM)
    outs = [[] for _ in range(8)]
    for l in range(depth):
        w_in_bf = w_in[l].astype(BF16)
        woa = w_out[l, :WIDTH].astype(BF16)
        wob = w_out[l, WIDTH:].astype(BF16)
        wq = peer_query[l].astype(BF16)
        sk = peer_subkeys[l].astype(BF16)
        u_bf = peer_u[l].astype(BF16)
        vt_bf = peer_v[l].T.astype(BF16)
        ga, gb = row2(gn_a[l]), row2(gn_b[l])
        g1, b1, g2, b2 = row2(ln1_g[l]), row2(ln1_b[l]), row2(ln2_g[l]), row2(ln2_b[l])

        qa, ka, va, kab, vab, qb, kb, vb, kbb, vbb = _project(xp, w_in_bf)
        ca = _sb_prompt(qa, kab, vab, ga, batch, seq)
        cb = _band_prompt(qb, kbb, vbb, _band_prompt_bias(rel_bias[l]), gb, batch, seq)
        x1, x1t, st = _merge(ca, cb, xp, woa, wob, g1, b1, wq, sk, alpha)
        tt = min(PEER_TOKEN_TILE, tp)
        th, ea, eb = _peer_gates(st, tt)
        xp = _peer(st, eb, th, ea, x1t, u_bf, vt_bf, x1, g2, b2, alpha, tt)
        outs[0].append(heads(ka, batch, seq))
        outs[1].append(heads(va, batch, seq))
        outs[2].append(heads(kb, batch, seq)[:, seq - BAND_PAST:])
        outs[3].append(heads(vb, batch, seq)[:, seq - BAND_PAST:])

        qa, ka, va, kab, vab, qb, kb, vb, kbb, vbb = _project(xs, w_in_bf)
        biasp, biasn = _sample_bias(rel_bias[l], dec_seq, nb)
        ca, cb, nbk, nbv = _sample_attn(
            qa, kab, vab, cache_sb_k[l].reshape(dec_batch, past, WIDTH), cache_sb_v[l].reshape(dec_batch, past, WIDTH),
            qb, kbb, vbb, kb, vb, cache_band_k[l].reshape(dec_batch, nb, WIDTH),
            cache_band_v[l].reshape(dec_batch, nb, WIDTH), biasp, biasn, ga, gb)
        x1, x1t, st = _merge(ca, cb, xs, woa, wob, g1, b1, wq, sk, alpha)
        tt = min(PEER_TOKEN_TILE, ts)
        th, ea, eb = _peer_gates(st, tt)
        xs = _peer(st, eb, th, ea, x1t, u_bf, vt_bf, x1, g2, b2, alpha, tt)
        outs[4].append(heads(ka, dec_batch, dec_seq))
        outs[5].append(heads(va, dec_batch, dec_seq))
        outs[6].append(heads(nbk, dec_batch, nb))
        outs[7].append(heads(nbv, dec_batch, nb))

    return (xp.reshape(batch, seq, d), xs.reshape(dec_batch, dec_seq, d)) + tuple(jnp.stack(o) for o in outs)
```

```python
import functools

import jax
import jax.numpy as jnp
import numpy as np
from jax import lax
from jax.experimental import pallas as pl
from jax.experimental.pallas import tpu as pltpu

F32 = jnp.float32
BF16 = jnp.bfloat16
U32 = jnp.uint32

HEAD_DIM = 64
N_HEADS = 8
WIDTH = N_HEADS * HEAD_DIM
CHUNK = 64
BAND_CHUNKS = 8
BAND_PAST = BAND_CHUNKS * CHUNK
MAX_REL = 128
N_KEYS = 128
PEER_HEADS = 8
PEER_TOPK = 16
NORM_EPS = 1e-5
NEG_INF = -1e30

LANES = 128
SUBLANES = 8
MXU_DEPTH = 256
VMEM_LIMIT_BYTES = 56 * 1024 * 1024

SB_DEAD_LOG = -104.0
SB_BLOCK = 128
BAND_QBLOCK = 256
ROW_TILE = 512
PEER_TOKEN_TILE = 512
PEER_GATE_TILE = 1024
PEER_EXPERT_BLOCK = 512
PEER_PIPE_FILL = 2
PEER_LANE_TILE = 256
PAD_SCORE = -3.0e38


def _params(*sem):
    return pltpu.CompilerParams(dimension_semantics=sem, vmem_limit_bytes=VMEM_LIMIT_BYTES)


def _nt_dot(a, b):
    return lax.dot_general(a, b, (((1,), (1,)), ((), ())), preferred_element_type=F32)


def _dot(a, b):
    return jnp.dot(a, b, preferred_element_type=F32)


def _layer_norm(y, g, b):
    mu = jnp.mean(y, axis=-1, keepdims=True)
    d = y - mu
    var = jnp.mean(d * d, axis=-1, keepdims=True)
    return d * lax.rsqrt(var + NORM_EPS) * g + b


def _head_rms(o, gain):
    ms = jnp.mean(o * o, axis=-1, keepdims=True)
    return o * lax.rsqrt(ms + NORM_EPS) * gain


def _proj_kernel(x_ref, w_ref, qa_ref, ka_ref, va_ref, kab_ref, vab_ref,
                 qb_ref, kb_ref, vb_ref, kbb_ref, vbb_ref):
    xb = x_ref[...].astype(BF16)
    scale = HEAD_DIM ** -0.5

    def group(g):
        return _dot(xb, w_ref[:, g * WIDTH:(g + 1) * WIDTH])

    qa_ref[...] = (group(0) * scale).astype(BF16)
    k = group(1)
    ka_ref[...] = k
    kab_ref[...] = k.astype(BF16)
    v = group(2)
    va_ref[...] = v
    vab_ref[...] = v.astype(BF16)
    qb_ref[...] = (group(3) * scale).astype(BF16)
    k = group(4)
    kb_ref[...] = k
    kbb_ref[...] = k.astype(BF16)
    v = group(5)
    vb_ref[...] = v
    vbb_ref[...] = v.astype(BF16)


def _project(x, w_bf):
    t, d = x.shape
    tm = min(ROW_TILE, t)
    assert t % tm == 0
    row = lambda i: (i, 0)
    f32o = jax.ShapeDtypeStruct((t, WIDTH), F32)
    bf16o = jax.ShapeDtypeStruct((t, WIDTH), BF16)
    blk = pl.BlockSpec((tm, WIDTH), row)
    return pl.pallas_call(
        _proj_kernel,
        grid=(t // tm,),
        in_specs=[pl.BlockSpec((tm, d), row), pl.BlockSpec(w_bf.shape, lambda i: (0, 0))],
        out_specs=[blk] * 10,
        out_shape=[bf16o, f32o, f32o, bf16o, bf16o, bf16o, f32o, f32o, bf16o, bf16o],
        compiler_params=_params("arbitrary"),
        name="proj",
    )(x, w_bf)


def _suffix_matrix(kb):
    kp = np.arange(kb)
    m = (kp[:, None] > kp[None, :]).astype(np.float32)
    one = np.concatenate([np.ones((kb, LANES), np.float32), m], axis=1)
    return jnp.asarray(np.concatenate([one, one], axis=0), dtype=BF16)


def _sb_log_terms(z, mask):
    sp = jnp.maximum(z, 0.0) + jnp.log(1.0 + jnp.exp(-jnp.abs(z)))
    log_keep = -sp
    if mask is not None:
        log_keep = jnp.where(mask, log_keep, 0.0)
    hi = log_keep.astype(BF16)
    lo = (log_keep - hi.astype(F32)).astype(BF16)
    return z - sp, jnp.concatenate([hi, lo], axis=1)


def _sb_weights(log_sig, sums, carry, mask):
    kb = log_sig.shape[1]
    log_w = log_sig + sums[:, LANES:LANES + kb]
    if carry is not None:
        log_w = log_w + carry
    w = jnp.exp(log_w)
    if mask is not None:
        w = jnp.where(mask, w, 0.0)
    return w, sums[:, :LANES]


def _sb_prompt_kernel(q_ref, k_ref, v_ref, g_ref, mm_ref, o_ref, carry_s, acc_s):
    i = pl.program_id(1)
    r = q_ref.shape[0]
    row = lax.broadcasted_iota(jnp.int32, (r, r), 0)
    col = lax.broadcasted_iota(jnp.int32, (r, r), 1)
    diag = col < row
    heads = [slice(h * HEAD_DIM, (h + 1) * HEAD_DIM) for h in range(N_HEADS)]

    def key_block(j, first):
        off = pl.multiple_of(j * r, r)
        q = q_ref[...]
        kj = k_ref[pl.ds(off, r), :]
        vj = v_ref[pl.ds(off, r), :]
        mm = mm_ref[...]
        mask = diag if first else None
        zs = [_nt_dot(q[:, hs], kj[:, hs]) for hs in heads]
        terms = [_sb_log_terms(z, mask) for z in zs]
        sums = [_dot(halves, mm) for _, halves in terms]
        ws = [_sb_weights(terms[h][0], sums[h], None if first else carry_s[h], mask) for h in range(N_HEADS)]
        pvs = [_dot(ws[h][0].astype(BF16), vj[:, hs]) for h, hs in enumerate(heads)]
        alive = None
        for h in range(N_HEADS):
            carry = ws[h][1] if first else carry_s[h] + ws[h][1]
            carry_s[h] = carry
            acc_s[h] = pvs[h] if first else acc_s[h] + pvs[h]
            alive = carry if alive is None else jnp.maximum(alive, carry)
        return jnp.max(alive)

    def cond(s):
        j, cmax = s
        return jnp.logical_and(j >= 0, cmax > SB_DEAD_LOG)

    def body(s):
        j, _ = s
        return j - 1, key_block(j, False)

    lax.while_loop(cond, body, (i - 1, key_block(i, True)))
    outs = [_head_rms(acc_s[h], g_ref[:, hs]) for h, hs in enumerate(heads)]
    o_ref[...] = jnp.concatenate(outs, axis=1).astype(BF16)


def _sb_prompt(q, k, v, gain, batch, seq):
    r = SB_BLOCK
    assert seq % r == 0
    nq = seq // r
    return pl.pallas_call(
        _sb_prompt_kernel,
        grid=(batch, nq),
        in_specs=[
            pl.BlockSpec((r, WIDTH), lambda b, i: (b * nq + i, 0)),
            pl.BlockSpec((seq, WIDTH), lambda b, i: (b, 0)),
            pl.BlockSpec((seq, WIDTH), lambda b, i: (b, 0)),
            pl.BlockSpec((1, WIDTH), lambda b, i: (0, 0)),
            pl.BlockSpec((2 * r, LANES + r), lambda b, i: (0, 0)),
        ],
        out_specs=pl.BlockSpec((r, WIDTH), lambda b, i: (b * nq + i, 0)),
        out_shape=jax.ShapeDtypeStruct((batch * seq, WIDTH), BF16),
        scratch_shapes=[pltpu.VMEM((N_HEADS, r, LANES), F32),
                        pltpu.VMEM((N_HEADS, r, HEAD_DIM), F32)],
        compiler_params=_params("arbitrary", "arbitrary"),
        name="sb_prompt",
    )(q, k, v, gain, _suffix_matrix(r))


def _band_prompt_kernel(q_ref, k0_ref, k1_ref, k2_ref, v0_ref, v1_ref, v2_ref, bias_ref, g_ref, o_ref):
    i = pl.program_id(1)
    k_refs = (k0_ref, k1_ref, k2_ref)
    v_refs = (v0_ref, v1_ref, v2_ref)
    outs = []
    for h in range(N_HEADS):
        hs = slice(h * HEAD_DIM, (h + 1) * HEAD_DIM)
        qh = q_ref[:, hs]
        scs = []
        for w in range(3):
            sc = _nt_dot(qh, k_refs[w][:, hs]) + bias_ref[h, w]
            if w < 2:
                sc = jnp.where(i >= 2 - w, sc, NEG_INF)
            scs.append(sc)
        m = jnp.maximum(jnp.maximum(jnp.max(scs[0], axis=-1, keepdims=True),
                                    jnp.max(scs[1], axis=-1, keepdims=True)),
                        jnp.max(scs[2], axis=-1, keepdims=True))
        es = [jnp.exp(sc - m) for sc in scs]
        den = (jnp.sum(es[0], axis=-1, keepdims=True) + jnp.sum(es[1], axis=-1, keepdims=True)
               + jnp.sum(es[2], axis=-1, keepdims=True))
        acc = (_dot(es[0].astype(BF16), v_refs[0][:, hs]) + _dot(es[1].astype(BF16), v_refs[1][:, hs])
               + _dot(es[2].astype(BF16), v_refs[2][:, hs]))
        outs.append(_head_rms(acc / den, g_ref[:, hs]))
    o_ref[...] = jnp.concatenate(outs, axis=1).astype(BF16)


def _toeplitz_bias(rel_bias, n, ncols, offset):
    period = n + ncols + 1
    m = jnp.arange(period)
    shift = jnp.where(m < ncols, m, m - period)
    vec = rel_bias[:, jnp.clip(offset - shift, -MAX_REL, MAX_REL) + MAX_REL].astype(F32)
    rows = jnp.tile(vec, (1, n))[:, :n * (period - 1)].reshape(rel_bias.shape[0], n, period - 1)
    return rows[:, :, :ncols]


def _band_prompt_bias(rel_bias):
    qb = BAND_QBLOCK
    bias = _toeplitz_bias(rel_bias, qb, 3 * qb, 2 * qb)
    kc = jnp.arange(3 * qb)[None, :] // CHUNK - (2 * qb // CHUNK)
    qc = jnp.arange(qb)[:, None] // CHUNK
    valid = (kc <= qc) & (kc >= qc - BAND_CHUNKS)
    bias = jnp.where(valid[None], bias, NEG_INF)
    return bias.reshape(N_HEADS, qb, 3, qb).transpose(0, 2, 1, 3)


def _band_prompt(q, k, v, bias, gain, batch, seq):
    qb = BAND_QBLOCK
    assert seq % qb == 0 and 2 * qb == BAND_PAST and qb % CHUNK == 0
    nq = seq // qb
    kspec = lambda back: pl.BlockSpec((qb, WIDTH), lambda b, i: (b * nq + jnp.maximum(i - back, 0), 0))
    return pl.pallas_call(
        _band_prompt_kernel,
        grid=(batch, nq),
        in_specs=[
            pl.BlockSpec((qb, WIDTH), lambda b, i: (b * nq + i, 0)),
            kspec(2), kspec(1), kspec(0), kspec(2), kspec(1), kspec(0),
            pl.BlockSpec(bias.shape, lambda b, i: (0, 0, 0, 0)),
            pl.BlockSpec((1, WIDTH), lambda b, i: (0, 0)),
        ],
        out_specs=pl.BlockSpec((qb, WIDTH), lambda b, i: (b * nq + i, 0)),
        out_shape=jax.ShapeDtypeStruct((batch * seq, WIDTH), BF16),
        compiler_params=_params("arbitrary", "arbitrary"),
        name="band_prompt",
    )(q, k, k, k, v, v, v, bias, gain)


def _sample_attn_kernel(qa_ref, kan_ref, van_ref, cak_ref, cav_ref,
                        qb_ref, kbn_ref, vbn_ref, kbf_ref, vbf_ref, cbk_ref, cbv_ref,
                        biasp_ref, biasn_ref, ga_ref, gb_ref, mmn_ref, mmp_ref,
                        oa_ref, ob_ref, nbk_ref, nbv_ref):
    sd = qa_ref.shape[0]
    past = cak_ref.shape[1]
    nb = cbk_ref.shape[1]
    row = lax.broadcasted_iota(jnp.int32, (sd, sd), 0)
    col = lax.broadcasted_iota(jnp.int32, (sd, sd), 1)
    diag = col < row
    cak = cak_ref[0].astype(BF16)
    cav = cav_ref[0].astype(BF16)
    cbk = cbk_ref[0].astype(BF16)
    cbv = cbv_ref[0].astype(BF16)
    mmn = mmn_ref[...]
    mmp = mmp_ref[...]
    heads = [slice(h * HEAD_DIM, (h + 1) * HEAD_DIM) for h in range(N_HEADS)]
    blocks = [slice(j * SB_BLOCK, (j + 1) * SB_BLOCK) for j in range(past // SB_BLOCK)]
    qa = qa_ref[...]
    z_new = [_nt_dot(qa[:, hs], kan_ref[:, hs]) for hs in heads]
    z_past = [_nt_dot(qa[:, hs], cak[:, hs]) for hs in heads]
    t_new = [_sb_log_terms(z, diag) for z in z_new]
    t_past = [[_sb_log_terms(z[:, ks], None) for ks in blocks] for z in z_past]
    s_new = [_dot(halves, mmn) for _, halves in t_new]
    s_past = [[_dot(halves, mmp) for _, halves in t] for t in t_past]
    outs_a = []
    for h, hs in enumerate(heads):
        w_new, carry = _sb_weights(t_new[h][0], s_new[h], None, diag)
        w_past = [None] * len(blocks)
        for j in reversed(range(len(blocks))):
            w_past[j], total = _sb_weights(t_past[h][j][0], s_past[h][j], carry, None)
            carry = carry + total
        acc = _dot(w_new.astype(BF16), van_ref[:, hs]) + _dot(jnp.concatenate(w_past, axis=1).astype(BF16), cav[:, hs])
        outs_a.append(_head_rms(acc, ga_ref[:, hs]))
    qb = qb_ref[...]
    sc_p = [_nt_dot(qb[:, hs], cbk[:, hs]) + biasp_ref[h] for h, hs in enumerate(heads)]
    sc_n = [_nt_dot(qb[:, hs], kbn_ref[:, hs]) + biasn_ref[h] for h, hs in enumerate(heads)]
    outs_b = []
    for h, hs in enumerate(heads):
        m = jnp.maximum(jnp.max(sc_p[h], axis=-1, keepdims=True), jnp.max(sc_n[h], axis=-1, keepdims=True))
        ep = jnp.exp(sc_p[h] - m)
        en = jnp.exp(sc_n[h] - m)
        den = jnp.sum(ep, axis=-1, keepdims=True) + jnp.sum(en, axis=-1, keepdims=True)
        acc = _dot(ep.astype(BF16), cbv[:, hs]) + _dot(en.astype(BF16), vbn_ref[:, hs])
        outs_b.append(_head_rms(acc / den, gb_ref[:, hs]))
    oa_ref[...] = jnp.concatenate(outs_a, axis=1).astype(BF16)
    ob_ref[...] = jnp.concatenate(outs_b, axis=1).astype(BF16)
    nbk_ref[0, :nb - sd, :] = cbk_ref[0, sd:, :]
    nbk_ref[0, nb - sd:, :] = kbf_ref[...]
    nbv_ref[0, :nb - sd, :] = cbv_ref[0, sd:, :]
    nbv_ref[0, nb - sd:, :] = vbf_ref[...]


def _sample_bias(rel_bias, sd, nb):
    bias = _toeplitz_bias(rel_bias, sd, nb + sd, nb)
    return bias[:, :, :nb], bias[:, :, nb:]


def _sample_attn(qa, kan, van, cak, cav, qb, kbn, vbn, kbf, vbf, cbk, cbv, biasp, biasn, ga, gb):
    nbatch, past, _ = cak.shape
    nb = cbk.shape[1]
    sd = qa.shape[0] // nbatch
    assert past % SB_BLOCK == 0 and sd % 8 == 0 and sd <= nb and sd <= LANES
    rows = pl.BlockSpec((sd, WIDTH), lambda b: (b, 0))
    cache = lambda n: pl.BlockSpec((1, n, WIDTH), lambda b: (b, 0, 0))
    full = lambda a: pl.BlockSpec(a.shape, lambda b: (0,) * a.ndim)
    mmn = _suffix_matrix(sd)
    mmp = _suffix_matrix(SB_BLOCK)
    return pl.pallas_call(
        _sample_attn_kernel,
        grid=(nbatch,),
        in_specs=[rows, rows, rows, cache(past), cache(past),
                  rows, rows, rows, rows, rows, cache(nb), cache(nb),
                  full(biasp), full(biasn), full(ga), full(gb), full(mmn), full(mmp)],
        out_specs=[rows, rows, cache(nb), cache(nb)],
        out_shape=[jax.ShapeDtypeStruct((nbatch * sd, WIDTH), BF16)] * 2
                  + [jax.ShapeDtypeStruct((nbatch, nb, WIDTH), F32)] * 2,
        compiler_params=_params("arbitrary"),
        name="sample_attn",
    )(qa, kan, van, cak, cav, qb, kbn, vbn, kbf, vbf, cbk, cbv, biasp, biasn, ga, gb, mmn, mmp)


def _merge_kernel(ca_ref, cb_ref, x_ref, woa_ref, wob_ref, g_ref, b_ref, wq_ref, sk_ref,
                  x1_ref, x1t_ref, st_ref, *, alpha):
    mix = _dot(ca_ref[...], woa_ref[...]) + _dot(cb_ref[...], wob_ref[...])
    x1 = _layer_norm(alpha * x_ref[...] + mix, g_ref[...], b_ref[...])
    x1_ref[...] = x1
    x1t_ref[...] = x1.T.astype(BF16)
    qp = _dot(x1.astype(BF16), wq_ref[...])
    half = sk_ref.shape[2]
    for h in range(PEER_HEADS):
        for j in range(2):
            c0 = (2 * h + j) * half
            qh = qp[:, c0:c0 + half].astype(BF16)
            st_ref[(2 * h + j) * N_KEYS:(2 * h + j + 1) * N_KEYS, :] = _nt_dot(sk_ref[j], qh)


def _merge(ca, cb, x, woa, wob, g, b, wq, sk, alpha):
    t, d = x.shape
    tm = min(ROW_TILE, t)
    assert t % tm == 0
    nscore = PEER_HEADS * 2 * N_KEYS
    row = lambda i: (i, 0)
    colb = lambda i: (0, i)
    full = lambda a: pl.BlockSpec(a.shape, lambda i: (0,) * a.ndim)
    return pl.pallas_call(
        functools.partial(_merge_kernel, alpha=alpha),
        grid=(t // tm,),
        in_specs=[pl.BlockSpec((tm, WIDTH), row), pl.BlockSpec((tm, WIDTH), row), pl.BlockSpec((tm, d), row),
                  full(woa), full(wob), full(g), full(b), full(wq), full(sk)],
        out_specs=[pl.BlockSpec((tm, d), row), pl.BlockSpec((d, tm), colb), pl.BlockSpec((nscore, tm), colb)],
        out_shape=[jax.ShapeDtypeStruct((t, d), F32), jax.ShapeDtypeStruct((d, t), BF16),
                   jax.ShapeDtypeStruct((nscore, t), F32)],
        compiler_params=_params("arbitrary"),
        name="merge",
    )(ca, cb, x, woa, wob, g, b, wq, sk)


def _cmp_exchange(v, i, j):
    a, b = v[i], v[j]
    v[i] = jnp.maximum(a, b)
    v[j] = jnp.minimum(a, b)


def _sort16_desc(v):
    v = list(v)
    n = len(v)
    k = 2
    while k <= n:
        j = k // 2
        while j >= 1:
            for i in range(n):
                l = i ^ j
                if l > i:
                    if (i & k) == 0:
                        _cmp_exchange(v, i, l)
                    else:
                        _cmp_exchange(v, l, i)
            j //= 2
        k *= 2
    return v


def _merge_top16(a, b):
    n = len(a)
    top = [jnp.maximum(a[i], b[n - 1 - i]) for i in range(n)]
    out = [jnp.minimum(a[i], b[n - 1 - i]) for i in range(n)]
    while len(out) > 1:
        out = [jnp.maximum(out[2 * i], out[2 * i + 1]) for i in range(len(out) // 2)]
    j = n // 2
    while j >= 1:
        for i in range(n):
            if (i & j) == 0:
                _cmp_exchange(top, i, i + j)
        j //= 2
    return top, out[0]


def _top16_and_next(vals):
    groups = [_sort16_desc(vals[g:g + PEER_TOPK]) for g in range(0, len(vals), PEER_TOPK)]
    nxt = None
    while len(groups) > 1:
        merged = []
        for g in range(0, len(groups), 2):
            top, left = _merge_top16(groups[g], groups[g + 1])
            merged.append(top)
            nxt = left if nxt is None else jnp.maximum(nxt, left)
        groups = merged
    return groups[0], nxt


def _gelu_tanh(x):
    return 0.5 * x * (1.0 + jnp.tanh(0.7978845608028654 * (x + 0.044715 * (x * x * x))))


def _bf16_bits(x):
    return pltpu.bitcast(x.astype(BF16).astype(F32), U32)


def _pack_row_pairs(x):
    groups = range(0, x.shape[0], 2 * SUBLANES)
    hi = jnp.concatenate([x[g:g + SUBLANES] for g in groups], axis=0)
    lo = jnp.concatenate([x[g + SUBLANES:g + 2 * SUBLANES] for g in groups], axis=0)
    return _bf16_bits(hi) | (_bf16_bits(lo) >> 16)


def _both_halves(x):
    b = _bf16_bits(x)
    return b | (b >> 16)


def _peer_gate_kernel(st_ref, kk_ref, ea_ref, cw_ref, ebw_ref, row_s, *, peer_tile):
    tt = st_ref.shape[1]
    ng = tt // LANES
    k1 = PEER_TOPK + 1

    def head(h, carry):
        base = pl.multiple_of(h * 2 * N_KEYS, 2 * N_KEYS)
        s1 = st_ref[pl.ds(base, N_KEYS), :]
        s2 = st_ref[pl.ds(base + N_KEYS, N_KEYS), :]
        s1r = s1.reshape(N_KEYS, ng, LANES)
        s2r = s2.reshape(N_KEYS, ng, LANES)
        top_a, next_a = _top16_and_next([s1r[a] for a in range(N_KEYS)])
        top_b, next_b = _top16_and_next([s2r[a] for a in range(N_KEYS)])
        la = top_a + [next_a]
        lb = top_b + [next_b]
        cands = [la[i - 1] + lb[j - 1] for i in range(1, k1 + 1) for j in range(1, k1 + 1) if i * j <= k1]
        pad = jnp.full_like(cands[0], PAD_SCORE)
        cands = cands + [pad] * (-len(cands) % PEER_TOPK)
        top_c, next_c = _top16_and_next(cands)
        tau = 0.5 * (top_c[PEER_TOPK - 1] + next_c)
        den = jnp.ones_like(tau)
        for cv in top_c[1:]:
            den = den + jnp.exp(cv - top_c[0])
        vals = [tau, la[0], 1.0 / den] + lb
        for r, val in enumerate(vals):
            for g in range(ng):
                row_s[r:r + 1, g * LANES:(g + 1) * LANES] = val[g:g + 1, :]
        tau_r = row_s[0:1, :]
        m1_r = row_s[1:2, :]
        iz_r = row_s[2:3, :]
        lb_r = [row_s[3 + j:4 + j, :] for j in range(k1)]
        code = jnp.zeros_like(s2)
        for j in reversed(range(k1)):
            code = jnp.where(s2 >= lb_r[j], float(k1 - j), code)
        th = tau_r - s1
        n = jnp.zeros_like(s1)
        for j in range(PEER_TOPK):
            n = jnp.where(lb_r[j] >= th, float(j + 1), n)
        kk_ref[:, pl.ds(h, 1), :] = _both_halves(float(k1 + 1) - n)[:, None, :]
        ea_ref[:, pl.ds(h, 1), :] = _both_halves(jnp.exp(s1 - m1_r) * iz_r)[:, None, :]
        rows = pl.ds(pl.multiple_of(h * (N_KEYS // 2), N_KEYS // 2), N_KEYS // 2)
        cw_ref[rows, :] = _pack_row_pairs(code)
        ebw = _pack_row_pairs(jnp.exp(s2 - lb_r[0]))
        for p0 in range(0, tt, peer_tile):
            p1 = p0 + peer_tile
            ebw_ref[rows, p0 + LANES:p1] = ebw[:, p0:p1 - LANES]
            ebw_ref[rows, p0:p0 + LANES] = ebw[:, p1 - LANES:p1]
        return carry

    lax.fori_loop(0, PEER_HEADS, head, 0)


def _peer_gates(st, peer_tile):
    nscore, t = st.shape
    tt = PEER_GATE_TILE if t % PEER_GATE_TILE == 0 else peer_tile
    assert t % tt == 0 and tt % peer_tile == 0 and peer_tile > LANES and peer_tile % LANES == 0
    rows = jax.ShapeDtypeStruct((N_KEYS, PEER_HEADS, t), U32)
    packed = jax.ShapeDtypeStruct((PEER_HEADS * N_KEYS // 2, t), U32)
    return pl.pallas_call(
        functools.partial(_peer_gate_kernel, peer_tile=peer_tile),
        grid=(t // tt,),
        in_specs=[pl.BlockSpec((nscore, tt), lambda i: (0, i))],
        out_specs=[pl.BlockSpec((N_KEYS, PEER_HEADS, tt), lambda i: (0, 0, i)),
                   pl.BlockSpec((N_KEYS, PEER_HEADS, tt), lambda i: (0, 0, i)),
                   pl.BlockSpec((PEER_HEADS * N_KEYS // 2, tt), lambda i: (0, i)),
                   pl.BlockSpec((PEER_HEADS * N_KEYS // 2, tt), lambda i: (0, i))],
        out_shape=[rows, rows, packed, packed],
        scratch_shapes=[pltpu.VMEM((3 * SUBLANES, tt), F32)],
        compiler_params=_params("arbitrary"),
        name="peer_gates",
    )(st)


def _peer_kernel(cw_ref, ebw_ref, kk_ref, ea_ref, xt_ref, u_ref, vt_ref, x1_ref, g_ref, b_ref, o_ref,
                 ht0_s, ht1_s, wt0_s, wt1_s, acc_s, *, alpha):
    c = pl.program_id(1)
    nc = pl.num_programs(1)
    tt = xt_ref.shape[1]
    ch = u_ref.shape[0]

    @pl.when(c == 0)
    def _clear():
        ht1_s[...] = jnp.zeros(ht1_s.shape, F32)
        wt0_s[...] = jnp.zeros(wt0_s.shape, BF16)
        wt1_s[...] = jnp.zeros(wt1_s.shape, BF16)
        acc_s[...] = jnp.zeros(acc_s.shape, F32)

    ablocks = ch // N_KEYS
    lt = PEER_LANE_TILE

    def stages(ht_new, ht_old, wt_new, wt_old):
        d = vt_ref.shape[0]
        npiece = ablocks // 2
        nlv = lt // LANES
        nr2 = N_KEYS // (2 * SUBLANES)
        assert nlv == 2
        for ts in range(tt // lt):
            ls = slice(ts * lt, (ts + 1) * lt)
            for unit in range(npiece * nlv):
                ap, tv = unit // nlv, unit % nlv
                piece = unit // 2
                if unit % 2 == 0:
                    rows = slice(piece * (ch // npiece), (piece + 1) * (ch // npiece))
                    lhs_ref, rhs_ref, kdim = u_ref, xt_ref, u_ref.shape[1]
                else:
                    rows = slice(piece * (d // npiece), (piece + 1) * (d // npiece))
                    lhs_ref, rhs_ref, kdim = vt_ref, wt_old, ch
                nk = kdim // MXU_DEPTH
                part = None
                g = ts * nlv + tv
                lv = slice(g * LANES, (g + 1) * LANES)
                ge = (g + 1) % (tt // LANES)
                le = slice(ge * LANES, (ge + 1) * LANES)
                als = (2 * ap, 2 * ap + 1)
                spread = lambda ref, al, h: pltpu.bitcast(jnp.broadcast_to(ref[al, h:h + 1, lv], (SUBLANES, LANES)), BF16)
                kk = [[spread(kk_ref, al, h) for h in range(PEER_HEADS)] for al in als]
                ea = [[spread(ea_ref, al, h) for h in range(PEER_HEADS)] for al in als]
                for r2 in range(nr2):
                    if r2 % (nr2 // nk) == 0:
                        kb = slice(r2 // (nr2 // nk) * MXU_DEPTH, (r2 // (nr2 // nk) + 1) * MXU_DEPTH)
                        dk = _dot(lhs_ref[rows, kb], rhs_ref[kb, ls])
                        part = dk if part is None else part + dk
                    gates = [jnp.zeros((2 * SUBLANES, LANES), BF16) for _ in als]
                    for h in range(PEER_HEADS):
                        wrows = slice(h * (N_KEYS // 2) + r2 * SUBLANES, h * (N_KEYS // 2) + (r2 + 1) * SUBLANES)
                        code = pltpu.bitcast(cw_ref[wrows, lv], BF16)
                        eb = pltpu.bitcast(ebw_ref[wrows, le], BF16)
                        for i in range(2):
                            gates[i] = gates[i] + jnp.where(code >= kk[i][h], eb, jnp.zeros_like(eb)) * ea[i][h]
                    for i, al in enumerate(als):
                        rs = slice(al * N_KEYS + r2 * 2 * SUBLANES, al * N_KEYS + (r2 + 1) * 2 * SUBLANES)
                        gw = pltpu.bitcast(gates[i], U32)
                        gate = jnp.concatenate([pltpu.bitcast(gw & jnp.uint32(0xFFFF0000), F32),
                                                pltpu.bitcast(gw << 16, F32)], axis=0)
                        wt_new[rs, lv] = (gate * _gelu_tanh(ht_old[rs, lv])).astype(BF16)
                if unit % 2 == 0:
                    ht_new[rows, ls] = part
                else:
                    acc_s[rows, ls] += part

    @pl.when(c % 2 == 0)
    def _even():
        stages(ht0_s, ht1_s, wt1_s, wt0_s)

    @pl.when(c % 2 == 1)
    def _odd():
        stages(ht1_s, ht0_s, wt0_s, wt1_s)

    @pl.when(c == nc - 1)
    def _finish():
        y = alpha * x1_ref[...] + acc_s[...].T
        o_ref[...] = _layer_norm(y, g_ref[...], b_ref[...])


def _peer(cw, ebw, kk, ea, xt, u_bf, vt_bf, x1, g, b, alpha, tt):
    t, d = x1.shape
    ne = u_bf.shape[0]
    ch = PEER_EXPERT_BLOCK
    assert t % tt == 0 and tt % PEER_LANE_TILE == 0 and ne % ch == 0 and ne == N_KEYS * N_KEYS
    assert vt_bf.shape == (ne // ch, d, ch)
    nblk = ne // ch
    ablocks = ch // N_KEYS
    gated = lambda i, c: (jnp.clip(c - 1, 0, nblk - 1), 0, i)
    return pl.pallas_call(
        functools.partial(_peer_kernel, alpha=alpha),
        grid=(t // tt, nblk + PEER_PIPE_FILL),
        in_specs=[
            pl.BlockSpec((PEER_HEADS * N_KEYS // 2, tt), lambda i, c: (0, i)),
            pl.BlockSpec((PEER_HEADS * N_KEYS // 2, tt), lambda i, c: (0, i)),
            pl.BlockSpec((ablocks, PEER_HEADS, tt), gated),
            pl.BlockSpec((ablocks, PEER_HEADS, tt), gated),
            pl.BlockSpec((d, tt), lambda i, c: (0, i)),
            pl.BlockSpec((ch, d), lambda i, c: (jnp.minimum(c, nblk - 1), 0)),
            pl.BlockSpec((None, d, ch), lambda i, c: (jnp.clip(c - PEER_PIPE_FILL, 0, nblk - 1), 0, 0)),
            pl.BlockSpec((tt, d), lambda i, c: (i, 0)),
            pl.BlockSpec((1, d), lambda i, c: (0, 0)),
            pl.BlockSpec((1, d), lambda i, c: (0, 0)),
        ],
        out_specs=pl.BlockSpec((tt, d), lambda i, c: (i, 0)),
        out_shape=jax.ShapeDtypeStruct((t, d), F32),
        scratch_shapes=[
            pltpu.VMEM((ch, tt), F32),
            pltpu.VMEM((ch, tt), F32),
            pltpu.VMEM((ch, tt), BF16),
            pltpu.VMEM((ch, tt), BF16),
            pltpu.VMEM((d, tt), F32),
        ],
        compiler_params=_params("arbitrary", "arbitrary"),
        name="peer",
    )(cw, ebw, kk, ea, xt, u_bf, vt_bf, x1, g, b)


def kernel(x_prompt, x_sample, cache_sb_k, cache_sb_v, cache_band_k, cache_band_v, w_in, w_out, gn_a, gn_b,
           rel_bias, ln1_g, ln1_b, peer_query, peer_subkeys, peer_u, peer_v, ln2_g, ln2_b):
    batch, seq, d = x_prompt.shape
    dec_batch, dec_seq, _ = x_sample.shape
    depth = w_in.shape[0]
    past = cache_sb_k.shape[2]
    nb = cache_band_k.shape[2]
    assert w_in.shape[2] == 6 * WIDTH and w_out.shape[1] == 2 * WIDTH
    assert seq >= BAND_PAST and nb == BAND_PAST
    alpha = float((2 * depth) ** 0.25)
    tp = batch * seq
    ts = dec_batch * dec_seq

    xp = x_prompt.reshape(tp, d)
    xs = x_sample.reshape(ts, d)
    row2 = lambda a: a.reshape(1, -1)
    heads = lambda a, n, s: a.reshape(n, s, N_HEADS, HEAD_DIM)
    outs = [[] for _ in range(8)]
    for l in range(depth):
        w_in_bf = w_in[l].astype(BF16)
        woa = w_out[l, :WIDTH].astype(BF16)
        wob = w_out[l, WIDTH:].astype(BF16)
        wq = peer_query[l].astype(BF16)
        sk = peer_subkeys[l].astype(BF16)
        u_bf = peer_u[l].astype(BF16)
        vt_bf = peer_v[l].reshape(-1, PEER_EXPERT_BLOCK, d).transpose(0, 2, 1).astype(BF16)
        ga, gb = row2(gn_a[l]), row2(gn_b[l])
        g1, b1, g2, b2 = row2(ln1_g[l]), row2(ln1_b[l]), row2(ln2_g[l]), row2(ln2_b[l])

        qa, ka, va, kab, vab, qb, kb, vb, kbb, vbb = _project(xp, w_in_bf)
        ca = _sb_prompt(qa, kab, vab, ga, batch, seq)
        cb = _band_prompt(qb, kbb, vbb, _band_prompt_bias(rel_bias[l]), gb, batch, seq)
        x1, x1t, st = _merge(ca, cb, xp, woa, wob, g1, b1, wq, sk, alpha)
        tt = min(PEER_TOKEN_TILE, tp)
        kk, ea, cw, ebw = _peer_gates(st, tt)
        xp = _peer(cw, ebw, kk, ea, x1t, u_bf, vt_bf, x1, g2, b2, alpha, tt)
        outs[0].append(heads(ka, batch, seq))
        outs[1].append(heads(va, batch, seq))
        outs[2].append(heads(kb, batch, seq)[:, seq - BAND_PAST:])
        outs[3].append(heads(vb, batch, seq)[:, seq - BAND_PAST:])

        qa, ka, va, kab, vab, qb, kb, vb, kbb, vbb = _project(xs, w_in_bf)
        biasp, biasn = _sample_bias(rel_bias[l], dec_seq, nb)
        ca, cb, nbk, nbv = _sample_attn(
            qa, kab, vab, cache_sb_k[l].reshape(dec_batch, past, WIDTH), cache_sb_v[l].reshape(dec_batch, past, WIDTH),
            qb, kbb, vbb, kb, vb, cache_band_k[l].reshape(dec_batch, nb, WIDTH),
            cache_band_v[l].reshape(dec_batch, nb, WIDTH), biasp, biasn, ga, gb)
        x1, x1t, st = _merge(ca, cb, xs, woa, wob, g1, b1, wq, sk, alpha)
        tt = min(PEER_TOKEN_TILE, ts)
        kk, ea, cw, ebw = _peer_gates(st, tt)
        xs = _peer(cw, ebw, kk, ea, x1t, u_bf, vt_bf, x1, g2, b2, alpha, tt)
        outs[4].append(heads(ka, dec_batch, dec_seq))
        outs[5].append(heads(va, dec_batch, dec_seq))
        outs[6].append(heads(nbk, dec_batch, nb))
        outs[7].append(heads(nbv, dec_batch, nb))

    return (xp.reshape(batch, seq, d), xs.reshape(dec_batch, dec_seq, d)) + tuple(jnp.stack(o) for o in outs)
```

```python
import functools

import jax
import jax.numpy as jnp
import numpy as np
from jax import lax
from jax.experimental import pallas as pl
from jax.experimental.pallas import tpu as pltpu

F32 = jnp.float32
BF16 = jnp.bfloat16
U32 = jnp.uint32

HEAD_DIM = 64
N_HEADS = 8
WIDTH = N_HEADS * HEAD_DIM
CHUNK = 64
BAND_CHUNKS = 8
BAND_PAST = BAND_CHUNKS * CHUNK
MAX_REL = 128
N_KEYS = 128
PEER_HEADS = 8
PEER_TOPK = 16
NORM_EPS = 1e-5
NEG_INF = -1e30

LANES = 128
SUBLANES = 8
MXU_DEPTH = 256
VMEM_LIMIT_BYTES = 56 * 1024 * 1024

SB_DEAD_LOG = -104.0
SB_BLOCK = 128
BAND_QBLOCK = 256
ROW_TILE = 512
PEER_TOKEN_TILE = 1024
PEER_GATE_TILE = 1024
PEER_EXPERT_BLOCK = 512
PEER_PIPE_FILL = 2
PEER_LANE_TILE = 256
PAD_SCORE = -3.0e38


def _params(*sem):
    return pltpu.CompilerParams(dimension_semantics=sem, vmem_limit_bytes=VMEM_LIMIT_BYTES)


def _nt_dot(a, b):
    return lax.dot_general(a, b, (((1,), (1,)), ((), ())), preferred_element_type=F32)


def _dot(a, b):
    return jnp.dot(a, b, preferred_element_type=F32)


def _layer_norm(y, g, b):
    mu = jnp.mean(y, axis=-1, keepdims=True)
    d = y - mu
    var = jnp.mean(d * d, axis=-1, keepdims=True)
    return d * lax.rsqrt(var + NORM_EPS) * g + b


def _head_rms(o, gain):
    ms = jnp.mean(o * o, axis=-1, keepdims=True)
    return o * lax.rsqrt(ms + NORM_EPS) * gain


def _proj_kernel(x_ref, w_ref, qa_ref, ka_ref, va_ref, kab_ref, vab_ref,
                 qb_ref, kb_ref, vb_ref, kbb_ref, vbb_ref):
    xb = x_ref[...].astype(BF16)
    scale = HEAD_DIM ** -0.5

    def group(g):
        return _dot(xb, w_ref[:, g * WIDTH:(g + 1) * WIDTH])

    qa_ref[...] = (group(0) * scale).astype(BF16)
    k = group(1)
    ka_ref[...] = k
    kab_ref[...] = k.astype(BF16)
    v = group(2)
    va_ref[...] = v
    vab_ref[...] = v.astype(BF16)
    qb_ref[...] = (group(3) * scale).astype(BF16)
    k = group(4)
    kb_ref[...] = k
    kbb_ref[...] = k.astype(BF16)
    v = group(5)
    vb_ref[...] = v
    vbb_ref[...] = v.astype(BF16)


def _project(x, w_bf):
    t, d = x.shape
    tm = min(ROW_TILE, t)
    assert t % tm == 0
    row = lambda i: (i, 0)
    f32o = jax.ShapeDtypeStruct((t, WIDTH), F32)
    bf16o = jax.ShapeDtypeStruct((t, WIDTH), BF16)
    blk = pl.BlockSpec((tm, WIDTH), row)
    return pl.pallas_call(
        _proj_kernel,
        grid=(t // tm,),
        in_specs=[pl.BlockSpec((tm, d), row), pl.BlockSpec(w_bf.shape, lambda i: (0, 0))],
        out_specs=[blk] * 10,
        out_shape=[bf16o, f32o, f32o, bf16o, bf16o, bf16o, f32o, f32o, bf16o, bf16o],
        compiler_params=_params("arbitrary"),
        name="proj",
    )(x, w_bf)


def _proj_prompt_kernel(x_ref, w_ref, qa_ref, kab_ref, vab_ref, qb_ref, kbb_ref, vbb_ref,
                        sbk_ref, sbv_ref, bdk_ref, bdv_ref, *, tiles_per_stream):
    j = pl.program_id(0) % tiles_per_stream
    xb = x_ref[...].astype(BF16)
    scale = HEAD_DIM ** -0.5

    def group(g):
        return _dot(xb, w_ref[:, g * WIDTH:(g + 1) * WIDTH])

    qa_ref[...] = (group(0) * scale).astype(BF16)
    k = group(1)
    kab_ref[...] = k.astype(BF16)
    sbk_ref[0] = k.T
    v = group(2)
    vab_ref[...] = v.astype(BF16)
    sbv_ref[0] = v.T
    qb_ref[...] = (group(3) * scale).astype(BF16)
    kb = group(4)
    kbb_ref[...] = kb.astype(BF16)
    vb = group(5)
    vbb_ref[...] = vb.astype(BF16)

    @pl.when(j == tiles_per_stream - 1)
    def _band_rows():
        bdk_ref[0] = kb.T
        bdv_ref[0] = vb.T


def _project_prompt(x, w_bf, batch, seq):
    t, d = x.shape
    tm = ROW_TILE
    assert seq % tm == 0 and tm == BAND_PAST and t == batch * seq
    nj = seq // tm
    row = lambda i: (i, 0)
    bf16o = jax.ShapeDtypeStruct((t, WIDTH), BF16)
    blk = pl.BlockSpec((tm, WIDTH), row)
    return pl.pallas_call(
        functools.partial(_proj_prompt_kernel, tiles_per_stream=nj),
        grid=(t // tm,),
        in_specs=[pl.BlockSpec((tm, d), row), pl.BlockSpec(w_bf.shape, lambda i: (0, 0))],
        out_specs=[blk] * 6 + [pl.BlockSpec((1, WIDTH, tm), lambda i: (i // nj, 0, i % nj))] * 2
                  + [pl.BlockSpec((1, WIDTH, tm), lambda i: (i // nj, 0, 0))] * 2,
        out_shape=[bf16o] * 6 + [jax.ShapeDtypeStruct((batch, WIDTH, seq), F32)] * 2
                  + [jax.ShapeDtypeStruct((batch, WIDTH, tm), F32)] * 2,
        compiler_params=_params("arbitrary"),
        name="proj_prompt",
    )(x, w_bf)


def _suffix_matrix(kb):
    kp = np.arange(kb)
    m = (kp[:, None] > kp[None, :]).astype(np.float32)
    one = np.concatenate([np.ones((kb, LANES), np.float32), m], axis=1)
    return jnp.asarray(np.concatenate([one, one], axis=0), dtype=BF16)


def _sb_log_terms(z, mask):
    sp = jnp.maximum(z, 0.0) + jnp.log(1.0 + jnp.exp(-jnp.abs(z)))
    log_keep = -sp
    if mask is not None:
        log_keep = jnp.where(mask, log_keep, 0.0)
    hi = log_keep.astype(BF16)
    lo = (log_keep - hi.astype(F32)).astype(BF16)
    return z - sp, jnp.concatenate([hi, lo], axis=1)


def _sb_weights(log_sig, sums, carry, mask):
    kb = log_sig.shape[1]
    log_w = log_sig + sums[:, LANES:LANES + kb]
    if carry is not None:
        log_w = log_w + carry
    w = jnp.exp(log_w)
    if mask is not None:
        w = jnp.where(mask, w, 0.0)
    return w, sums[:, :LANES]


def _sb_prompt_kernel(q_ref, k_ref, v_ref, g_ref, mm_ref, o_ref, carry_s, acc_s):
    i = pl.program_id(1)
    r = q_ref.shape[0]
    row = lax.broadcasted_iota(jnp.int32, (r, r), 0)
    col = lax.broadcasted_iota(jnp.int32, (r, r), 1)
    diag = col < row
    heads = [slice(h * HEAD_DIM, (h + 1) * HEAD_DIM) for h in range(N_HEADS)]

    def key_block(j, first):
        off = pl.multiple_of(j * r, r)
        q = q_ref[...]
        kj = k_ref[pl.ds(off, r), :]
        vj = v_ref[pl.ds(off, r), :]
        mm = mm_ref[...]
        mask = diag if first else None
        zs = [_nt_dot(q[:, hs], kj[:, hs]) for hs in heads]
        terms = [_sb_log_terms(z, mask) for z in zs]
        sums = [_dot(halves, mm) for _, halves in terms]
        ws = [_sb_weights(terms[h][0], sums[h], None if first else carry_s[h], mask) for h in range(N_HEADS)]
        pvs = [_dot(ws[h][0].astype(BF16), vj[:, hs]) for h, hs in enumerate(heads)]
        alive = None
        for h in range(N_HEADS):
            carry = ws[h][1] if first else carry_s[h] + ws[h][1]
            carry_s[h] = carry
            acc_s[h] = pvs[h] if first else acc_s[h] + pvs[h]
            alive = carry if alive is None else jnp.maximum(alive, carry)
        return jnp.max(alive)

    def cond(s):
        j, cmax = s
        return jnp.logical_and(j >= 0, cmax > SB_DEAD_LOG)

    def body(s):
        j, _ = s
        return j - 1, key_block(j, False)

    lax.while_loop(cond, body, (i - 1, key_block(i, True)))
    outs = [_head_rms(acc_s[h], g_ref[:, hs]) for h, hs in enumerate(heads)]
    o_ref[...] = jnp.concatenate(outs, axis=1).astype(BF16)


def _sb_prompt(q, k, v, gain, batch, seq):
    r = SB_BLOCK
    assert seq % r == 0
    nq = seq // r
    return pl.pallas_call(
        _sb_prompt_kernel,
        grid=(batch, nq),
        in_specs=[
            pl.BlockSpec((r, WIDTH), lambda b, i: (b * nq + i, 0)),
            pl.BlockSpec((seq, WIDTH), lambda b, i: (b, 0)),
            pl.BlockSpec((seq, WIDTH), lambda b, i: (b, 0)),
            pl.BlockSpec((1, WIDTH), lambda b, i: (0, 0)),
            pl.BlockSpec((2 * r, LANES + r), lambda b, i: (0, 0)),
        ],
        out_specs=pl.BlockSpec((r, WIDTH), lambda b, i: (b * nq + i, 0)),
        out_shape=jax.ShapeDtypeStruct((batch * seq, WIDTH), BF16),
        scratch_shapes=[pltpu.VMEM((N_HEADS, r, LANES), F32),
                        pltpu.VMEM((N_HEADS, r, HEAD_DIM), F32)],
        compiler_params=_params("arbitrary", "arbitrary"),
        name="sb_prompt",
    )(q, k, v, gain, _suffix_matrix(r))


def _band_prompt_kernel(q_ref, k0_ref, k1_ref, k2_ref, v0_ref, v1_ref, v2_ref, bias_ref, g_ref, o_ref):
    i = pl.program_id(1)
    k_refs = (k0_ref, k1_ref, k2_ref)
    v_refs = (v0_ref, v1_ref, v2_ref)
    outs = []
    for h in range(N_HEADS):
        hs = slice(h * HEAD_DIM, (h + 1) * HEAD_DIM)
        qh = q_ref[:, hs]
        scs = []
        for w in range(3):
            sc = _nt_dot(qh, k_refs[w][:, hs]) + bias_ref[h, w]
            if w < 2:
                sc = jnp.where(i >= 2 - w, sc, NEG_INF)
            scs.append(sc)
        m = jnp.maximum(jnp.maximum(jnp.max(scs[0], axis=-1, keepdims=True),
                                    jnp.max(scs[1], axis=-1, keepdims=True)),
                        jnp.max(scs[2], axis=-1, keepdims=True))
        es = [jnp.exp(sc - m) for sc in scs]
        den = (jnp.sum(es[0], axis=-1, keepdims=True) + jnp.sum(es[1], axis=-1, keepdims=True)
               + jnp.sum(es[2], axis=-1, keepdims=True))
        acc = (_dot(es[0].astype(BF16), v_refs[0][:, hs]) + _dot(es[1].astype(BF16), v_refs[1][:, hs])
               + _dot(es[2].astype(BF16), v_refs[2][:, hs]))
        outs.append(_head_rms(acc / den, g_ref[:, hs]))
    o_ref[...] = jnp.concatenate(outs, axis=1).astype(BF16)


def _toeplitz_bias(rel_bias, n, ncols, offset):
    period = n + ncols + 1
    m = jnp.arange(period)
    shift = jnp.where(m < ncols, m, m - period)
    vec = rel_bias[:, jnp.clip(offset - shift, -MAX_REL, MAX_REL) + MAX_REL].astype(F32)
    rows = jnp.tile(vec, (1, n))[:, :n * (period - 1)].reshape(rel_bias.shape[0], n, period - 1)
    return rows[:, :, :ncols]


def _band_prompt_bias(rel_bias):
    qb = BAND_QBLOCK
    bias = _toeplitz_bias(rel_bias, qb, 3 * qb, 2 * qb)
    kc = jnp.arange(3 * qb)[None, :] // CHUNK - (2 * qb // CHUNK)
    qc = jnp.arange(qb)[:, None] // CHUNK
    valid = (kc <= qc) & (kc >= qc - BAND_CHUNKS)
    bias = jnp.where(valid[None], bias, NEG_INF)
    return bias.reshape(N_HEADS, qb, 3, qb).transpose(0, 2, 1, 3)


def _band_prompt(q, k, v, bias, gain, batch, seq):
    qb = BAND_QBLOCK
    assert seq % qb == 0 and 2 * qb == BAND_PAST and qb % CHUNK == 0
    nq = seq // qb
    kspec = lambda back: pl.BlockSpec((qb, WIDTH), lambda b, i: (b * nq + jnp.maximum(i - back, 0), 0))
    return pl.pallas_call(
        _band_prompt_kernel,
        grid=(batch, nq),
        in_specs=[
            pl.BlockSpec((qb, WIDTH), lambda b, i: (b * nq + i, 0)),
            kspec(2), kspec(1), kspec(0), kspec(2), kspec(1), kspec(0),
            pl.BlockSpec(bias.shape, lambda b, i: (0, 0, 0, 0)),
            pl.BlockSpec((1, WIDTH), lambda b, i: (0, 0)),
        ],
        out_specs=pl.BlockSpec((qb, WIDTH), lambda b, i: (b * nq + i, 0)),
        out_shape=jax.ShapeDtypeStruct((batch * seq, WIDTH), BF16),
        compiler_params=_params("arbitrary", "arbitrary"),
        name="band_prompt",
    )(q, k, k, k, v, v, v, bias, gain)


def _sample_attn_kernel(qa_ref, kan_ref, van_ref, cak_ref, cav_ref,
                        qb_ref, kbn_ref, vbn_ref, kbf_ref, vbf_ref, cbk_ref, cbv_ref,
                        biasp_ref, biasn_ref, ga_ref, gb_ref, mmn_ref, mmp_ref,
                        oa_ref, ob_ref, nbk_ref, nbv_ref):
    sd = qa_ref.shape[0]
    past = cak_ref.shape[2]
    nb = cbk_ref.shape[2]
    row = lax.broadcasted_iota(jnp.int32, (sd, sd), 0)
    col = lax.broadcasted_iota(jnp.int32, (sd, sd), 1)
    diag = col < row
    cak = cak_ref[0].astype(BF16)
    cav = cav_ref[0].astype(BF16)
    cbk = cbk_ref[0].astype(BF16)
    cbv = cbv_ref[0].astype(BF16)
    mmn = mmn_ref[...]
    mmp = mmp_ref[...]
    heads = [slice(h * HEAD_DIM, (h + 1) * HEAD_DIM) for h in range(N_HEADS)]
    blocks = [slice(j * SB_BLOCK, (j + 1) * SB_BLOCK) for j in range(past // SB_BLOCK)]
    qa = qa_ref[...]
    z_new = [_nt_dot(qa[:, hs], kan_ref[:, hs]) for hs in heads]
    z_past = [_dot(qa[:, hs], cak[hs, :]) for hs in heads]
    t_new = [_sb_log_terms(z, diag) for z in z_new]
    t_past = [[_sb_log_terms(z[:, ks], None) for ks in blocks] for z in z_past]
    s_new = [_dot(halves, mmn) for _, halves in t_new]
    s_past = [[_dot(halves, mmp) for _, halves in t] for t in t_past]
    outs_a = []
    for h, hs in enumerate(heads):
        w_new, carry = _sb_weights(t_new[h][0], s_new[h], None, diag)
        w_past = [None] * len(blocks)
        for j in reversed(range(len(blocks))):
            w_past[j], total = _sb_weights(t_past[h][j][0], s_past[h][j], carry, None)
            carry = carry + total
        acc = _dot(w_new.astype(BF16), van_ref[:, hs]) + _nt_dot(jnp.concatenate(w_past, axis=1).astype(BF16), cav[hs, :])
        outs_a.append(_head_rms(acc, ga_ref[:, hs]))
    qb = qb_ref[...]
    sc_p = [_dot(qb[:, hs], cbk[hs, :]) + biasp_ref[h] for h, hs in enumerate(heads)]
    sc_n = [_nt_dot(qb[:, hs], kbn_ref[:, hs]) + biasn_ref[h] for h, hs in enumerate(heads)]
    outs_b = []
    for h, hs in enumerate(heads):
        m = jnp.maximum(jnp.max(sc_p[h], axis=-1, keepdims=True), jnp.max(sc_n[h], axis=-1, keepdims=True))
        ep = jnp.exp(sc_p[h] - m)
        en = jnp.exp(sc_n[h] - m)
        den = jnp.sum(ep, axis=-1, keepdims=True) + jnp.sum(en, axis=-1, keepdims=True)
        acc = _nt_dot(ep.astype(BF16), cbv[hs, :]) + _dot(en.astype(BF16), vbn_ref[:, hs])
        outs_b.append(_head_rms(acc / den, gb_ref[:, hs]))
    oa_ref[...] = jnp.concatenate(outs_a, axis=1).astype(BF16)
    ob_ref[...] = jnp.concatenate(outs_b, axis=1).astype(BF16)
    nbk_ref[0, :, :nb - sd] = cbk_ref[0, :, sd:]
    nbk_ref[0, :, nb - sd:] = kbf_ref[...].T
    nbv_ref[0, :, :nb - sd] = cbv_ref[0, :, sd:]
    nbv_ref[0, :, nb - sd:] = vbf_ref[...].T


def _sample_bias(rel_bias, sd, nb):
    bias = _toeplitz_bias(rel_bias, sd, nb + sd, nb)
    return bias[:, :, :nb], bias[:, :, nb:]


def _sample_attn(qa, kan, van, cak, cav, qb, kbn, vbn, kbf, vbf, cbk, cbv, biasp, biasn, ga, gb, layer):
    _, nbatch, _, past = cak.shape
    nb = cbk.shape[3]
    sd = qa.shape[0] // nbatch
    assert past % SB_BLOCK == 0 and sd % 8 == 0 and sd <= nb and sd <= LANES
    rows = pl.BlockSpec((sd, WIDTH), lambda b: (b, 0))
    cache = lambda n: pl.BlockSpec((None, 1, WIDTH, n), lambda b: (layer, b, 0, 0))
    rolled = pl.BlockSpec((1, WIDTH, nb), lambda b: (b, 0, 0))
    full = lambda a: pl.BlockSpec(a.shape, lambda b: (0,) * a.ndim)
    mmn = _suffix_matrix(sd)
    mmp = _suffix_matrix(SB_BLOCK)
    return pl.pallas_call(
        _sample_attn_kernel,
        grid=(nbatch,),
        in_specs=[rows, rows, rows, cache(past), cache(past),
                  rows, rows, rows, rows, rows, cache(nb), cache(nb),
                  full(biasp), full(biasn), full(ga), full(gb), full(mmn), full(mmp)],
        out_specs=[rows, rows, rolled, rolled],
        out_shape=[jax.ShapeDtypeStruct((nbatch * sd, WIDTH), BF16)] * 2
                  + [jax.ShapeDtypeStruct((nbatch, WIDTH, nb), F32)] * 2,
        compiler_params=_params("arbitrary"),
        name="sample_attn",
    )(qa, kan, van, cak, cav, qb, kbn, vbn, kbf, vbf, cbk, cbv, biasp, biasn, ga, gb, mmn, mmp)


def _merge_kernel(ca_ref, cb_ref, x_ref, woa_ref, wob_ref, g_ref, b_ref, wq_ref, sk_ref,
                  x1_ref, x1t_ref, st_ref, *, alpha):
    mix = _dot(ca_ref[...], woa_ref[...]) + _dot(cb_ref[...], wob_ref[...])
    x1 = _layer_norm(alpha * x_ref[...] + mix, g_ref[...], b_ref[...])
    x1_ref[...] = x1
    x1t_ref[...] = x1.T.astype(BF16)
    qp = _dot(x1.astype(BF16), wq_ref[...])
    half = sk_ref.shape[2]
    for h in range(PEER_HEADS):
        for j in range(2):
            c0 = (2 * h + j) * half
            qh = qp[:, c0:c0 + half].astype(BF16)
            st_ref[(2 * h + j) * N_KEYS:(2 * h + j + 1) * N_KEYS, :] = _nt_dot(sk_ref[j], qh)


def _merge(ca, cb, x, woa, wob, g, b, wq, sk, alpha):
    t, d = x.shape
    tm = min(ROW_TILE, t)
    assert t % tm == 0
    nscore = PEER_HEADS * 2 * N_KEYS
    row = lambda i: (i, 0)
    colb = lambda i: (0, i)
    full = lambda a: pl.BlockSpec(a.shape, lambda i: (0,) * a.ndim)
    return pl.pallas_call(
        functools.partial(_merge_kernel, alpha=alpha),
        grid=(t // tm,),
        in_specs=[pl.BlockSpec((tm, WIDTH), row), pl.BlockSpec((tm, WIDTH), row), pl.BlockSpec((tm, d), row),
                  full(woa), full(wob), full(g), full(b), full(wq), full(sk)],
        out_specs=[pl.BlockSpec((tm, d), row), pl.BlockSpec((d, tm), colb), pl.BlockSpec((nscore, tm), colb)],
        out_shape=[jax.ShapeDtypeStruct((t, d), F32), jax.ShapeDtypeStruct((d, t), BF16),
                   jax.ShapeDtypeStruct((nscore, t), F32)],
        compiler_params=_params("arbitrary"),
        name="merge",
    )(ca, cb, x, woa, wob, g, b, wq, sk)


def _cmp_exchange(v, i, j):
    a, b = v[i], v[j]
    v[i] = jnp.maximum(a, b)
    v[j] = jnp.minimum(a, b)


def _sort16_desc(v):
    v = list(v)
    n = len(v)
    k = 2
    while k <= n:
        j = k // 2
        while j >= 1:
            for i in range(n):
                l = i ^ j
                if l > i:
                    if (i & k) == 0:
                        _cmp_exchange(v, i, l)
                    else:
                        _cmp_exchange(v, l, i)
            j //= 2
        k *= 2
    return v


def _merge_top16(a, b):
    n = len(a)
    top = [jnp.maximum(a[i], b[n - 1 - i]) for i in range(n)]
    out = [jnp.minimum(a[i], b[n - 1 - i]) for i in range(n)]
    while len(out) > 1:
        out = [jnp.maximum(out[2 * i], out[2 * i + 1]) for i in range(len(out) // 2)]
    j = n // 2
    while j >= 1:
        for i in range(n):
            if (i & j) == 0:
                _cmp_exchange(top, i, i + j)
        j //= 2
    return top, out[0]


def _top16_and_next(vals):
    groups = [_sort16_desc(vals[g:g + PEER_TOPK]) for g in range(0, len(vals), PEER_TOPK)]
    nxt = None
    while len(groups) > 1:
        merged = []
        for g in range(0, len(groups), 2):
            top, left = _merge_top16(groups[g], groups[g + 1])
            merged.append(top)
            nxt = left if nxt is None else jnp.maximum(nxt, left)
        groups = merged
    return groups[0], nxt


def _gelu_tanh(x):
    return 0.5 * x * (1.0 + jnp.tanh(0.7978845608028654 * (x + 0.044715 * (x * x * x))))


def _bf16_bits(x):
    return pltpu.bitcast(x.astype(BF16).astype(F32), U32)


def _pack_row_pairs(x):
    groups = range(0, x.shape[0], 2 * SUBLANES)
    hi = jnp.concatenate([x[g:g + SUBLANES] for g in groups], axis=0)
    lo = jnp.concatenate([x[g + SUBLANES:g + 2 * SUBLANES] for g in groups], axis=0)
    return _bf16_bits(hi) | (_bf16_bits(lo) >> 16)


def _both_halves(x):
    b = _bf16_bits(x)
    return b | (b >> 16)


def _peer_gate_kernel(st_ref, kk_ref, ea_ref, cw_ref, ebw_ref, row_s, *, peer_tile):
    tt = st_ref.shape[1]
    ng = tt // LANES
    k1 = PEER_TOPK + 1

    def head(h, carry):
        base = pl.multiple_of(h * 2 * N_KEYS, 2 * N_KEYS)
        s1 = st_ref[pl.ds(base, N_KEYS), :]
        s2 = st_ref[pl.ds(base + N_KEYS, N_KEYS), :]
        s1r = s1.reshape(N_KEYS, ng, LANES)
        s2r = s2.reshape(N_KEYS, ng, LANES)
        top_a, next_a = _top16_and_next([s1r[a] for a in range(N_KEYS)])
        top_b, next_b = _top16_and_next([s2r[a] for a in range(N_KEYS)])
        la = top_a + [next_a]
        lb = top_b + [next_b]
        cands = [la[i - 1] + lb[j - 1] for i in range(1, k1 + 1) for j in range(1, k1 + 1) if i * j <= k1]
        pad = jnp.full_like(cands[0], PAD_SCORE)
        cands = cands + [pad] * (-len(cands) % PEER_TOPK)
        top_c, next_c = _top16_and_next(cands)
        tau = 0.5 * (top_c[PEER_TOPK - 1] + next_c)
        den = jnp.ones_like(tau)
        for cv in top_c[1:]:
            den = den + jnp.exp(cv - top_c[0])
        vals = [tau, la[0], 1.0 / den] + lb
        for r, val in enumerate(vals):
            for g in range(ng):
                row_s[r:r + 1, g * LANES:(g + 1) * LANES] = val[g:g + 1, :]
        tau_r = row_s[0:1, :]
        m1_r = row_s[1:2, :]
        iz_r = row_s[2:3, :]
        lb_r = [row_s[3 + j:4 + j, :] for j in range(k1)]
        code = jnp.zeros_like(s2)
        for j in reversed(range(k1)):
            code = jnp.where(s2 >= lb_r[j], float(k1 - j), code)
        th = tau_r - s1
        n = jnp.zeros_like(s1)
        for j in range(PEER_TOPK):
            n = jnp.where(lb_r[j] >= th, float(j + 1), n)
        kk_ref[:, pl.ds(h, 1), :] = _both_halves(float(k1 + 1) - n)[:, None, :]
        ea_ref[:, pl.ds(h, 1), :] = _both_halves(jnp.exp(s1 - m1_r) * iz_r)[:, None, :]
        rows = pl.ds(pl.multiple_of(h * (N_KEYS // 2), N_KEYS // 2), N_KEYS // 2)
        cw_ref[rows, :] = _pack_row_pairs(code)
        ebw = _pack_row_pairs(jnp.exp(s2 - lb_r[0]))
        for p0 in range(0, tt, peer_tile):
            p1 = p0 + peer_tile
            ebw_ref[rows, p0 + LANES:p1] = ebw[:, p0:p1 - LANES]
            ebw_ref[rows, p0:p0 + LANES] = ebw[:, p1 - LANES:p1]
        return carry

    lax.fori_loop(0, PEER_HEADS, head, 0)


def _peer_gates(st, peer_tile):
    nscore, t = st.shape
    tt = PEER_GATE_TILE if t % PEER_GATE_TILE == 0 else peer_tile
    assert t % tt == 0 and tt % peer_tile == 0 and peer_tile > LANES and peer_tile % LANES == 0
    rows = jax.ShapeDtypeStruct((N_KEYS, PEER_HEADS, t), U32)
    packed = jax.ShapeDtypeStruct((PEER_HEADS * N_KEYS // 2, t), U32)
    return pl.pallas_call(
        functools.partial(_peer_gate_kernel, peer_tile=peer_tile),
        grid=(t // tt,),
        in_specs=[pl.BlockSpec((nscore, tt), lambda i: (0, i))],
        out_specs=[pl.BlockSpec((N_KEYS, PEER_HEADS, tt), lambda i: (0, 0, i)),
                   pl.BlockSpec((N_KEYS, PEER_HEADS, tt), lambda i: (0, 0, i)),
                   pl.BlockSpec((PEER_HEADS * N_KEYS // 2, tt), lambda i: (0, i)),
                   pl.BlockSpec((PEER_HEADS * N_KEYS // 2, tt), lambda i: (0, i))],
        out_shape=[rows, rows, packed, packed],
        scratch_shapes=[pltpu.VMEM((3 * SUBLANES, tt), F32)],
        compiler_params=_params("arbitrary"),
        name="peer_gates",
    )(st)


def _peer_kernel(cw_ref, ebw_ref, kk_ref, ea_ref, xt_ref, u_ref, vt_ref, x1_ref, g_ref, b_ref, o_ref,
                 ht0_s, ht1_s, wt0_s, wt1_s, acc_s, *, alpha):
    c = pl.program_id(1)
    nc = pl.num_programs(1)
    tt = xt_ref.shape[1]
    ch = u_ref.shape[0]

    @pl.when(c == 0)
    def _clear():
        ht1_s[...] = jnp.zeros(ht1_s.shape, F32)
        wt0_s[...] = jnp.zeros(wt0_s.shape, BF16)
        wt1_s[...] = jnp.zeros(wt1_s.shape, BF16)
        acc_s[...] = jnp.zeros(acc_s.shape, F32)

    ablocks = ch // N_KEYS
    lt = PEER_LANE_TILE

    def stages(ht_new, ht_old, wt_new, wt_old):
        d = vt_ref.shape[0]
        npiece = ablocks // 2
        nlv = lt // LANES
        nr2 = N_KEYS // (2 * SUBLANES)
        assert nlv == 2
        for ts in range(tt // lt):
            ls = slice(ts * lt, (ts + 1) * lt)
            for unit in range(npiece * nlv):
                ap, tv = unit // nlv, unit % nlv
                piece = unit // 2
                if unit % 2 == 0:
                    rows = slice(piece * (ch // npiece), (piece + 1) * (ch // npiece))
                    lhs_ref, rhs_ref, kdim = u_ref, xt_ref, u_ref.shape[1]
                else:
                    rows = slice(piece * (d // npiece), (piece + 1) * (d // npiece))
                    lhs_ref, rhs_ref, kdim = vt_ref, wt_old, ch
                nk = kdim // MXU_DEPTH
                part = None
                g = ts * nlv + tv
                lv = slice(g * LANES, (g + 1) * LANES)
                ge = (g + 1) % (tt // LANES)
                le = slice(ge * LANES, (ge + 1) * LANES)
                als = (2 * ap, 2 * ap + 1)
                spread = lambda ref, al, h: pltpu.bitcast(jnp.broadcast_to(ref[al, h:h + 1, lv], (SUBLANES, LANES)), BF16)
                kk = [[spread(kk_ref, al, h) for h in range(PEER_HEADS)] for al in als]
                ea = [[spread(ea_ref, al, h) for h in range(PEER_HEADS)] for al in als]
                for r2 in range(nr2):
                    if r2 % (nr2 // nk) == 0:
                        kb = slice(r2 // (nr2 // nk) * MXU_DEPTH, (r2 // (nr2 // nk) + 1) * MXU_DEPTH)
                        dk = _dot(lhs_ref[rows, kb], rhs_ref[kb, ls])
                        part = dk if part is None else part + dk
                    gates = [jnp.zeros((2 * SUBLANES, LANES), BF16) for _ in als]
                    for h in range(PEER_HEADS):
                        wrows = slice(h * (N_KEYS // 2) + r2 * SUBLANES, h * (N_KEYS // 2) + (r2 + 1) * SUBLANES)
                        code = pltpu.bitcast(cw_ref[wrows, lv], BF16)
                        eb = pltpu.bitcast(ebw_ref[wrows, le], BF16)
                        for i in range(2):
                            gates[i] = gates[i] + jnp.where(code >= kk[i][h], eb, jnp.zeros_like(eb)) * ea[i][h]
                    for i, al in enumerate(als):
                        rs = slice(al * N_KEYS + r2 * 2 * SUBLANES, al * N_KEYS + (r2 + 1) * 2 * SUBLANES)
                        gw = pltpu.bitcast(gates[i], U32)
                        gate = jnp.concatenate([pltpu.bitcast(gw & jnp.uint32(0xFFFF0000), F32),
                                                pltpu.bitcast(gw << 16, F32)], axis=0)
                        wt_new[rs, lv] = (gate * _gelu_tanh(ht_old[rs, lv])).astype(BF16)
                if unit % 2 == 0:
                    ht_new[rows, ls] = part
                else:
                    acc_s[rows, ls] += part

    @pl.when(c % 2 == 0)
    def _even():
        stages(ht0_s, ht1_s, wt1_s, wt0_s)

    @pl.when(c % 2 == 1)
    def _odd():
        stages(ht1_s, ht0_s, wt0_s, wt1_s)

    @pl.when(c == nc - 1)
    def _finish():
        y = alpha * x1_ref[...] + acc_s[...].T
        o_ref[...] = _layer_norm(y, g_ref[...], b_ref[...])


def _peer(cw, ebw, kk, ea, xt, u_bf, vt_bf, x1, g, b, alpha, tt):
    t, d = x1.shape
    ne = u_bf.shape[0]
    ch = PEER_EXPERT_BLOCK
    assert t % tt == 0 and tt % PEER_LANE_TILE == 0 and ne % ch == 0 and ne == N_KEYS * N_KEYS
    assert vt_bf.shape == (ne // ch, d, ch)
    nblk = ne // ch
    ablocks = ch // N_KEYS
    gated = lambda i, c: (jnp.clip(c - 1, 0, nblk - 1), 0, i)
    return pl.pallas_call(
        functools.partial(_peer_kernel, alpha=alpha),
        grid=(t // tt, nblk + PEER_PIPE_FILL),
        in_specs=[
            pl.BlockSpec((PEER_HEADS * N_KEYS // 2, tt), lambda i, c: (0, i)),
            pl.BlockSpec((PEER_HEADS * N_KEYS // 2, tt), lambda i, c: (0, i)),
            pl.BlockSpec((ablocks, PEER_HEADS, tt), gated),
            pl.BlockSpec((ablocks, PEER_HEADS, tt), gated),
            pl.BlockSpec((d, tt), lambda i, c: (0, i)),
            pl.BlockSpec((ch, d), lambda i, c: (jnp.minimum(c, nblk - 1), 0)),
            pl.BlockSpec((None, d, ch), lambda i, c: (jnp.clip(c - PEER_PIPE_FILL, 0, nblk - 1), 0, 0)),
            pl.BlockSpec((tt, d), lambda i, c: (i, 0)),
            pl.BlockSpec((1, d), lambda i, c: (0, 0)),
            pl.BlockSpec((1, d), lambda i, c: (0, 0)),
        ],
        out_specs=pl.BlockSpec((tt, d), lambda i, c: (i, 0)),
        out_shape=jax.ShapeDtypeStruct((t, d), F32),
        scratch_shapes=[
            pltpu.VMEM((ch, tt), F32),
            pltpu.VMEM((ch, tt), F32),
            pltpu.VMEM((ch, tt), BF16),
            pltpu.VMEM((ch, tt), BF16),
            pltpu.VMEM((d, tt), F32),
        ],
        compiler_params=_params("arbitrary", "arbitrary"),
        name="peer",
    )(cw, ebw, kk, ea, xt, u_bf, vt_bf, x1, g, b)


def kernel(x_prompt, x_sample, cache_sb_k, cache_sb_v, cache_band_k, cache_band_v, w_in, w_out, gn_a, gn_b,
           rel_bias, ln1_g, ln1_b, peer_query, peer_subkeys, peer_u, peer_v, ln2_g, ln2_b):
    batch, seq, d = x_prompt.shape
    dec_batch, dec_seq, _ = x_sample.shape
    depth = w_in.shape[0]
    past = cache_sb_k.shape[2]
    nb = cache_band_k.shape[2]
    assert w_in.shape[2] == 6 * WIDTH and w_out.shape[1] == 2 * WIDTH
    assert seq >= BAND_PAST and nb == BAND_PAST
    alpha = float((2 * depth) ** 0.25)
    tp = batch * seq
    ts = dec_batch * dec_seq

    xp = x_prompt.reshape(tp, d)
    xs = x_sample.reshape(ts, d)
    row2 = lambda a: a.reshape(1, -1)
    heads = lambda a, n, s: a.reshape(n, s, N_HEADS, HEAD_DIM)
    to_slab = lambda a: jnp.transpose(a, (0, 1, 3, 4, 2)).reshape(a.shape[0], a.shape[1], WIDTH, a.shape[2])
    caches = [to_slab(c) for c in (cache_sb_k, cache_sb_v, cache_band_k, cache_band_v)]
    from_slab = lambda a: jnp.transpose(a.reshape(a.shape[0], N_HEADS, HEAD_DIM, a.shape[2]), (0, 3, 1, 2))
    outs = [[] for _ in range(8)]
    for l in range(depth):
        w_in_bf = w_in[l].astype(BF16)
        woa = w_out[l, :WIDTH].astype(BF16)
        wob = w_out[l, WIDTH:].astype(BF16)
        wq = peer_query[l].astype(BF16)
        sk = peer_subkeys[l].astype(BF16)
        u_bf = peer_u[l].astype(BF16)
        vt_bf = peer_v[l].reshape(-1, PEER_EXPERT_BLOCK, d).transpose(0, 2, 1).astype(BF16)
        ga, gb = row2(gn_a[l]), row2(gn_b[l])
        g1, b1, g2, b2 = row2(ln1_g[l]), row2(ln1_b[l]), row2(ln2_g[l]), row2(ln2_b[l])

        qa, kab, vab, qb, kbb, vbb, sbk, sbv, bdk, bdv = _project_prompt(xp, w_in_bf, batch, seq)
        ca = _sb_prompt(qa, kab, vab, ga, batch, seq)
        cb = _band_prompt(qb, kbb, vbb, _band_prompt_bias(rel_bias[l]), gb, batch, seq)
        x1, x1t, st = _merge(ca, cb, xp, woa, wob, g1, b1, wq, sk, alpha)
        tt = min(PEER_TOKEN_TILE, tp)
        kk, ea, cw, ebw = _peer_gates(st, tt)
        xp = _peer(cw, ebw, kk, ea, x1t, u_bf, vt_bf, x1, g2, b2, alpha, tt)
        outs[0].append(from_slab(sbk))
        outs[1].append(from_slab(sbv))
        outs[2].append(from_slab(bdk))
        outs[3].append(from_slab(bdv))

        qa, ka, va, kab, vab, qb, kb, vb, kbb, vbb = _project(xs, w_in_bf)
        biasp, biasn = _sample_bias(rel_bias[l], dec_seq, nb)
        ca, cb, nbk, nbv = _sample_attn(
            qa, kab, vab, caches[0], caches[1], qb, kbb, vbb, kb, vb, caches[2], caches[3], biasp, biasn, ga, gb, l)
        x1, x1t, st = _merge(ca, cb, xs, woa, wob, g1, b1, wq, sk, alpha)
        tt = min(PEER_TOKEN_TILE, ts)
        kk, ea, cw, ebw = _peer_gates(st, tt)
        xs = _peer(cw, ebw, kk, ea, x1t, u_bf, vt_bf, x1, g2, b2, alpha, tt)
        outs[4].append(heads(ka, dec_batch, dec_seq))
        outs[5].append(heads(va, dec_batch, dec_seq))
        outs[6].append(from_slab(nbk))
        outs[7].append(from_slab(nbv))

    return (xp.reshape(batch, seq, d), xs.reshape(dec_batch, dec_seq, d)) + tuple(jnp.stack(o) for o in outs)
```

```python
import functools

import jax
import jax.numpy as jnp
import numpy as np
from jax import lax
from jax.experimental import pallas as pl
from jax.experimental.pallas import tpu as pltpu

F32 = jnp.float32
BF16 = jnp.bfloat16
U32 = jnp.uint32

HEAD_DIM = 64
N_HEADS = 8
WIDTH = N_HEADS * HEAD_DIM
CHUNK = 64
BAND_CHUNKS = 8
BAND_PAST = BAND_CHUNKS * CHUNK
MAX_REL = 128
N_KEYS = 128
PEER_HEADS = 8
PEER_TOPK = 16
NORM_EPS = 1e-5
NEG_INF = -1e30

LANES = 128
SUBLANES = 8
MXU_DEPTH = 256
VMEM_LIMIT_BYTES = 56 * 1024 * 1024

SB_DEAD_LOG = -104.0
SB_BLOCK = 128
BAND_QBLOCK = 256
BAND_HEAD_GROUP = 4
ROW_TILE = 512
PEER_TOKEN_TILE = 1024
PEER_GATE_TILE = 1024
PEER_EXPERT_BLOCK = 512
PEER_PIPE_FILL = 2
PEER_LANE_TILE = 256
PAD_SCORE = -3.0e38


def _params(*sem):
    return pltpu.CompilerParams(dimension_semantics=sem, vmem_limit_bytes=VMEM_LIMIT_BYTES)


def _nt_dot(a, b):
    return lax.dot_general(a, b, (((1,), (1,)), ((), ())), preferred_element_type=F32)


def _dot(a, b):
    return jnp.dot(a, b, preferred_element_type=F32)


def _layer_norm(y, g, b):
    mu = jnp.mean(y, axis=-1, keepdims=True)
    d = y - mu
    var = jnp.mean(d * d, axis=-1, keepdims=True)
    return d * lax.rsqrt(var + NORM_EPS) * g + b


def _head_rms(o, gain):
    ms = jnp.mean(o * o, axis=-1, keepdims=True)
    return o * lax.rsqrt(ms + NORM_EPS) * gain


def _proj_kernel(x_ref, w_ref, qa_ref, ka_ref, va_ref, kab_ref, vab_ref,
                 qb_ref, kb_ref, vb_ref, kbb_ref, vbb_ref):
    xb = x_ref[...].astype(BF16)
    scale = HEAD_DIM ** -0.5

    def group(g):
        return _dot(xb, w_ref[:, g * WIDTH:(g + 1) * WIDTH])

    qa_ref[...] = (group(0) * scale).astype(BF16)
    k = group(1)
    ka_ref[...] = k
    kab_ref[...] = k.astype(BF16)
    v = group(2)
    va_ref[...] = v
    vab_ref[...] = v.astype(BF16)
    qb_ref[...] = (group(3) * scale).astype(BF16)
    k = group(4)
    kb_ref[...] = k
    kbb_ref[...] = k.astype(BF16)
    v = group(5)
    vb_ref[...] = v
    vbb_ref[...] = v.astype(BF16)


def _project(x, w_bf):
    t, d = x.shape
    tm = min(ROW_TILE, t)
    assert t % tm == 0
    row = lambda i: (i, 0)
    f32o = jax.ShapeDtypeStruct((t, WIDTH), F32)
    bf16o = jax.ShapeDtypeStruct((t, WIDTH), BF16)
    blk = pl.BlockSpec((tm, WIDTH), row)
    return pl.pallas_call(
        _proj_kernel,
        grid=(t // tm,),
        in_specs=[pl.BlockSpec((tm, d), row), pl.BlockSpec(w_bf.shape, lambda i: (0, 0))],
        out_specs=[blk] * 10,
        out_shape=[bf16o, f32o, f32o, bf16o, bf16o, bf16o, f32o, f32o, bf16o, bf16o],
        compiler_params=_params("arbitrary"),
        name="proj",
    )(x, w_bf)


def _proj_prompt_kernel(x_ref, w_ref, qa_ref, kab_ref, vab_ref, qb_ref, kbb_ref, vbb_ref,
                        sbk_ref, sbv_ref, bdk_ref, bdv_ref, *, tiles_per_stream):
    j = pl.program_id(0) % tiles_per_stream
    xb = x_ref[...].astype(BF16)
    scale = HEAD_DIM ** -0.5

    def group(g):
        return _dot(xb, w_ref[:, g * WIDTH:(g + 1) * WIDTH])

    qa_ref[...] = (group(0) * scale).astype(BF16)
    k = group(1)
    kab_ref[...] = k.astype(BF16)
    sbk_ref[0] = k.T
    v = group(2)
    vab_ref[...] = v.astype(BF16)
    sbv_ref[0] = v.T
    qb_ref[...] = (group(3) * scale).astype(BF16)
    kb = group(4)
    kbb_ref[...] = kb.astype(BF16)
    vb = group(5)
    vbb_ref[...] = vb.astype(BF16)

    @pl.when(j == tiles_per_stream - 1)
    def _band_rows():
        bdk_ref[0] = kb.T
        bdv_ref[0] = vb.T


def _project_prompt(x, w_bf, batch, seq):
    t, d = x.shape
    tm = ROW_TILE
    assert seq % tm == 0 and tm == BAND_PAST and t == batch * seq
    nj = seq // tm
    row = lambda i: (i, 0)
    bf16o = jax.ShapeDtypeStruct((t, WIDTH), BF16)
    blk = pl.BlockSpec((tm, WIDTH), row)
    return pl.pallas_call(
        functools.partial(_proj_prompt_kernel, tiles_per_stream=nj),
        grid=(t // tm,),
        in_specs=[pl.BlockSpec((tm, d), row), pl.BlockSpec(w_bf.shape, lambda i: (0, 0))],
        out_specs=[blk] * 6 + [pl.BlockSpec((1, WIDTH, tm), lambda i: (i // nj, 0, i % nj))] * 2
                  + [pl.BlockSpec((1, WIDTH, tm), lambda i: (i // nj, 0, 0))] * 2,
        out_shape=[bf16o] * 6 + [jax.ShapeDtypeStruct((batch, WIDTH, seq), F32)] * 2
                  + [jax.ShapeDtypeStruct((batch, WIDTH, tm), F32)] * 2,
        compiler_params=_params("arbitrary"),
        name="proj_prompt",
    )(x, w_bf)


def _suffix_matrix(kb):
    kp = np.arange(kb)
    m = (kp[:, None] > kp[None, :]).astype(np.float32)
    one = np.concatenate([np.ones((kb, LANES), np.float32), m], axis=1)
    return jnp.asarray(np.concatenate([one, one], axis=0), dtype=BF16)


def _sb_log_terms(z, mask):
    sp = jnp.maximum(z, 0.0) + jnp.log(1.0 + jnp.exp(-jnp.abs(z)))
    log_keep = -sp
    if mask is not None:
        log_keep = jnp.where(mask, log_keep, 0.0)
    hi = log_keep.astype(BF16)
    lo = (log_keep - hi.astype(F32)).astype(BF16)
    return z - sp, jnp.concatenate([hi, lo], axis=1)


def _sb_weights(log_sig, sums, carry, mask):
    kb = log_sig.shape[1]
    log_w = log_sig + sums[:, LANES:LANES + kb]
    if carry is not None:
        log_w = log_w + carry
    w = jnp.exp(log_w)
    if mask is not None:
        w = jnp.where(mask, w, 0.0)
    return w, sums[:, :LANES]


def _sb_prompt_kernel(q_ref, k_ref, v_ref, g_ref, mm_ref, o_ref, carry_s, acc_s):
    i = pl.program_id(1)
    r = q_ref.shape[0]
    row = lax.broadcasted_iota(jnp.int32, (r, r), 0)
    col = lax.broadcasted_iota(jnp.int32, (r, r), 1)
    diag = col < row
    heads = [slice(h * HEAD_DIM, (h + 1) * HEAD_DIM) for h in range(N_HEADS)]

    def key_block(j, first):
        off = pl.multiple_of(j * r, r)
        q = q_ref[...]
        kj = k_ref[pl.ds(off, r), :]
        vj = v_ref[pl.ds(off, r), :]
        mm = mm_ref[...]
        mask = diag if first else None
        zs = [_nt_dot(q[:, hs], kj[:, hs]) for hs in heads]
        terms = [_sb_log_terms(z, mask) for z in zs]
        sums = [_dot(halves, mm) for _, halves in terms]
        ws = [_sb_weights(terms[h][0], sums[h], None if first else carry_s[h], mask) for h in range(N_HEADS)]
        pvs = [_dot(ws[h][0].astype(BF16), vj[:, hs]) for h, hs in enumerate(heads)]
        alive = None
        for h in range(N_HEADS):
            carry = ws[h][1] if first else carry_s[h] + ws[h][1]
            carry_s[h] = carry
            acc_s[h] = pvs[h] if first else acc_s[h] + pvs[h]
            alive = carry if alive is None else jnp.maximum(alive, carry)
        return jnp.max(alive)

    def cond(s):
        j, cmax = s
        return jnp.logical_and(j >= 0, cmax > SB_DEAD_LOG)

    def body(s):
        j, _ = s
        return j - 1, key_block(j, False)

    lax.while_loop(cond, body, (i - 1, key_block(i, True)))
    outs = [_head_rms(acc_s[h], g_ref[:, hs]) for h, hs in enumerate(heads)]
    o_ref[...] = jnp.concatenate(outs, axis=1).astype(BF16)


def _sb_prompt(q, k, v, gain, batch, seq):
    r = SB_BLOCK
    assert seq % r == 0
    nq = seq // r
    return pl.pallas_call(
        _sb_prompt_kernel,
        grid=(batch, nq),
        in_specs=[
            pl.BlockSpec((r, WIDTH), lambda b, i: (b * nq + i, 0)),
            pl.BlockSpec((seq, WIDTH), lambda b, i: (b, 0)),
            pl.BlockSpec((seq, WIDTH), lambda b, i: (b, 0)),
            pl.BlockSpec((1, WIDTH), lambda b, i: (0, 0)),
            pl.BlockSpec((2 * r, LANES + r), lambda b, i: (0, 0)),
        ],
        out_specs=pl.BlockSpec((r, WIDTH), lambda b, i: (b * nq + i, 0)),
        out_shape=jax.ShapeDtypeStruct((batch * seq, WIDTH), BF16),
        scratch_shapes=[pltpu.VMEM((N_HEADS, r, LANES), F32),
                        pltpu.VMEM((N_HEADS, r, HEAD_DIM), F32)],
        compiler_params=_params("arbitrary", "arbitrary"),
        name="sb_prompt",
    )(q, k, v, gain, _suffix_matrix(r))


def _band_prompt_kernel(q_ref, k0_ref, k1_ref, k2_ref, v0_ref, v1_ref, v2_ref, bias_ref, g_ref, o_ref):
    i = pl.program_id(1)
    k_refs = (k0_ref, k1_ref, k2_ref)
    v_refs = (v0_ref, v1_ref, v2_ref)
    heads = [slice(h * HEAD_DIM, (h + 1) * HEAD_DIM) for h in range(N_HEADS)]
    q = q_ref[...]
    ks = [k_ref[...] for k_ref in k_refs]
    vs = [v_ref[...] for v_ref in v_refs]
    outs = []
    for g0 in range(0, N_HEADS, BAND_HEAD_GROUP):
        group = range(g0, g0 + BAND_HEAD_GROUP)
        scores = []
        for h in group:
            hs = heads[h]
            scs = []
            for w in range(3):
                sc = _nt_dot(q[:, hs], ks[w][:, hs]) + bias_ref[h, w]
                if w < 2:
                    sc = jnp.where(i >= 2 - w, sc, NEG_INF)
                scs.append(sc)
            scores.append(scs)
        weights = []
        for scs in scores:
            m = jnp.max(jnp.maximum(jnp.maximum(scs[0], scs[1]), scs[2]), axis=-1, keepdims=True)
            es = [jnp.exp(sc - m) for sc in scs]
            den = jnp.sum(es[0] + es[1] + es[2], axis=-1, keepdims=True)
            weights.append(([e.astype(BF16) for e in es], den))
        for (es, den), h in zip(weights, group):
            hs = heads[h]
            acc = _dot(es[0], vs[0][:, hs]) + _dot(es[1], vs[1][:, hs]) + _dot(es[2], vs[2][:, hs])
            outs.append(_head_rms(acc / den, g_ref[:, hs]))
    o_ref[...] = jnp.concatenate(outs, axis=1).astype(BF16)


def _toeplitz_bias(rel_bias, n, ncols, offset):
    period = n + ncols + 1
    m = jnp.arange(period)
    shift = jnp.where(m < ncols, m, m - period)
    vec = rel_bias[:, jnp.clip(offset - shift, -MAX_REL, MAX_REL) + MAX_REL].astype(F32)
    rows = jnp.tile(vec, (1, n))[:, :n * (period - 1)].reshape(rel_bias.shape[0], n, period - 1)
    return rows[:, :, :ncols]


def _band_prompt_bias(rel_bias):
    qb = BAND_QBLOCK
    bias = _toeplitz_bias(rel_bias, qb, 3 * qb, 2 * qb)
    kc = jnp.arange(3 * qb)[None, :] // CHUNK - (2 * qb // CHUNK)
    qc = jnp.arange(qb)[:, None] // CHUNK
    valid = (kc <= qc) & (kc >= qc - BAND_CHUNKS)
    bias = jnp.where(valid[None], bias, NEG_INF)
    return bias.reshape(N_HEADS, qb, 3, qb).transpose(0, 2, 1, 3)


def _band_prompt(q, k, v, bias, gain, batch, seq):
    qb = BAND_QBLOCK
    assert seq % qb == 0 and 2 * qb == BAND_PAST and qb % CHUNK == 0
    nq = seq // qb
    kspec = lambda back: pl.BlockSpec((qb, WIDTH), lambda b, i: (b * nq + jnp.maximum(i - back, 0), 0))
    return pl.pallas_call(
        _band_prompt_kernel,
        grid=(batch, nq),
        in_specs=[
            pl.BlockSpec((qb, WIDTH), lambda b, i: (b * nq + i, 0)),
            kspec(2), kspec(1), kspec(0), kspec(2), kspec(1), kspec(0),
            pl.BlockSpec(bias.shape, lambda b, i: (0, 0, 0, 0)),
            pl.BlockSpec((1, WIDTH), lambda b, i: (0, 0)),
        ],
        out_specs=pl.BlockSpec((qb, WIDTH), lambda b, i: (b * nq + i, 0)),
        out_shape=jax.ShapeDtypeStruct((batch * seq, WIDTH), BF16),
        compiler_params=_params("arbitrary", "arbitrary"),
        name="band_prompt",
    )(q, k, k, k, v, v, v, bias, gain)


def _sample_attn_kernel(qa_ref, kan_ref, van_ref, cak_ref, cav_ref,
                        qb_ref, kbn_ref, vbn_ref, kbf_ref, vbf_ref, cbk_ref, cbv_ref,
                        biasp_ref, biasn_ref, ga_ref, gb_ref, mmn_ref, mmp_ref,
                        oa_ref, ob_ref, nbk_ref, nbv_ref):
    sd = qa_ref.shape[0]
    past = cak_ref.shape[2]
    nb = cbk_ref.shape[2]
    row = lax.broadcasted_iota(jnp.int32, (sd, sd), 0)
    col = lax.broadcasted_iota(jnp.int32, (sd, sd), 1)
    diag = col < row
    cak = cak_ref[0].astype(BF16)
    cav = cav_ref[0].astype(BF16)
    cbk = cbk_ref[0].astype(BF16)
    cbv = cbv_ref[0].astype(BF16)
    mmn = mmn_ref[...]
    mmp = mmp_ref[...]
    heads = [slice(h * HEAD_DIM, (h + 1) * HEAD_DIM) for h in range(N_HEADS)]
    blocks = [slice(j * SB_BLOCK, (j + 1) * SB_BLOCK) for j in range(past // SB_BLOCK)]
    qa = qa_ref[...]
    z_new = [_nt_dot(qa[:, hs], kan_ref[:, hs]) for hs in heads]
    z_past = [_dot(qa[:, hs], cak[hs, :]) for hs in heads]
    t_new = [_sb_log_terms(z, diag) for z in z_new]
    t_past = [[_sb_log_terms(z[:, ks], None) for ks in blocks] for z in z_past]
    s_new = [_dot(halves, mmn) for _, halves in t_new]
    s_past = [[_dot(halves, mmp) for _, halves in t] for t in t_past]
    outs_a = []
    for h, hs in enumerate(heads):
        w_new, carry = _sb_weights(t_new[h][0], s_new[h], None, diag)
        w_past = [None] * len(blocks)
        for j in reversed(range(len(blocks))):
            w_past[j], total = _sb_weights(t_past[h][j][0], s_past[h][j], carry, None)
            carry = carry + total
        acc = _dot(w_new.astype(BF16), van_ref[:, hs]) + _nt_dot(jnp.concatenate(w_past, axis=1).astype(BF16), cav[hs, :])
        outs_a.append(_head_rms(acc, ga_ref[:, hs]))
    qb = qb_ref[...]
    sc_p = [_dot(qb[:, hs], cbk[hs, :]) + biasp_ref[h] for h, hs in enumerate(heads)]
    sc_n = [_nt_dot(qb[:, hs], kbn_ref[:, hs]) + biasn_ref[h] for h, hs in enumerate(heads)]
    outs_b = []
    for h, hs in enumerate(heads):
        m = jnp.maximum(jnp.max(sc_p[h], axis=-1, keepdims=True), jnp.max(sc_n[h], axis=-1, keepdims=True))
        ep = jnp.exp(sc_p[h] - m)
        en = jnp.exp(sc_n[h] - m)
        den = jnp.sum(ep, axis=-1, keepdims=True) + jnp.sum(en, axis=-1, keepdims=True)
        acc = _nt_dot(ep.astype(BF16), cbv[hs, :]) + _dot(en.astype(BF16), vbn_ref[:, hs])
        outs_b.append(_head_rms(acc / den, gb_ref[:, hs]))
    oa_ref[...] = jnp.concatenate(outs_a, axis=1).astype(BF16)
    ob_ref[...] = jnp.concatenate(outs_b, axis=1).astype(BF16)
    nbk_ref[0, :, :nb - sd] = cbk_ref[0, :, sd:]
    nbk_ref[0, :, nb - sd:] = kbf_ref[...].T
    nbv_ref[0, :, :nb - sd] = cbv_ref[0, :, sd:]
    nbv_ref[0, :, nb - sd:] = vbf_ref[...].T


def _sample_bias(rel_bias, sd, nb):
    bias = _toeplitz_bias(rel_bias, sd, nb + sd, nb)
    return bias[:, :, :nb], bias[:, :, nb:]


def _sample_attn(qa, kan, van, cak, cav, qb, kbn, vbn, kbf, vbf, cbk, cbv, biasp, biasn, ga, gb, layer):
    _, nbatch, _, past = cak.shape
    nb = cbk.shape[3]
    sd = qa.shape[0] // nbatch
    assert past % SB_BLOCK == 0 and sd % 8 == 0 and sd <= nb and sd <= LANES
    rows = pl.BlockSpec((sd, WIDTH), lambda b: (b, 0))
    cache = lambda n: pl.BlockSpec((None, 1, WIDTH, n), lambda b: (layer, b, 0, 0))
    rolled = pl.BlockSpec((1, WIDTH, nb), lambda b: (b, 0, 0))
    full = lambda a: pl.BlockSpec(a.shape, lambda b: (0,) * a.ndim)
    mmn = _suffix_matrix(sd)
    mmp = _suffix_matrix(SB_BLOCK)
    return pl.pallas_call(
        _sample_attn_kernel,
        grid=(nbatch,),
        in_specs=[rows, rows, rows, cache(past), cache(past),
                  rows, rows, rows, rows, rows, cache(nb), cache(nb),
                  full(biasp), full(biasn), full(ga), full(gb), full(mmn), full(mmp)],
        out_specs=[rows, rows, rolled, rolled],
        out_shape=[jax.ShapeDtypeStruct((nbatch * sd, WIDTH), BF16)] * 2
                  + [jax.ShapeDtypeStruct((nbatch, WIDTH, nb), F32)] * 2,
        compiler_params=_params("arbitrary"),
        name="sample_attn",
    )(qa, kan, van, cak, cav, qb, kbn, vbn, kbf, vbf, cbk, cbv, biasp, biasn, ga, gb, mmn, mmp)


def _merge_kernel(ca_ref, cb_ref, x_ref, woa_ref, wob_ref, g_ref, b_ref, wq_ref, sk_ref,
                  x1_ref, x1t_ref, st_ref, *, alpha):
    mix = _dot(ca_ref[...], woa_ref[...]) + _dot(cb_ref[...], wob_ref[...])
    x1 = _layer_norm(alpha * x_ref[...] + mix, g_ref[...], b_ref[...])
    x1_ref[...] = x1
    x1t_ref[...] = x1.T.astype(BF16)
    qp = _dot(x1.astype(BF16), wq_ref[...])
    half = sk_ref.shape[2]
    for h in range(PEER_HEADS):
        for j in range(2):
            c0 = (2 * h + j) * half
            qh = qp[:, c0:c0 + half].astype(BF16)
            st_ref[(2 * h + j) * N_KEYS:(2 * h + j + 1) * N_KEYS, :] = _nt_dot(sk_ref[j], qh)


def _merge(ca, cb, x, woa, wob, g, b, wq, sk, alpha):
    t, d = x.shape
    tm = min(ROW_TILE, t)
    assert t % tm == 0
    nscore = PEER_HEADS * 2 * N_KEYS
    row = lambda i: (i, 0)
    colb = lambda i: (0, i)
    full = lambda a: pl.BlockSpec(a.shape, lambda i: (0,) * a.ndim)
    return pl.pallas_call(
        functools.partial(_merge_kernel, alpha=alpha),
        grid=(t // tm,),
        in_specs=[pl.BlockSpec((tm, WIDTH), row), pl.BlockSpec((tm, WIDTH), row), pl.BlockSpec((tm, d), row),
                  full(woa), full(wob), full(g), full(b), full(wq), full(sk)],
        out_specs=[pl.BlockSpec((tm, d), row), pl.BlockSpec((d, tm), colb), pl.BlockSpec((nscore, tm), colb)],
        out_shape=[jax.ShapeDtypeStruct((t, d), F32), jax.ShapeDtypeStruct((d, t), BF16),
                   jax.ShapeDtypeStruct((nscore, t), F32)],
        compiler_params=_params("arbitrary"),
        name="merge",
    )(ca, cb, x, woa, wob, g, b, wq, sk)


def _cmp_exchange(v, i, j):
    a, b = v[i], v[j]
    v[i] = jnp.maximum(a, b)
    v[j] = jnp.minimum(a, b)


def _sort16_desc(v):
    v = list(v)
    n = len(v)
    k = 2
    while k <= n:
        j = k // 2
        while j >= 1:
            for i in range(n):
                l = i ^ j
                if l > i:
                    if (i & k) == 0:
                        _cmp_exchange(v, i, l)
                    else:
                        _cmp_exchange(v, l, i)
            j //= 2
        k *= 2
    return v


def _merge_top16(a, b):
    n = len(a)
    top = [jnp.maximum(a[i], b[n - 1 - i]) for i in range(n)]
    out = [jnp.minimum(a[i], b[n - 1 - i]) for i in range(n)]
    while len(out) > 1:
        out = [jnp.maximum(out[2 * i], out[2 * i + 1]) for i in range(len(out) // 2)]
    j = n // 2
    while j >= 1:
        for i in range(n):
            if (i & j) == 0:
                _cmp_exchange(top, i, i + j)
        j //= 2
    return top, out[0]


def _top16_and_next(vals):
    groups = [_sort16_desc(vals[g:g + PEER_TOPK]) for g in range(0, len(vals), PEER_TOPK)]
    nxt = None
    while len(groups) > 1:
        merged = []
        for g in range(0, len(groups), 2):
            top, left = _merge_top16(groups[g], groups[g + 1])
            merged.append(top)
            nxt = left if nxt is None else jnp.maximum(nxt, left)
        groups = merged
    return groups[0], nxt


def _gelu_tanh(x):
    return 0.5 * x * (1.0 + jnp.tanh(0.7978845608028654 * (x + 0.044715 * (x * x * x))))


def _bf16_bits(x):
    return pltpu.bitcast(x.astype(BF16).astype(F32), U32)


def _pack_row_pairs(x):
    groups = range(0, x.shape[0], 2 * SUBLANES)
    hi = jnp.concatenate([x[g:g + SUBLANES] for g in groups], axis=0)
    lo = jnp.concatenate([x[g + SUBLANES:g + 2 * SUBLANES] for g in groups], axis=0)
    return _bf16_bits(hi) | (_bf16_bits(lo) >> 16)


def _both_halves(x):
    b = _bf16_bits(x)
    return b | (b >> 16)


def _peer_gate_kernel(st_ref, kk_ref, ea_ref, cw_ref, ebw_ref, row_s, *, peer_tile):
    tt = st_ref.shape[1]
    ng = tt // LANES
    k1 = PEER_TOPK + 1

    def head(h, carry):
        base = pl.multiple_of(h * 2 * N_KEYS, 2 * N_KEYS)
        s1 = st_ref[pl.ds(base, N_KEYS), :]
        s2 = st_ref[pl.ds(base + N_KEYS, N_KEYS), :]
        s1r = s1.reshape(N_KEYS, ng, LANES)
        s2r = s2.reshape(N_KEYS, ng, LANES)
        top_a, next_a = _top16_and_next([s1r[a] for a in range(N_KEYS)])
        top_b, next_b = _top16_and_next([s2r[a] for a in range(N_KEYS)])
        la = top_a + [next_a]
        lb = top_b + [next_b]
        cands = [la[i - 1] + lb[j - 1] for i in range(1, k1 + 1) for j in range(1, k1 + 1) if i * j <= k1]
        pad = jnp.full_like(cands[0], PAD_SCORE)
        cands = cands + [pad] * (-len(cands) % PEER_TOPK)
        top_c, next_c = _top16_and_next(cands)
        tau = 0.5 * (top_c[PEER_TOPK - 1] + next_c)
        den = jnp.ones_like(tau)
        for cv in top_c[1:]:
            den = den + jnp.exp(cv - top_c[0])
        vals = [tau, la[0], 1.0 / den] + lb
        for r, val in enumerate(vals):
            for g in range(ng):
                row_s[r:r + 1, g * LANES:(g + 1) * LANES] = val[g:g + 1, :]
        tau_r = row_s[0:1, :]
        m1_r = row_s[1:2, :]
        iz_r = row_s[2:3, :]
        lb_r = [row_s[3 + j:4 + j, :] for j in range(k1)]
        code = jnp.zeros_like(s2)
        for j in reversed(range(k1)):
            code = jnp.where(s2 >= lb_r[j], float(k1 - j), code)
        th = tau_r - s1
        n = jnp.zeros_like(s1)
        for j in range(PEER_TOPK):
            n = jnp.where(lb_r[j] >= th, float(j + 1), n)
        kk_ref[:, pl.ds(h, 1), :] = _both_halves(float(k1 + 1) - n)[:, None, :]
        ea_ref[:, pl.ds(h, 1), :] = _both_halves(jnp.exp(s1 - m1_r) * iz_r)[:, None, :]
        rows = pl.ds(pl.multiple_of(h * (N_KEYS // 2), N_KEYS // 2), N_KEYS // 2)
        cw_ref[rows, :] = _pack_row_pairs(code)
        ebw = _pack_row_pairs(jnp.exp(s2 - lb_r[0]))
        for p0 in range(0, tt, peer_tile):
            p1 = p0 + peer_tile
            ebw_ref[rows, p0 + LANES:p1] = ebw[:, p0:p1 - LANES]
            ebw_ref[rows, p0:p0 + LANES] = ebw[:, p1 - LANES:p1]
        return carry

    lax.fori_loop(0, PEER_HEADS, head, 0)


def _peer_gates(st, peer_tile):
    nscore, t = st.shape
    tt = PEER_GATE_TILE if t % PEER_GATE_TILE == 0 else peer_tile
    assert t % tt == 0 and tt % peer_tile == 0 and peer_tile > LANES and peer_tile % LANES == 0
    rows = jax.ShapeDtypeStruct((N_KEYS, PEER_HEADS, t), U32)
    packed = jax.ShapeDtypeStruct((PEER_HEADS * N_KEYS // 2, t), U32)
    return pl.pallas_call(
        functools.partial(_peer_gate_kernel, peer_tile=peer_tile),
        grid=(t // tt,),
        in_specs=[pl.BlockSpec((nscore, tt), lambda i: (0, i))],
        out_specs=[pl.BlockSpec((N_KEYS, PEER_HEADS, tt), lambda i: (0, 0, i)),
                   pl.BlockSpec((N_KEYS, PEER_HEADS, tt), lambda i: (0, 0, i)),
                   pl.BlockSpec((PEER_HEADS * N_KEYS // 2, tt), lambda i: (0, i)),
                   pl.BlockSpec((PEER_HEADS * N_KEYS // 2, tt), lambda i: (0, i))],
        out_shape=[rows, rows, packed, packed],
        scratch_shapes=[pltpu.VMEM((3 * SUBLANES, tt), F32)],
        compiler_params=_params("arbitrary"),
        name="peer_gates",
    )(st)


def _peer_kernel(cw_ref, ebw_ref, kk_ref, ea_ref, xt_ref, u_ref, vt_ref, x1_ref, g_ref, b_ref, o_ref,
                 ht0_s, ht1_s, wt0_s, wt1_s, acc_s, *, alpha, nblk):
    c = pl.program_id(0)
    tt = xt_ref.shape[1]
    ch = u_ref.shape[0]

    @pl.when(c == 0)
    def _clear():
        ht1_s[...] = jnp.zeros(ht1_s.shape, F32)
        wt0_s[...] = jnp.zeros(wt0_s.shape, BF16)
        wt1_s[...] = jnp.zeros(wt1_s.shape, BF16)
        acc_s[...] = jnp.zeros(acc_s.shape, F32)

    ablocks = ch // N_KEYS
    lt = PEER_LANE_TILE

    def stages(ht_new, ht_old, wt_new, wt_old):
        d = vt_ref.shape[0]
        npiece = ablocks // 2
        nlv = lt // LANES
        nr2 = N_KEYS // (2 * SUBLANES)
        assert nlv == 2
        for ts in range(tt // lt):
            ls = slice(ts * lt, (ts + 1) * lt)
            for unit in range(npiece * nlv):
                ap, tv = unit // nlv, unit % nlv
                piece = unit // 2
                if unit % 2 == 0:
                    rows = slice(piece * (ch // npiece), (piece + 1) * (ch // npiece))
                    lhs_ref, rhs_ref, kdim = u_ref, xt_ref, u_ref.shape[1]
                else:
                    rows = slice(piece * (d // npiece), (piece + 1) * (d // npiece))
                    lhs_ref, rhs_ref, kdim = vt_ref, wt_old, ch
                nk = kdim // MXU_DEPTH
                part = None
                g = ts * nlv + tv
                lv = slice(g * LANES, (g + 1) * LANES)
                ge = (g + 1) % (tt // LANES)
                le = slice(ge * LANES, (ge + 1) * LANES)
                als = (2 * ap, 2 * ap + 1)
                spread = lambda ref, al, h: pltpu.bitcast(jnp.broadcast_to(ref[al, h:h + 1, lv], (SUBLANES, LANES)), BF16)
                kk = [[spread(kk_ref, al, h) for h in range(PEER_HEADS)] for al in als]
                ea = [[spread(ea_ref, al, h) for h in range(PEER_HEADS)] for al in als]
                for r2 in range(nr2):
                    if r2 % (nr2 // nk) == 0:
                        kb = slice(r2 // (nr2 // nk) * MXU_DEPTH, (r2 // (nr2 // nk) + 1) * MXU_DEPTH)
                        dk = _dot(lhs_ref[rows, kb], rhs_ref[kb, ls])
                        part = dk if part is None else part + dk
                    gates = [jnp.zeros((2 * SUBLANES, LANES), BF16) for _ in als]
                    for h in range(PEER_HEADS):
                        wrows = slice(h * (N_KEYS // 2) + r2 * SUBLANES, h * (N_KEYS // 2) + (r2 + 1) * SUBLANES)
                        code = pltpu.bitcast(cw_ref[wrows, lv], BF16)
                        eb = pltpu.bitcast(ebw_ref[wrows, le], BF16)
                        for i in range(2):
                            gates[i] = gates[i] + jnp.where(code >= kk[i][h], eb, jnp.zeros_like(eb)) * ea[i][h]
                    for i, al in enumerate(als):
                        rs = slice(al * N_KEYS + r2 * 2 * SUBLANES, al * N_KEYS + (r2 + 1) * 2 * SUBLANES)
                        gw = pltpu.bitcast(gates[i], U32)
                        gate = jnp.concatenate([pltpu.bitcast(gw & jnp.uint32(0xFFFF0000), F32),
                                                pltpu.bitcast(gw << 16, F32)], axis=0)
                        wt_new[rs, lv] = (gate * _gelu_tanh(ht_old[rs, lv])).astype(BF16)
                if unit % 2 == 0:
                    ht_new[rows, ls] = part
                else:
                    acc_s[rows, ls] += part

    @pl.when(c % 2 == 0)
    def _even():
        stages(ht0_s, ht1_s, wt1_s, wt0_s)

    @pl.when(c % 2 == 1)
    def _odd():
        stages(ht1_s, ht0_s, wt0_s, wt1_s)

    @pl.when(jnp.logical_and(c >= PEER_PIPE_FILL, (c - PEER_PIPE_FILL) % nblk == nblk - 1))
    def _finish():
        y = alpha * x1_ref[...] + acc_s[...].T
        o_ref[...] = _layer_norm(y, g_ref[...], b_ref[...])
        acc_s[...] = jnp.zeros(acc_s.shape, F32)


def _peer(cw, ebw, kk, ea, xt, u_bf, vt_bf, x1, g, b, alpha, tt):
    t, d = x1.shape
    ne = u_bf.shape[0]
    ch = PEER_EXPERT_BLOCK
    assert t % tt == 0 and tt % PEER_LANE_TILE == 0 and ne % ch == 0 and ne == N_KEYS * N_KEYS
    assert vt_bf.shape == (ne // ch, d, ch)
    nblk = ne // ch
    ablocks = ch // N_KEYS
    npairs = (t // tt) * nblk
    tile = lambda c, lag: jnp.clip(c - lag, 0, npairs - 1) // nblk
    block = lambda c, lag: jnp.clip(c - lag, 0, npairs - 1) % nblk
    return pl.pallas_call(
        functools.partial(_peer_kernel, alpha=alpha, nblk=nblk),
        grid=(npairs + PEER_PIPE_FILL,),
        in_specs=[
            pl.BlockSpec((PEER_HEADS * N_KEYS // 2, tt), lambda c: (0, tile(c, 1))),
            pl.BlockSpec((PEER_HEADS * N_KEYS // 2, tt), lambda c: (0, tile(c, 1))),
            pl.BlockSpec((ablocks, PEER_HEADS, tt), lambda c: (block(c, 1), 0, tile(c, 1))),
            pl.BlockSpec((ablocks, PEER_HEADS, tt), lambda c: (block(c, 1), 0, tile(c, 1))),
            pl.BlockSpec((d, tt), lambda c: (0, tile(c, 0))),
            pl.BlockSpec((ch, d), lambda c: (block(c, 0), 0)),
            pl.BlockSpec((None, d, ch), lambda c: (block(c, PEER_PIPE_FILL), 0, 0)),
            pl.BlockSpec((tt, d), lambda c: (tile(c, PEER_PIPE_FILL), 0)),
            pl.BlockSpec((1, d), lambda c: (0, 0)),
            pl.BlockSpec((1, d), lambda c: (0, 0)),
        ],
        out_specs=pl.BlockSpec((tt, d), lambda c: (tile(c, PEER_PIPE_FILL), 0)),
        out_shape=jax.ShapeDtypeStruct((t, d), F32),
        scratch_shapes=[
            pltpu.VMEM((ch, tt), F32),
            pltpu.VMEM((ch, tt), F32),
            pltpu.VMEM((ch, tt), BF16),
            pltpu.VMEM((ch, tt), BF16),
            pltpu.VMEM((d, tt), F32),
        ],
        compiler_params=_params("arbitrary"),
        name="peer",
    )(cw, ebw, kk, ea, xt, u_bf, vt_bf, x1, g, b)


def kernel(x_prompt, x_sample, cache_sb_k, cache_sb_v, cache_band_k, cache_band_v, w_in, w_out, gn_a, gn_b,
           rel_bias, ln1_g, ln1_b, peer_query, peer_subkeys, peer_u, peer_v, ln2_g, ln2_b):
    batch, seq, d = x_prompt.shape
    dec_batch, dec_seq, _ = x_sample.shape
    depth = w_in.shape[0]
    past = cache_sb_k.shape[2]
    nb = cache_band_k.shape[2]
    assert w_in.shape[2] == 6 * WIDTH and w_out.shape[1] == 2 * WIDTH
    assert seq >= BAND_PAST and nb == BAND_PAST
    alpha = float((2 * depth) ** 0.25)
    tp = batch * seq
    ts = dec_batch * dec_seq

    xp = x_prompt.reshape(tp, d)
    xs = x_sample.reshape(ts, d)
    row2 = lambda a: a.reshape(1, -1)
    heads = lambda a, n, s: a.reshape(n, s, N_HEADS, HEAD_DIM)
    to_slab = lambda a: jnp.transpose(a, (0, 1, 3, 4, 2)).reshape(a.shape[0], a.shape[1], WIDTH, a.shape[2])
    caches = [to_slab(c) for c in (cache_sb_k, cache_sb_v, cache_band_k, cache_band_v)]
    from_slab = lambda a: jnp.transpose(a.reshape(a.shape[0], N_HEADS, HEAD_DIM, a.shape[2]), (0, 3, 1, 2))
    outs = [[] for _ in range(8)]
    for l in range(depth):
        w_in_bf = w_in[l].astype(BF16)
        woa = w_out[l, :WIDTH].astype(BF16)
        wob = w_out[l, WIDTH:].astype(BF16)
        wq = peer_query[l].astype(BF16)
        sk = peer_subkeys[l].astype(BF16)
        u_bf = peer_u[l].astype(BF16)
        vt_bf = peer_v[l].reshape(-1, PEER_EXPERT_BLOCK, d).transpose(0, 2, 1).astype(BF16)
        ga, gb = row2(gn_a[l]), row2(gn_b[l])
        g1, b1, g2, b2 = row2(ln1_g[l]), row2(ln1_b[l]), row2(ln2_g[l]), row2(ln2_b[l])

        qa, kab, vab, qb, kbb, vbb, sbk, sbv, bdk, bdv = _project_prompt(xp, w_in_bf, batch, seq)
        ca = _sb_prompt(qa, kab, vab, ga, batch, seq)
        cb = _band_prompt(qb, kbb, vbb, _band_prompt_bias(rel_bias[l]), gb, batch, seq)
        x1, x1t, st = _merge(ca, cb, xp, woa, wob, g1, b1, wq, sk, alpha)
        tt = min(PEER_TOKEN_TILE, tp)
        kk, ea, cw, ebw = _peer_gates(st, tt)
        xp = _peer(cw, ebw, kk, ea, x1t, u_bf, vt_bf, x1, g2, b2, alpha, tt)
        outs[0].append(from_slab(sbk))
        outs[1].append(from_slab(sbv))
        outs[2].append(from_slab(bdk))
        outs[3].append(from_slab(bdv))

        qa, ka, va, kab, vab, qb, kb, vb, kbb, vbb = _project(xs, w_in_bf)
        biasp, biasn = _sample_bias(rel_bias[l], dec_seq, nb)
        ca, cb, nbk, nbv = _sample_attn(
            qa, kab, vab, caches[0], caches[1], qb, kbb, vbb, kb, vb, caches[2], caches[3], biasp, biasn, ga, gb, l)
        x1, x1t, st = _merge(ca, cb, xs, woa, wob, g1, b1, wq, sk, alpha)
        tt = min(PEER_TOKEN_TILE, ts)
        kk, ea, cw, ebw = _peer_gates(st, tt)
        xs = _peer(cw, ebw, kk, ea, x1t, u_bf, vt_bf, x1, g2, b2, alpha, tt)
        outs[4].append(heads(ka, dec_batch, dec_seq))
        outs[5].append(heads(va, dec_batch, dec_seq))
        outs[6].append(from_slab(nbk))
        outs[7].append(from_slab(nbv))

    return (xp.reshape(batch, seq, d), xs.reshape(dec_batch, dec_seq, d)) + tuple(jnp.stack(o) for o in outs)
```

```python
import functools

import jax
import jax.numpy as jnp
import numpy as np
from jax import lax
from jax.experimental import pallas as pl
from jax.experimental.pallas import tpu as pltpu

F32 = jnp.float32
BF16 = jnp.bfloat16
U32 = jnp.uint32

HEAD_DIM = 64
N_HEADS = 8
WIDTH = N_HEADS * HEAD_DIM
CHUNK = 64
BAND_CHUNKS = 8
BAND_PAST = BAND_CHUNKS * CHUNK
MAX_REL = 128
N_KEYS = 128
PEER_HEADS = 8
PEER_TOPK = 16
NORM_EPS = 1e-5
NEG_INF = -1e30

LANES = 128
SUBLANES = 8
MXU_DEPTH = 256
VMEM_LIMIT_BYTES = 56 * 1024 * 1024

SB_DEAD_LOG = -104.0
SB_BLOCK = 128
BAND_QBLOCK = 256
BAND_HEAD_GROUP = 4
ROW_TILE = 512
PEER_TOKEN_TILE = 1024
PEER_GATE_TILE = 1024
PEER_EXPERT_BLOCK = 512
PEER_PIPE_FILL = 2
PEER_LANE_TILE = 256
PAD_SCORE = -3.0e38


def _params(*sem):
    return pltpu.CompilerParams(dimension_semantics=sem, vmem_limit_bytes=VMEM_LIMIT_BYTES)


def _nt_dot(a, b):
    return lax.dot_general(a, b, (((1,), (1,)), ((), ())), preferred_element_type=F32)


def _dot(a, b):
    return jnp.dot(a, b, preferred_element_type=F32)


def _layer_norm(y, g, b):
    mu = jnp.mean(y, axis=-1, keepdims=True)
    d = y - mu
    var = jnp.mean(d * d, axis=-1, keepdims=True)
    return d * lax.rsqrt(var + NORM_EPS) * g + b


def _head_rms(o, gain):
    ms = jnp.mean(o * o, axis=-1, keepdims=True)
    return o * lax.rsqrt(ms + NORM_EPS) * gain


def _proj_kernel(x_ref, w_ref, qa_ref, ka_ref, va_ref, kab_ref, vab_ref,
                 qb_ref, kb_ref, vb_ref, kbb_ref, vbb_ref):
    xb = x_ref[...].astype(BF16)
    scale = HEAD_DIM ** -0.5

    def group(g):
        return _dot(xb, w_ref[:, g * WIDTH:(g + 1) * WIDTH])

    qa_ref[...] = (group(0) * scale).astype(BF16)
    k = group(1)
    ka_ref[...] = k
    kab_ref[...] = k.astype(BF16)
    v = group(2)
    va_ref[...] = v
    vab_ref[...] = v.astype(BF16)
    qb_ref[...] = (group(3) * scale).astype(BF16)
    k = group(4)
    kb_ref[...] = k
    kbb_ref[...] = k.astype(BF16)
    v = group(5)
    vb_ref[...] = v
    vbb_ref[...] = v.astype(BF16)


def _project(x, w_bf):
    t, d = x.shape
    tm = min(ROW_TILE, t)
    assert t % tm == 0
    row = lambda i: (i, 0)
    f32o = jax.ShapeDtypeStruct((t, WIDTH), F32)
    bf16o = jax.ShapeDtypeStruct((t, WIDTH), BF16)
    blk = pl.BlockSpec((tm, WIDTH), row)
    return pl.pallas_call(
        _proj_kernel,
        grid=(t // tm,),
        in_specs=[pl.BlockSpec((tm, d), row), pl.BlockSpec(w_bf.shape, lambda i: (0, 0))],
        out_specs=[blk] * 10,
        out_shape=[bf16o, f32o, f32o, bf16o, bf16o, bf16o, f32o, f32o, bf16o, bf16o],
        compiler_params=_params("arbitrary"),
        name="proj",
    )(x, w_bf)


def _proj_prompt_kernel(x_ref, w_ref, qa_ref, kab_ref, vab_ref, qb_ref, kbb_ref, vbb_ref,
                        sbk_ref, sbv_ref, bdk_ref, bdv_ref, *, tiles_per_stream):
    j = pl.program_id(0) % tiles_per_stream
    xb = x_ref[...].astype(BF16)
    scale = HEAD_DIM ** -0.5

    def group(g):
        return _dot(xb, w_ref[:, g * WIDTH:(g + 1) * WIDTH])

    qa_ref[...] = (group(0) * scale).astype(BF16)
    k = group(1)
    kab_ref[...] = k.astype(BF16)
    sbk_ref[0] = k.T
    v = group(2)
    vab_ref[...] = v.astype(BF16)
    sbv_ref[0] = v.T
    qb_ref[...] = (group(3) * scale).astype(BF16)
    kb = group(4)
    kbb_ref[...] = kb.astype(BF16)
    vb = group(5)
    vbb_ref[...] = vb.astype(BF16)

    @pl.when(j == tiles_per_stream - 1)
    def _band_rows():
        bdk_ref[0] = kb.T
        bdv_ref[0] = vb.T


def _project_prompt(x, w_bf, batch, seq):
    t, d = x.shape
    tm = ROW_TILE
    assert seq % tm == 0 and tm == BAND_PAST and t == batch * seq
    nj = seq // tm
    row = lambda i: (i, 0)
    bf16o = jax.ShapeDtypeStruct((t, WIDTH), BF16)
    blk = pl.BlockSpec((tm, WIDTH), row)
    return pl.pallas_call(
        functools.partial(_proj_prompt_kernel, tiles_per_stream=nj),
        grid=(t // tm,),
        in_specs=[pl.BlockSpec((tm, d), row), pl.BlockSpec(w_bf.shape, lambda i: (0, 0))],
        out_specs=[blk] * 6 + [pl.BlockSpec((1, WIDTH, tm), lambda i: (i // nj, 0, i % nj))] * 2
                  + [pl.BlockSpec((1, WIDTH, tm), lambda i: (i // nj, 0, 0))] * 2,
        out_shape=[bf16o] * 6 + [jax.ShapeDtypeStruct((batch, WIDTH, seq), F32)] * 2
                  + [jax.ShapeDtypeStruct((batch, WIDTH, tm), F32)] * 2,
        compiler_params=_params("arbitrary"),
        name="proj_prompt",
    )(x, w_bf)


def _suffix_matrix(kb):
    kp = np.arange(kb)
    m = (kp[:, None] > kp[None, :]).astype(np.float32)
    one = np.concatenate([np.ones((kb, LANES), np.float32), m], axis=1)
    return jnp.asarray(np.concatenate([one, one], axis=0), dtype=BF16)


def _sb_log_terms(z, mask):
    sp = jnp.maximum(z, 0.0) + jnp.log(1.0 + jnp.exp(-jnp.abs(z)))
    log_keep = -sp
    if mask is not None:
        log_keep = jnp.where(mask, log_keep, 0.0)
    hi = log_keep.astype(BF16)
    lo = (log_keep - hi.astype(F32)).astype(BF16)
    return z - sp, jnp.concatenate([hi, lo], axis=1)


def _sb_weights(log_sig, sums, carry, mask):
    kb = log_sig.shape[1]
    log_w = log_sig + sums[:, LANES:LANES + kb]
    if carry is not None:
        log_w = log_w + carry
    w = jnp.exp(log_w)
    if mask is not None:
        w = jnp.where(mask, w, 0.0)
    return w, sums[:, :LANES]


def _sb_prompt_kernel(q_ref, k_ref, v_ref, g_ref, mm_ref, o_ref, carry_s, acc_s):
    i = pl.program_id(1)
    r = q_ref.shape[0]
    row = lax.broadcasted_iota(jnp.int32, (r, r), 0)
    col = lax.broadcasted_iota(jnp.int32, (r, r), 1)
    diag = col < row
    heads = [slice(h * HEAD_DIM, (h + 1) * HEAD_DIM) for h in range(N_HEADS)]

    def key_block(j, first):
        off = pl.multiple_of(j * r, r)
        q = q_ref[...]
        kj = k_ref[pl.ds(off, r), :]
        vj = v_ref[pl.ds(off, r), :]
        mm = mm_ref[...]
        mask = diag if first else None
        zs = [_nt_dot(q[:, hs], kj[:, hs]) for hs in heads]
        terms = [_sb_log_terms(z, mask) for z in zs]
        sums = [_dot(halves, mm) for _, halves in terms]
        ws = [_sb_weights(terms[h][0], sums[h], None if first else carry_s[h], mask) for h in range(N_HEADS)]
        pvs = [_dot(ws[h][0].astype(BF16), vj[:, hs]) for h, hs in enumerate(heads)]
        alive = None
        for h in range(N_HEADS):
            carry = ws[h][1] if first else carry_s[h] + ws[h][1]
            carry_s[h] = carry
            acc_s[h] = pvs[h] if first else acc_s[h] + pvs[h]
            alive = carry if alive is None else jnp.maximum(alive, carry)
        return jnp.max(alive)

    def cond(s):
        j, cmax = s
        return jnp.logical_and(j >= 0, cmax > SB_DEAD_LOG)

    def body(s):
        j, _ = s
        return j - 1, key_block(j, False)

    lax.while_loop(cond, body, (i - 1, key_block(i, True)))
    outs = [_head_rms(acc_s[h], g_ref[:, hs]) for h, hs in enumerate(heads)]
    o_ref[...] = jnp.concatenate(outs, axis=1).astype(BF16)


def _sb_prompt(q, k, v, gain, batch, seq):
    r = SB_BLOCK
    assert seq % r == 0
    nq = seq // r
    return pl.pallas_call(
        _sb_prompt_kernel,
        grid=(batch, nq),
        in_specs=[
            pl.BlockSpec((r, WIDTH), lambda b, i: (b * nq + i, 0)),
            pl.BlockSpec((seq, WIDTH), lambda b, i: (b, 0)),
            pl.BlockSpec((seq, WIDTH), lambda b, i: (b, 0)),
            pl.BlockSpec((1, WIDTH), lambda b, i: (0, 0)),
            pl.BlockSpec((2 * r, LANES + r), lambda b, i: (0, 0)),
        ],
        out_specs=pl.BlockSpec((r, WIDTH), lambda b, i: (b * nq + i, 0)),
        out_shape=jax.ShapeDtypeStruct((batch * seq, WIDTH), BF16),
        scratch_shapes=[pltpu.VMEM((N_HEADS, r, LANES), F32),
                        pltpu.VMEM((N_HEADS, r, HEAD_DIM), F32)],
        compiler_params=_params("arbitrary", "arbitrary"),
        name="sb_prompt",
    )(q, k, v, gain, _suffix_matrix(r))


def _band_prompt_kernel(q_ref, k0_ref, k1_ref, k2_ref, v0_ref, v1_ref, v2_ref, bias_ref, g_ref, o_ref):
    i = pl.program_id(1)
    k_refs = (k0_ref, k1_ref, k2_ref)
    v_refs = (v0_ref, v1_ref, v2_ref)
    heads = [slice(h * HEAD_DIM, (h + 1) * HEAD_DIM) for h in range(N_HEADS)]
    q = q_ref[...]
    ks = [k_ref[...] for k_ref in k_refs]
    vs = [v_ref[...] for v_ref in v_refs]
    outs = []
    for g0 in range(0, N_HEADS, BAND_HEAD_GROUP):
        group = range(g0, g0 + BAND_HEAD_GROUP)
        scores = []
        for h in group:
            hs = heads[h]
            scs = []
            for w in range(3):
                sc = _nt_dot(q[:, hs], ks[w][:, hs]) + bias_ref[h, w]
                if w < 2:
                    sc = jnp.where(i >= 2 - w, sc, NEG_INF)
                scs.append(sc)
            scores.append(scs)
        weights = []
        for scs in scores:
            m = jnp.max(jnp.maximum(jnp.maximum(scs[0], scs[1]), scs[2]), axis=-1, keepdims=True)
            es = [jnp.exp(sc - m) for sc in scs]
            den = jnp.sum(es[0] + es[1] + es[2], axis=-1, keepdims=True)
            weights.append(([e.astype(BF16) for e in es], den))
        for (es, den), h in zip(weights, group):
            hs = heads[h]
            acc = _dot(es[0], vs[0][:, hs]) + _dot(es[1], vs[1][:, hs]) + _dot(es[2], vs[2][:, hs])
            outs.append(_head_rms(acc / den, g_ref[:, hs]))
    o_ref[...] = jnp.concatenate(outs, axis=1).astype(BF16)


def _toeplitz_bias(rel_bias, n, ncols, offset):
    period = n + ncols + 1
    m = jnp.arange(period)
    shift = jnp.where(m < ncols, m, m - period)
    vec = rel_bias[:, jnp.clip(offset - shift, -MAX_REL, MAX_REL) + MAX_REL].astype(F32)
    rows = jnp.tile(vec, (1, n))[:, :n * (period - 1)].reshape(rel_bias.shape[0], n, period - 1)
    return rows[:, :, :ncols]


def _band_prompt_bias(rel_bias):
    qb = BAND_QBLOCK
    bias = _toeplitz_bias(rel_bias, qb, 3 * qb, 2 * qb)
    kc = jnp.arange(3 * qb)[None, :] // CHUNK - (2 * qb // CHUNK)
    qc = jnp.arange(qb)[:, None] // CHUNK
    valid = (kc <= qc) & (kc >= qc - BAND_CHUNKS)
    bias = jnp.where(valid[None], bias, NEG_INF)
    return bias.reshape(N_HEADS, qb, 3, qb).transpose(0, 2, 1, 3)


def _band_prompt(q, k, v, bias, gain, batch, seq):
    qb = BAND_QBLOCK
    assert seq % qb == 0 and 2 * qb == BAND_PAST and qb % CHUNK == 0
    nq = seq // qb
    kspec = lambda back: pl.BlockSpec((qb, WIDTH), lambda b, i: (b * nq + jnp.maximum(i - back, 0), 0))
    return pl.pallas_call(
        _band_prompt_kernel,
        grid=(batch, nq),
        in_specs=[
            pl.BlockSpec((qb, WIDTH), lambda b, i: (b * nq + i, 0)),
            kspec(2), kspec(1), kspec(0), kspec(2), kspec(1), kspec(0),
            pl.BlockSpec(bias.shape, lambda b, i: (0, 0, 0, 0)),
            pl.BlockSpec((1, WIDTH), lambda b, i: (0, 0)),
        ],
        out_specs=pl.BlockSpec((qb, WIDTH), lambda b, i: (b * nq + i, 0)),
        out_shape=jax.ShapeDtypeStruct((batch * seq, WIDTH), BF16),
        compiler_params=_params("arbitrary", "arbitrary"),
        name="band_prompt",
    )(q, k, k, k, v, v, v, bias, gain)


def _sample_attn_kernel(qa_ref, kan_ref, van_ref, cak_ref, cav_ref,
                        qb_ref, kbn_ref, vbn_ref, kbf_ref, vbf_ref, cbk_ref, cbv_ref,
                        biasp_ref, biasn_ref, ga_ref, gb_ref, mmn_ref, mmp_ref,
                        oa_ref, ob_ref, nbk_ref, nbv_ref):
    sd = qa_ref.shape[0]
    past = cak_ref.shape[2]
    nb = cbk_ref.shape[2]
    row = lax.broadcasted_iota(jnp.int32, (sd, sd), 0)
    col = lax.broadcasted_iota(jnp.int32, (sd, sd), 1)
    diag = col < row
    cak = cak_ref[0].astype(BF16)
    cav = cav_ref[0].astype(BF16)
    cbk = cbk_ref[0].astype(BF16)
    cbv = cbv_ref[0].astype(BF16)
    mmn = mmn_ref[...]
    mmp = mmp_ref[...]
    heads = [slice(h * HEAD_DIM, (h + 1) * HEAD_DIM) for h in range(N_HEADS)]
    blocks = [slice(j * SB_BLOCK, (j + 1) * SB_BLOCK) for j in range(past // SB_BLOCK)]
    qa = qa_ref[...]
    z_new = [_nt_dot(qa[:, hs], kan_ref[:, hs]) for hs in heads]
    z_past = [_dot(qa[:, hs], cak[hs, :]) for hs in heads]
    t_new = [_sb_log_terms(z, diag) for z in z_new]
    t_past = [[_sb_log_terms(z[:, ks], None) for ks in blocks] for z in z_past]
    s_new = [_dot(halves, mmn) for _, halves in t_new]
    s_past = [[_dot(halves, mmp) for _, halves in t] for t in t_past]
    outs_a = []
    for h, hs in enumerate(heads):
        w_new, carry = _sb_weights(t_new[h][0], s_new[h], None, diag)
        w_past = [None] * len(blocks)
        for j in reversed(range(len(blocks))):
            w_past[j], total = _sb_weights(t_past[h][j][0], s_past[h][j], carry, None)
            carry = carry + total
        acc = _dot(w_new.astype(BF16), van_ref[:, hs]) + _nt_dot(jnp.concatenate(w_past, axis=1).astype(BF16), cav[hs, :])
        outs_a.append(_head_rms(acc, ga_ref[:, hs]))
    qb = qb_ref[...]
    sc_p = [_dot(qb[:, hs], cbk[hs, :]) + biasp_ref[h] for h, hs in enumerate(heads)]
    sc_n = [_nt_dot(qb[:, hs], kbn_ref[:, hs]) + biasn_ref[h] for h, hs in enumerate(heads)]
    outs_b = []
    for h, hs in enumerate(heads):
        m = jnp.maximum(jnp.max(sc_p[h], axis=-1, keepdims=True), jnp.max(sc_n[h], axis=-1, keepdims=True))
        ep = jnp.exp(sc_p[h] - m)
        en = jnp.exp(sc_n[h] - m)
        den = jnp.sum(ep, axis=-1, keepdims=True) + jnp.sum(en, axis=-1, keepdims=True)
        acc = _nt_dot(ep.astype(BF16), cbv[hs, :]) + _dot(en.astype(BF16), vbn_ref[:, hs])
        outs_b.append(_head_rms(acc / den, gb_ref[:, hs]))
    oa_ref[...] = jnp.concatenate(outs_a, axis=1).astype(BF16)
    ob_ref[...] = jnp.concatenate(outs_b, axis=1).astype(BF16)
    nbk_ref[0, :, :nb - sd] = cbk_ref[0, :, sd:]
    nbk_ref[0, :, nb - sd:] = kbf_ref[...].T
    nbv_ref[0, :, :nb - sd] = cbv_ref[0, :, sd:]
    nbv_ref[0, :, nb - sd:] = vbf_ref[...].T


def _sample_bias(rel_bias, sd, nb):
    bias = _toeplitz_bias(rel_bias, sd, nb + sd, nb)
    return bias[:, :, :nb], bias[:, :, nb:]


def _sample_attn(qa, kan, van, cak, cav, qb, kbn, vbn, kbf, vbf, cbk, cbv, biasp, biasn, ga, gb, layer):
    _, nbatch, _, past = cak.shape
    nb = cbk.shape[3]
    sd = qa.shape[0] // nbatch
    assert past % SB_BLOCK == 0 and sd % 8 == 0 and sd <= nb and sd <= LANES
    rows = pl.BlockSpec((sd, WIDTH), lambda b: (b, 0))
    cache = lambda n: pl.BlockSpec((None, 1, WIDTH, n), lambda b: (layer, b, 0, 0))
    rolled = pl.BlockSpec((1, WIDTH, nb), lambda b: (b, 0, 0))
    full = lambda a: pl.BlockSpec(a.shape, lambda b: (0,) * a.ndim)
    mmn = _suffix_matrix(sd)
    mmp = _suffix_matrix(SB_BLOCK)
    return pl.pallas_call(
        _sample_attn_kernel,
        grid=(nbatch,),
        in_specs=[rows, rows, rows, cache(past), cache(past),
                  rows, rows, rows, rows, rows, cache(nb), cache(nb),
                  full(biasp), full(biasn), full(ga), full(gb), full(mmn), full(mmp)],
        out_specs=[rows, rows, rolled, rolled],
        out_shape=[jax.ShapeDtypeStruct((nbatch * sd, WIDTH), BF16)] * 2
                  + [jax.ShapeDtypeStruct((nbatch, WIDTH, nb), F32)] * 2,
        compiler_params=_params("arbitrary"),
        name="sample_attn",
    )(qa, kan, van, cak, cav, qb, kbn, vbn, kbf, vbf, cbk, cbv, biasp, biasn, ga, gb, mmn, mmp)


def _merge_kernel(ca_ref, cb_ref, x_ref, woa_ref, wob_ref, g_ref, b_ref, wq_ref, sk_ref,
                  x1_ref, x1t_ref, st_ref, *, alpha):
    mix = _dot(ca_ref[...], woa_ref[...]) + _dot(cb_ref[...], wob_ref[...])
    x1 = _layer_norm(alpha * x_ref[...] + mix, g_ref[...], b_ref[...])
    x1_ref[...] = x1
    x1t_ref[...] = x1.T.astype(BF16)
    qp = _dot(x1.astype(BF16), wq_ref[...])
    half = sk_ref.shape[2]
    for h in range(PEER_HEADS):
        for j in range(2):
            c0 = (2 * h + j) * half
            qh = qp[:, c0:c0 + half].astype(BF16)
            st_ref[(2 * h + j) * N_KEYS:(2 * h + j + 1) * N_KEYS, :] = _nt_dot(sk_ref[j], qh)


def _merge(ca, cb, x, woa, wob, g, b, wq, sk, alpha):
    t, d = x.shape
    tm = min(ROW_TILE, t)
    assert t % tm == 0
    nscore = PEER_HEADS * 2 * N_KEYS
    row = lambda i: (i, 0)
    colb = lambda i: (0, i)
    full = lambda a: pl.BlockSpec(a.shape, lambda i: (0,) * a.ndim)
    return pl.pallas_call(
        functools.partial(_merge_kernel, alpha=alpha),
        grid=(t // tm,),
        in_specs=[pl.BlockSpec((tm, WIDTH), row), pl.BlockSpec((tm, WIDTH), row), pl.BlockSpec((tm, d), row),
                  full(woa), full(wob), full(g), full(b), full(wq), full(sk)],
        out_specs=[pl.BlockSpec((tm, d), row), pl.BlockSpec((d, tm), colb), pl.BlockSpec((nscore, tm), colb)],
        out_shape=[jax.ShapeDtypeStruct((t, d), F32), jax.ShapeDtypeStruct((d, t), BF16),
                   jax.ShapeDtypeStruct((nscore, t), F32)],
        compiler_params=_params("arbitrary"),
        name="merge",
    )(ca, cb, x, woa, wob, g, b, wq, sk)


def _cmp_exchange(v, i, j):
    a, b = v[i], v[j]
    v[i] = jnp.maximum(a, b)
    v[j] = jnp.minimum(a, b)


def _sort16_desc(v):
    v = list(v)
    n = len(v)
    k = 2
    while k <= n:
        j = k // 2
        while j >= 1:
            for i in range(n):
                l = i ^ j
                if l > i:
                    if (i & k) == 0:
                        _cmp_exchange(v, i, l)
                    else:
                        _cmp_exchange(v, l, i)
            j //= 2
        k *= 2
    return v


def _merge_top16(a, b):
    n = len(a)
    top = [jnp.maximum(a[i], b[n - 1 - i]) for i in range(n)]
    out = [jnp.minimum(a[i], b[n - 1 - i]) for i in range(n)]
    while len(out) > 1:
        out = [jnp.maximum(out[2 * i], out[2 * i + 1]) for i in range(len(out) // 2)]
    j = n // 2
    while j >= 1:
        for i in range(n):
            if (i & j) == 0:
                _cmp_exchange(top, i, i + j)
        j //= 2
    return top, out[0]


def _top16_and_next(vals):
    groups = [_sort16_desc(vals[g:g + PEER_TOPK]) for g in range(0, len(vals), PEER_TOPK)]
    nxt = None
    while len(groups) > 1:
        merged = []
        for g in range(0, len(groups), 2):
            top, left = _merge_top16(groups[g], groups[g + 1])
            merged.append(top)
            nxt = left if nxt is None else jnp.maximum(nxt, left)
        groups = merged
    return groups[0], nxt


def _gelu_tanh(x):
    return 0.5 * x * (1.0 + jnp.tanh(0.7978845608028654 * (x + 0.044715 * (x * x * x))))


def _bf16_bits(x):
    return pltpu.bitcast(x.astype(BF16).astype(F32), U32)


def _pack_row_pairs(x, pair_s):
    half = x.shape[0] // 2
    words = []
    for g in range(x.shape[1] // LANES):
        pair_s[g] = x[:, g * LANES:(g + 1) * LANES]
        lo = pair_s[g, pl.ds(0, half, stride=2), :]
        hi = pair_s[g, pl.ds(1, half, stride=2), :]
        words.append(_bf16_bits(hi) | (_bf16_bits(lo) >> 16))
    return jnp.concatenate(words, axis=1)


def _both_halves(x):
    b = _bf16_bits(x)
    return b | (b >> 16)


def _peer_gate_kernel(st_ref, kk_ref, ea_ref, cw_ref, ebw_ref, row_s, pair_s, *, peer_tile):
    tt = st_ref.shape[1]
    ng = tt // LANES
    k1 = PEER_TOPK + 1

    def head(h, carry):
        base = pl.multiple_of(h * 2 * N_KEYS, 2 * N_KEYS)
        s1 = st_ref[pl.ds(base, N_KEYS), :]
        s2 = st_ref[pl.ds(base + N_KEYS, N_KEYS), :]
        s1r = s1.reshape(N_KEYS, ng, LANES)
        s2r = s2.reshape(N_KEYS, ng, LANES)
        top_a, next_a = _top16_and_next([s1r[a] for a in range(N_KEYS)])
        top_b, next_b = _top16_and_next([s2r[a] for a in range(N_KEYS)])
        la = top_a + [next_a]
        lb = top_b + [next_b]
        cands = [la[i - 1] + lb[j - 1] for i in range(1, k1 + 1) for j in range(1, k1 + 1) if i * j <= k1]
        pad = jnp.full_like(cands[0], PAD_SCORE)
        cands = cands + [pad] * (-len(cands) % PEER_TOPK)
        top_c, next_c = _top16_and_next(cands)
        tau = 0.5 * (top_c[PEER_TOPK - 1] + next_c)
        den = jnp.ones_like(tau)
        for cv in top_c[1:]:
            den = den + jnp.exp(cv - top_c[0])
        vals = [tau, la[0], 1.0 / den] + lb
        for r, val in enumerate(vals):
            for g in range(ng):
                row_s[r:r + 1, g * LANES:(g + 1) * LANES] = val[g:g + 1, :]
        tau_r = row_s[0:1, :]
        m1_r = row_s[1:2, :]
        iz_r = row_s[2:3, :]
        lb_r = [row_s[3 + j:4 + j, :] for j in range(k1)]
        code = jnp.zeros_like(s2)
        for j in reversed(range(k1)):
            code = jnp.where(s2 >= lb_r[j], float(k1 - j), code)
        th = tau_r - s1
        n = jnp.zeros_like(s1)
        for j in range(PEER_TOPK):
            n = jnp.where(lb_r[j] >= th, float(j + 1), n)
        kk_ref[:, pl.ds(h, 1), :] = _both_halves(float(k1 + 1) - n)[:, None, :]
        ea_ref[:, pl.ds(h, 1), :] = _both_halves(jnp.exp(s1 - m1_r) * iz_r)[:, None, :]
        rows = pl.ds(pl.multiple_of(h * (N_KEYS // 2), N_KEYS // 2), N_KEYS // 2)
        cw_ref[rows, :] = _pack_row_pairs(code, pair_s)
        ebw = _pack_row_pairs(jnp.exp(s2 - lb_r[0]), pair_s)
        for p0 in range(0, tt, peer_tile):
            p1 = p0 + peer_tile
            ebw_ref[rows, p0 + LANES:p1] = ebw[:, p0:p1 - LANES]
            ebw_ref[rows, p0:p0 + LANES] = ebw[:, p1 - LANES:p1]
        return carry

    lax.fori_loop(0, PEER_HEADS, head, 0)


def _peer_gates(st, peer_tile):
    nscore, t = st.shape
    tt = PEER_GATE_TILE if t % PEER_GATE_TILE == 0 else peer_tile
    assert t % tt == 0 and tt % peer_tile == 0 and peer_tile > LANES and peer_tile % LANES == 0
    rows = jax.ShapeDtypeStruct((N_KEYS, PEER_HEADS, t), U32)
    packed = jax.ShapeDtypeStruct((PEER_HEADS * N_KEYS // 2, t), U32)
    return pl.pallas_call(
        functools.partial(_peer_gate_kernel, peer_tile=peer_tile),
        grid=(t // tt,),
        in_specs=[pl.BlockSpec((nscore, tt), lambda i: (0, i))],
        out_specs=[pl.BlockSpec((N_KEYS, PEER_HEADS, tt), lambda i: (0, 0, i)),
                   pl.BlockSpec((N_KEYS, PEER_HEADS, tt), lambda i: (0, 0, i)),
                   pl.BlockSpec((PEER_HEADS * N_KEYS // 2, tt), lambda i: (0, i)),
                   pl.BlockSpec((PEER_HEADS * N_KEYS // 2, tt), lambda i: (0, i))],
        out_shape=[rows, rows, packed, packed],
        scratch_shapes=[pltpu.VMEM((3 * SUBLANES, tt), F32), pltpu.VMEM((tt // LANES, N_KEYS, LANES), F32)],
        compiler_params=_params("arbitrary"),
        name="peer_gates",
    )(st)


def _peer_kernel(cw_ref, ebw_ref, kk_ref, ea_ref, xt_ref, u_ref, vt_ref, x1_ref, g_ref, b_ref, o_ref,
                 ht0_s, ht1_s, wt0_s, wt1_s, acc_s, *, alpha, nblk):
    c = pl.program_id(0)
    tt = xt_ref.shape[1]
    ch = u_ref.shape[0]

    @pl.when(c == 0)
    def _clear():
        ht1_s[...] = jnp.zeros(ht1_s.shape, F32)
        wt0_s[...] = jnp.zeros(wt0_s.shape, BF16)
        wt1_s[...] = jnp.zeros(wt1_s.shape, BF16)
        acc_s[...] = jnp.zeros(acc_s.shape, F32)

    ablocks = ch // N_KEYS
    lt = PEER_LANE_TILE

    def stages(ht_new, ht_old, wt_new, wt_old):
        d = vt_ref.shape[0]
        npiece = ablocks // 2
        nlv = lt // LANES
        nr2 = N_KEYS // (2 * SUBLANES)
        assert nlv == 2
        for ts in range(tt // lt):
            ls = slice(ts * lt, (ts + 1) * lt)
            for unit in range(npiece * nlv):
                ap, tv = unit // nlv, unit % nlv
                piece = unit // 2
                if unit % 2 == 0:
                    rows = slice(piece * (ch // npiece), (piece + 1) * (ch // npiece))
                    lhs_ref, rhs_ref, kdim = u_ref, xt_ref, u_ref.shape[1]
                else:
                    rows = slice(piece * (d // npiece), (piece + 1) * (d // npiece))
                    lhs_ref, rhs_ref, kdim = vt_ref, wt_old, ch
                nk = kdim // MXU_DEPTH
                part = None
                g = ts * nlv + tv
                lv = slice(g * LANES, (g + 1) * LANES)
                ge = (g + 1) % (tt // LANES)
                le = slice(ge * LANES, (ge + 1) * LANES)
                als = (2 * ap, 2 * ap + 1)
                spread = lambda ref, al, h: pltpu.bitcast(jnp.broadcast_to(ref[al, h:h + 1, lv], (SUBLANES, LANES)), BF16)
                kk = [[spread(kk_ref, al, h) for h in range(PEER_HEADS)] for al in als]
                ea = [[spread(ea_ref, al, h) for h in range(PEER_HEADS)] for al in als]
                for r2 in range(nr2):
                    if r2 % (nr2 // nk) == 0:
                        kb = slice(r2 // (nr2 // nk) * MXU_DEPTH, (r2 // (nr2 // nk) + 1) * MXU_DEPTH)
                        dk = _dot(lhs_ref[rows, kb], rhs_ref[kb, ls])
                        part = dk if part is None else part + dk
                    gates = [jnp.zeros((2 * SUBLANES, LANES), BF16) for _ in als]
                    for h in range(PEER_HEADS):
                        wrows = slice(h * (N_KEYS // 2) + r2 * SUBLANES, h * (N_KEYS // 2) + (r2 + 1) * SUBLANES)
                        code = pltpu.bitcast(cw_ref[wrows, lv], BF16)
                        eb = pltpu.bitcast(ebw_ref[wrows, le], BF16)
                        for i in range(2):
                            gates[i] = gates[i] + jnp.where(code >= kk[i][h], eb, jnp.zeros_like(eb)) * ea[i][h]
                    for i, al in enumerate(als):
                        rs = slice(al * N_KEYS + r2 * 2 * SUBLANES, al * N_KEYS + (r2 + 1) * 2 * SUBLANES)
                        wt_new[rs, lv] = gates[i] * _gelu_tanh(ht_old[rs, lv].astype(BF16))
                if unit % 2 == 0:
                    ht_new[rows, ls] = part
                else:
                    acc_s[rows, ls] += part

    @pl.when(c % 2 == 0)
    def _even():
        stages(ht0_s, ht1_s, wt1_s, wt0_s)

    @pl.when(c % 2 == 1)
    def _odd():
        stages(ht1_s, ht0_s, wt0_s, wt1_s)

    @pl.when(jnp.logical_and(c >= PEER_PIPE_FILL, (c - PEER_PIPE_FILL) % nblk == nblk - 1))
    def _finish():
        y = alpha * x1_ref[...] + acc_s[...].T
        o_ref[...] = _layer_norm(y, g_ref[...], b_ref[...])
        acc_s[...] = jnp.zeros(acc_s.shape, F32)


def _peer(cw, ebw, kk, ea, xt, u_bf, vt_bf, x1, g, b, alpha, tt):
    t, d = x1.shape
    ne = u_bf.shape[0]
    ch = PEER_EXPERT_BLOCK
    assert t % tt == 0 and tt % PEER_LANE_TILE == 0 and ne % ch == 0 and ne == N_KEYS * N_KEYS
    assert vt_bf.shape == (ne // ch, d, ch)
    nblk = ne // ch
    ablocks = ch // N_KEYS
    npairs = (t // tt) * nblk
    tile = lambda c, lag: jnp.clip(c - lag, 0, npairs - 1) // nblk
    block = lambda c, lag: jnp.clip(c - lag, 0, npairs - 1) % nblk
    return pl.pallas_call(
        functools.partial(_peer_kernel, alpha=alpha, nblk=nblk),
        grid=(npairs + PEER_PIPE_FILL,),
        in_specs=[
            pl.BlockSpec((PEER_HEADS * N_KEYS // 2, tt), lambda c: (0, tile(c, 1))),
            pl.BlockSpec((PEER_HEADS * N_KEYS // 2, tt), lambda c: (0, tile(c, 1))),
            pl.BlockSpec((ablocks, PEER_HEADS, tt), lambda c: (block(c, 1), 0, tile(c, 1))),
            pl.BlockSpec((ablocks, PEER_HEADS, tt), lambda c: (block(c, 1), 0, tile(c, 1))),
            pl.BlockSpec((d, tt), lambda c: (0, tile(c, 0))),
            pl.BlockSpec((ch, d), lambda c: (block(c, 0), 0)),
            pl.BlockSpec((None, d, ch), lambda c: (block(c, PEER_PIPE_FILL), 0, 0)),
            pl.BlockSpec((tt, d), lambda c: (tile(c, PEER_PIPE_FILL), 0)),
            pl.BlockSpec((1, d), lambda c: (0, 0)),
            pl.BlockSpec((1, d), lambda c: (0, 0)),
        ],
        out_specs=pl.BlockSpec((tt, d), lambda c: (tile(c, PEER_PIPE_FILL), 0)),
        out_shape=jax.ShapeDtypeStruct((t, d), F32),
        scratch_shapes=[
            pltpu.VMEM((ch, tt), F32),
            pltpu.VMEM((ch, tt), F32),
            pltpu.VMEM((ch, tt), BF16),
            pltpu.VMEM((ch, tt), BF16),
            pltpu.VMEM((d, tt), F32),
        ],
        compiler_params=_params("arbitrary"),
        name="peer",
    )(cw, ebw, kk, ea, xt, u_bf, vt_bf, x1, g, b)


def kernel(x_prompt, x_sample, cache_sb_k, cache_sb_v, cache_band_k, cache_band_v, w_in, w_out, gn_a, gn_b,
           rel_bias, ln1_g, ln1_b, peer_query, peer_subkeys, peer_u, peer_v, ln2_g, ln2_b):
    batch, seq, d = x_prompt.shape
    dec_batch, dec_seq, _ = x_sample.shape
    depth = w_in.shape[0]
    past = cache_sb_k.shape[2]
    nb = cache_band_k.shape[2]
    assert w_in.shape[2] == 6 * WIDTH and w_out.shape[1] == 2 * WIDTH
    assert seq >= BAND_PAST and nb == BAND_PAST
    alpha = float((2 * depth) ** 0.25)
    tp = batch * seq
    ts = dec_batch * dec_seq

    xp = x_prompt.reshape(tp, d)
    xs = x_sample.reshape(ts, d)
    row2 = lambda a: a.reshape(1, -1)
    heads = lambda a, n, s: a.reshape(n, s, N_HEADS, HEAD_DIM)
    to_slab = lambda a: jnp.transpose(a, (0, 1, 3, 4, 2)).reshape(a.shape[0], a.shape[1], WIDTH, a.shape[2])
    caches = [to_slab(c) for c in (cache_sb_k, cache_sb_v, cache_band_k, cache_band_v)]
    from_slab = lambda a: jnp.transpose(a.reshape(a.shape[0], N_HEADS, HEAD_DIM, a.shape[2]), (0, 3, 1, 2))
    outs = [[] for _ in range(8)]
    for l in range(depth):
        w_in_bf = w_in[l].astype(BF16)
        woa = w_out[l, :WIDTH].astype(BF16)
        wob = w_out[l, WIDTH:].astype(BF16)
        wq = peer_query[l].astype(BF16)
        sk = peer_subkeys[l].astype(BF16)
        u_bf = peer_u[l].astype(BF16)
        vt_bf = peer_v[l].reshape(-1, PEER_EXPERT_BLOCK, d).transpose(0, 2, 1).astype(BF16)
        ga, gb = row2(gn_a[l]), row2(gn_b[l])
        g1, b1, g2, b2 = row2(ln1_g[l]), row2(ln1_b[l]), row2(ln2_g[l]), row2(ln2_b[l])

        qa, kab, vab, qb, kbb, vbb, sbk, sbv, bdk, bdv = _project_prompt(xp, w_in_bf, batch, seq)
        ca = _sb_prompt(qa, kab, vab, ga, batch, seq)
        cb = _band_prompt(qb, kbb, vbb, _band_prompt_bias(rel_bias[l]), gb, batch, seq)
        x1, x1t, st = _merge(ca, cb, xp, woa, wob, g1, b1, wq, sk, alpha)
        tt = min(PEER_TOKEN_TILE, tp)
        kk, ea, cw, ebw = _peer_gates(st, tt)
        xp = _peer(cw, ebw, kk, ea, x1t, u_bf, vt_bf, x1, g2, b2, alpha, tt)
        outs[0].append(from_slab(sbk))
        outs[1].append(from_slab(sbv))
        outs[2].append(from_slab(bdk))
        outs[3].append(from_slab(bdv))

        qa, ka, va, kab, vab, qb, kb, vb, kbb, vbb = _project(xs, w_in_bf)
        biasp, biasn = _sample_bias(rel_bias[l], dec_seq, nb)
        ca, cb, nbk, nbv = _sample_attn(
            qa, kab, vab, caches[0], caches[1], qb, kbb, vbb, kb, vb, caches[2], caches[3], biasp, biasn, ga, gb, l)
        x1, x1t, st = _merge(ca, cb, xs, woa, wob, g1, b1, wq, sk, alpha)
        tt = min(PEER_TOKEN_TILE, ts)
        kk, ea, cw, ebw = _peer_gates(st, tt)
        xs = _peer(cw, ebw, kk, ea, x1t, u_bf, vt_bf, x1, g2, b2, alpha, tt)
        outs[4].append(heads(ka, dec_batch, dec_seq))
        outs[5].append(heads(va, dec_batch, dec_seq))
        outs[6].append(from_slab(nbk))
        outs[7].append(from_slab(nbv))

    return (xp.reshape(batch, seq, d), xs.reshape(dec_batch, dec_seq, d)) + tuple(jnp.stack(o) for o in outs)
```

```python
import functools

import jax
import jax.numpy as jnp
import numpy as np
from jax import lax
from jax.experimental import pallas as pl
from jax.experimental.pallas import tpu as pltpu

F32 = jnp.float32
BF16 = jnp.bfloat16
U32 = jnp.uint32

HEAD_DIM = 64
N_HEADS = 8
WIDTH = N_HEADS * HEAD_DIM
CHUNK = 64
BAND_CHUNKS = 8
BAND_PAST = BAND_CHUNKS * CHUNK
MAX_REL = 128
N_KEYS = 128
PEER_HEADS = 8
PEER_TOPK = 16
NORM_EPS = 1e-5
NEG_INF = -1e30

LANES = 128
SUBLANES = 8
MXU_DEPTH = 256
VMEM_LIMIT_BYTES = 56 * 1024 * 1024

SB_DEAD_LOG = -104.0
SB_BLOCK = 128
BAND_QBLOCK = 256
BAND_HEAD_GROUP = 4
ROW_TILE = 512
PEER_TOKEN_TILE = 1024
PEER_GATE_TILE = 1024
PEER_EXPERT_BLOCK = 512
PEER_PIPE_FILL = 2
PEER_LANE_TILE = 256
PAD_SCORE = -3.0e38


def _params(*sem):
    return pltpu.CompilerParams(dimension_semantics=sem, vmem_limit_bytes=VMEM_LIMIT_BYTES)


def _nt_dot(a, b):
    return lax.dot_general(a, b, (((1,), (1,)), ((), ())), preferred_element_type=F32)


def _dot(a, b):
    return jnp.dot(a, b, preferred_element_type=F32)


def _layer_norm(y, g, b):
    mu = jnp.mean(y, axis=-1, keepdims=True)
    d = y - mu
    var = jnp.mean(d * d, axis=-1, keepdims=True)
    return d * lax.rsqrt(var + NORM_EPS) * g + b


def _head_rms(o, gain):
    ms = jnp.mean(o * o, axis=-1, keepdims=True)
    return o * lax.rsqrt(ms + NORM_EPS) * gain


def _proj_kernel(x_ref, w_ref, qa_ref, ka_ref, va_ref, kab_ref, vab_ref,
                 qb_ref, kb_ref, vb_ref, kbb_ref, vbb_ref):
    xb = x_ref[...].astype(BF16)
    scale = HEAD_DIM ** -0.5

    def group(g):
        return _dot(xb, w_ref[:, g * WIDTH:(g + 1) * WIDTH])

    qa_ref[...] = (group(0) * scale).astype(BF16)
    k = group(1)
    ka_ref[...] = k
    kab_ref[...] = k.astype(BF16)
    v = group(2)
    va_ref[...] = v
    vab_ref[...] = v.astype(BF16)
    qb_ref[...] = (group(3) * scale).astype(BF16)
    k = group(4)
    kb_ref[...] = k
    kbb_ref[...] = k.astype(BF16)
    v = group(5)
    vb_ref[...] = v
    vbb_ref[...] = v.astype(BF16)


def _project(x, w_bf):
    t, d = x.shape
    tm = min(ROW_TILE, t)
    assert t % tm == 0
    row = lambda i: (i, 0)
    f32o = jax.ShapeDtypeStruct((t, WIDTH), F32)
    bf16o = jax.ShapeDtypeStruct((t, WIDTH), BF16)
    blk = pl.BlockSpec((tm, WIDTH), row)
    return pl.pallas_call(
        _proj_kernel,
        grid=(t // tm,),
        in_specs=[pl.BlockSpec((tm, d), row), pl.BlockSpec(w_bf.shape, lambda i: (0, 0))],
        out_specs=[blk] * 10,
        out_shape=[bf16o, f32o, f32o, bf16o, bf16o, bf16o, f32o, f32o, bf16o, bf16o],
        compiler_params=_params("arbitrary"),
        name="proj",
    )(x, w_bf)


def _proj_prompt_kernel(x_ref, w_ref, qa_ref, kab_ref, vab_ref, qb_ref, kbb_ref, vbb_ref,
                        sbk_ref, sbv_ref, bdk_ref, bdv_ref, *, tiles_per_stream):
    j = pl.program_id(0) % tiles_per_stream
    xb = x_ref[...].astype(BF16)
    scale = HEAD_DIM ** -0.5

    def group(g):
        return _dot(xb, w_ref[:, g * WIDTH:(g + 1) * WIDTH])

    qa_ref[...] = (group(0) * scale).astype(BF16)
    k = group(1)
    kab_ref[...] = k.astype(BF16)
    sbk_ref[0] = k.T
    v = group(2)
    vab_ref[...] = v.astype(BF16)
    sbv_ref[0] = v.T
    qb_ref[...] = (group(3) * scale).astype(BF16)
    kb = group(4)
    kbb_ref[...] = kb.astype(BF16)
    vb = group(5)
    vbb_ref[...] = vb.astype(BF16)

    @pl.when(j == tiles_per_stream - 1)
    def _band_rows():
        bdk_ref[0] = kb.T
        bdv_ref[0] = vb.T


def _project_prompt(x, w_bf, batch, seq):
    t, d = x.shape
    tm = ROW_TILE
    assert seq % tm == 0 and tm == BAND_PAST and t == batch * seq
    nj = seq // tm
    row = lambda i: (i, 0)
    bf16o = jax.ShapeDtypeStruct((t, WIDTH), BF16)
    blk = pl.BlockSpec((tm, WIDTH), row)
    return pl.pallas_call(
        functools.partial(_proj_prompt_kernel, tiles_per_stream=nj),
        grid=(t // tm,),
        in_specs=[pl.BlockSpec((tm, d), row), pl.BlockSpec(w_bf.shape, lambda i: (0, 0))],
        out_specs=[blk] * 6 + [pl.BlockSpec((1, WIDTH, tm), lambda i: (i // nj, 0, i % nj))] * 2
                  + [pl.BlockSpec((1, WIDTH, tm), lambda i: (i // nj, 0, 0))] * 2,
        out_shape=[bf16o] * 6 + [jax.ShapeDtypeStruct((batch, WIDTH, seq), F32)] * 2
                  + [jax.ShapeDtypeStruct((batch, WIDTH, tm), F32)] * 2,
        compiler_params=_params("arbitrary"),
        name="proj_prompt",
    )(x, w_bf)


def _suffix_matrix(kb):
    kp = np.arange(kb)
    m = (kp[:, None] > kp[None, :]).astype(np.float32)
    one = np.concatenate([np.ones((kb, LANES), np.float32), m], axis=1)
    return jnp.asarray(np.concatenate([one, one], axis=0), dtype=BF16)


def _sb_log_terms(z, mask):
    sp = jnp.maximum(z, 0.0) + jnp.log(1.0 + jnp.exp(-jnp.abs(z)))
    log_keep = -sp
    if mask is not None:
        log_keep = jnp.where(mask, log_keep, 0.0)
    hi = log_keep.astype(BF16)
    lo = (log_keep - hi.astype(F32)).astype(BF16)
    return z - sp, jnp.concatenate([hi, lo], axis=1)


def _sb_weights(log_sig, sums, carry, mask):
    kb = log_sig.shape[1]
    log_w = log_sig + sums[:, LANES:LANES + kb]
    if carry is not None:
        log_w = log_w + carry
    w = jnp.exp(log_w)
    if mask is not None:
        w = jnp.where(mask, w, 0.0)
    return w, sums[:, :LANES]


def _sb_prompt_kernel(q_ref, k_ref, v_ref, g_ref, mm_ref, o_ref, carry_s, acc_s):
    i = pl.program_id(1)
    r = q_ref.shape[0]
    row = lax.broadcasted_iota(jnp.int32, (r, r), 0)
    col = lax.broadcasted_iota(jnp.int32, (r, r), 1)
    diag = col < row
    heads = [slice(h * HEAD_DIM, (h + 1) * HEAD_DIM) for h in range(N_HEADS)]

    def key_block(j, first):
        off = pl.multiple_of(j * r, r)
        q = q_ref[...]
        kj = k_ref[pl.ds(off, r), :]
        vj = v_ref[pl.ds(off, r), :]
        mm = mm_ref[...]
        mask = diag if first else None
        zs = [_nt_dot(q[:, hs], kj[:, hs]) for hs in heads]
        terms = [_sb_log_terms(z, mask) for z in zs]
        sums = [_dot(halves, mm) for _, halves in terms]
        ws = [_sb_weights(terms[h][0], sums[h], None if first else carry_s[h], mask) for h in range(N_HEADS)]
        pvs = [_dot(ws[h][0].astype(BF16), vj[:, hs]) for h, hs in enumerate(heads)]
        alive = None
        for h in range(N_HEADS):
            carry = ws[h][1] if first else carry_s[h] + ws[h][1]
            carry_s[h] = carry
            acc_s[h] = pvs[h] if first else acc_s[h] + pvs[h]
            alive = carry if alive is None else jnp.maximum(alive, carry)
        return jnp.max(alive)

    def cond(s):
        j, cmax = s
        return jnp.logical_and(j >= 0, cmax > SB_DEAD_LOG)

    def body(s):
        j, _ = s
        return j - 1, key_block(j, False)

    lax.while_loop(cond, body, (i - 1, key_block(i, True)))
    outs = [_head_rms(acc_s[h], g_ref[:, hs]) for h, hs in enumerate(heads)]
    o_ref[...] = jnp.concatenate(outs, axis=1).astype(BF16)


def _sb_prompt(q, k, v, gain, batch, seq):
    r = SB_BLOCK
    assert seq % r == 0
    nq = seq // r
    return pl.pallas_call(
        _sb_prompt_kernel,
        grid=(batch, nq),
        in_specs=[
            pl.BlockSpec((r, WIDTH), lambda b, i: (b * nq + i, 0)),
            pl.BlockSpec((seq, WIDTH), lambda b, i: (b, 0)),
            pl.BlockSpec((seq, WIDTH), lambda b, i: (b, 0)),
            pl.BlockSpec((1, WIDTH), lambda b, i: (0, 0)),
            pl.BlockSpec((2 * r, LANES + r), lambda b, i: (0, 0)),
        ],
        out_specs=pl.BlockSpec((r, WIDTH), lambda b, i: (b * nq + i, 0)),
        out_shape=jax.ShapeDtypeStruct((batch * seq, WIDTH), BF16),
        scratch_shapes=[pltpu.VMEM((N_HEADS, r, LANES), F32),
                        pltpu.VMEM((N_HEADS, r, HEAD_DIM), F32)],
        compiler_params=_params("arbitrary", "arbitrary"),
        name="sb_prompt",
    )(q, k, v, gain, _suffix_matrix(r))


def _band_prompt_kernel(q_ref, k0_ref, k1_ref, k2_ref, v0_ref, v1_ref, v2_ref, bias_ref, g_ref, o_ref):
    i = pl.program_id(1)
    k_refs = (k0_ref, k1_ref, k2_ref)
    v_refs = (v0_ref, v1_ref, v2_ref)
    heads = [slice(h * HEAD_DIM, (h + 1) * HEAD_DIM) for h in range(N_HEADS)]
    q = q_ref[...]
    ks = [k_ref[...] for k_ref in k_refs]
    vs = [v_ref[...] for v_ref in v_refs]
    outs = []
    for g0 in range(0, N_HEADS, BAND_HEAD_GROUP):
        group = range(g0, g0 + BAND_HEAD_GROUP)
        scores = []
        for h in group:
            hs = heads[h]
            scs = []
            for w in range(3):
                sc = _nt_dot(q[:, hs], ks[w][:, hs]) + bias_ref[h, w]
                if w < 2:
                    sc = jnp.where(i >= 2 - w, sc, NEG_INF)
                scs.append(sc)
            scores.append(scs)
        weights = []
        for scs in scores:
            m = jnp.max(jnp.maximum(jnp.maximum(scs[0], scs[1]), scs[2]), axis=-1, keepdims=True)
            es = [jnp.exp(sc - m) for sc in scs]
            den = jnp.sum(es[0] + es[1] + es[2], axis=-1, keepdims=True)
            weights.append(([e.astype(BF16) for e in es], den))
        for (es, den), h in zip(weights, group):
            hs = heads[h]
            acc = _dot(es[0], vs[0][:, hs]) + _dot(es[1], vs[1][:, hs]) + _dot(es[2], vs[2][:, hs])
            outs.append(_head_rms(acc / den, g_ref[:, hs]))
    o_ref[...] = jnp.concatenate(outs, axis=1).astype(BF16)


def _toeplitz_bias(rel_bias, n, ncols, offset):
    period = n + ncols + 1
    m = jnp.arange(period)
    shift = jnp.where(m < ncols, m, m - period)
    vec = rel_bias[:, jnp.clip(offset - shift, -MAX_REL, MAX_REL) + MAX_REL].astype(F32)
    rows = jnp.tile(vec, (1, n))[:, :n * (period - 1)].reshape(rel_bias.shape[0], n, period - 1)
    return rows[:, :, :ncols]


def _band_prompt_bias(rel_bias):
    qb = BAND_QBLOCK
    bias = _toeplitz_bias(rel_bias, qb, 3 * qb, 2 * qb)
    kc = jnp.arange(3 * qb)[None, :] // CHUNK - (2 * qb // CHUNK)
    qc = jnp.arange(qb)[:, None] // CHUNK
    valid = (kc <= qc) & (kc >= qc - BAND_CHUNKS)
    bias = jnp.where(valid[None], bias, NEG_INF)
    return bias.reshape(N_HEADS, qb, 3, qb).transpose(0, 2, 1, 3)


def _band_prompt(q, k, v, bias, gain, batch, seq):
    qb = BAND_QBLOCK
    assert seq % qb == 0 and 2 * qb == BAND_PAST and qb % CHUNK == 0
    nq = seq // qb
    kspec = lambda back: pl.BlockSpec((qb, WIDTH), lambda b, i: (b * nq + jnp.maximum(i - back, 0), 0))
    return pl.pallas_call(
        _band_prompt_kernel,
        grid=(batch, nq),
        in_specs=[
            pl.BlockSpec((qb, WIDTH), lambda b, i: (b * nq + i, 0)),
            kspec(2), kspec(1), kspec(0), kspec(2), kspec(1), kspec(0),
            pl.BlockSpec(bias.shape, lambda b, i: (0, 0, 0, 0)),
            pl.BlockSpec((1, WIDTH), lambda b, i: (0, 0)),
        ],
        out_specs=pl.BlockSpec((qb, WIDTH), lambda b, i: (b * nq + i, 0)),
        out_shape=jax.ShapeDtypeStruct((batch * seq, WIDTH), BF16),
        compiler_params=_params("arbitrary", "arbitrary"),
        name="band_prompt",
    )(q, k, k, k, v, v, v, bias, gain)


def _sample_attn_kernel(qa_ref, kan_ref, van_ref, cak_ref, cav_ref,
                        qb_ref, kbn_ref, vbn_ref, kbf_ref, vbf_ref, cbk_ref, cbv_ref,
                        biasp_ref, biasn_ref, ga_ref, gb_ref, mmn_ref, mmp_ref,
                        oa_ref, ob_ref, nbk_ref, nbv_ref):
    sd = qa_ref.shape[0]
    past = cak_ref.shape[2]
    nb = cbk_ref.shape[2]
    row = lax.broadcasted_iota(jnp.int32, (sd, sd), 0)
    col = lax.broadcasted_iota(jnp.int32, (sd, sd), 1)
    diag = col < row
    cak = cak_ref[0].astype(BF16)
    cav = cav_ref[0].astype(BF16)
    cbk = cbk_ref[0].astype(BF16)
    cbv = cbv_ref[0].astype(BF16)
    mmn = mmn_ref[...]
    mmp = mmp_ref[...]
    heads = [slice(h * HEAD_DIM, (h + 1) * HEAD_DIM) for h in range(N_HEADS)]
    blocks = [slice(j * SB_BLOCK, (j + 1) * SB_BLOCK) for j in range(past // SB_BLOCK)]
    qa = qa_ref[...]
    z_new = [_nt_dot(qa[:, hs], kan_ref[:, hs]) for hs in heads]
    z_past = [_dot(qa[:, hs], cak[hs, :]) for hs in heads]
    t_new = [_sb_log_terms(z, diag) for z in z_new]
    t_past = [[_sb_log_terms(z[:, ks], None) for ks in blocks] for z in z_past]
    s_new = [_dot(halves, mmn) for _, halves in t_new]
    s_past = [[_dot(halves, mmp) for _, halves in t] for t in t_past]
    outs_a = []
    for h, hs in enumerate(heads):
        w_new, carry = _sb_weights(t_new[h][0], s_new[h], None, diag)
        w_past = [None] * len(blocks)
        for j in reversed(range(len(blocks))):
            w_past[j], total = _sb_weights(t_past[h][j][0], s_past[h][j], carry, None)
            carry = carry + total
        acc = _dot(w_new.astype(BF16), van_ref[:, hs]) + _nt_dot(jnp.concatenate(w_past, axis=1).astype(BF16), cav[hs, :])
        outs_a.append(_head_rms(acc, ga_ref[:, hs]))
    qb = qb_ref[...]
    sc_p = [_dot(qb[:, hs], cbk[hs, :]) + biasp_ref[h] for h, hs in enumerate(heads)]
    sc_n = [_nt_dot(qb[:, hs], kbn_ref[:, hs]) + biasn_ref[h] for h, hs in enumerate(heads)]
    outs_b = []
    for h, hs in enumerate(heads):
        m = jnp.maximum(jnp.max(sc_p[h], axis=-1, keepdims=True), jnp.max(sc_n[h], axis=-1, keepdims=True))
        ep = jnp.exp(sc_p[h] - m)
        en = jnp.exp(sc_n[h] - m)
        den = jnp.sum(ep, axis=-1, keepdims=True) + jnp.sum(en, axis=-1, keepdims=True)
        acc = _nt_dot(ep.astype(BF16), cbv[hs, :]) + _dot(en.astype(BF16), vbn_ref[:, hs])
        outs_b.append(_head_rms(acc / den, gb_ref[:, hs]))
    oa_ref[...] = jnp.concatenate(outs_a, axis=1).astype(BF16)
    ob_ref[...] = jnp.concatenate(outs_b, axis=1).astype(BF16)
    nbk_ref[0, :, :nb - sd] = cbk_ref[0, :, sd:]
    nbk_ref[0, :, nb - sd:] = kbf_ref[...].T
    nbv_ref[0, :, :nb - sd] = cbv_ref[0, :, sd:]
    nbv_ref[0, :, nb - sd:] = vbf_ref[...].T


def _sample_bias(rel_bias, sd, nb):
    bias = _toeplitz_bias(rel_bias, sd, nb + sd, nb)
    return bias[:, :, :nb], bias[:, :, nb:]


def _sample_attn(qa, kan, van, cak, cav, qb, kbn, vbn, kbf, vbf, cbk, cbv, biasp, biasn, ga, gb, layer):
    _, nbatch, _, past = cak.shape
    nb = cbk.shape[3]
    sd = qa.shape[0] // nbatch
    assert past % SB_BLOCK == 0 and sd % 8 == 0 and sd <= nb and sd <= LANES
    rows = pl.BlockSpec((sd, WIDTH), lambda b: (b, 0))
    cache = lambda n: pl.BlockSpec((None, 1, WIDTH, n), lambda b: (layer, b, 0, 0))
    rolled = pl.BlockSpec((1, WIDTH, nb), lambda b: (b, 0, 0))
    full = lambda a: pl.BlockSpec(a.shape, lambda b: (0,) * a.ndim)
    mmn = _suffix_matrix(sd)
    mmp = _suffix_matrix(SB_BLOCK)
    return pl.pallas_call(
        _sample_attn_kernel,
        grid=(nbatch,),
        in_specs=[rows, rows, rows, cache(past), cache(past),
                  rows, rows, rows, rows, rows, cache(nb), cache(nb),
                  full(biasp), full(biasn), full(ga), full(gb), full(mmn), full(mmp)],
        out_specs=[rows, rows, rolled, rolled],
        out_shape=[jax.ShapeDtypeStruct((nbatch * sd, WIDTH), BF16)] * 2
                  + [jax.ShapeDtypeStruct((nbatch, WIDTH, nb), F32)] * 2,
        compiler_params=_params("arbitrary"),
        name="sample_attn",
    )(qa, kan, van, cak, cav, qb, kbn, vbn, kbf, vbf, cbk, cbv, biasp, biasn, ga, gb, mmn, mmp)


def _merge_kernel(ca_ref, cb_ref, x_ref, woa_ref, wob_ref, g_ref, b_ref, wq_ref, sk_ref,
                  x1_ref, x1t_ref, st_ref, *, alpha):
    mix = _dot(ca_ref[...], woa_ref[...]) + _dot(cb_ref[...], wob_ref[...])
    x1 = _layer_norm(alpha * x_ref[...] + mix, g_ref[...], b_ref[...])
    x1_ref[...] = x1
    x1t_ref[...] = x1.T.astype(BF16)
    qp = _dot(x1.astype(BF16), wq_ref[...])
    half = sk_ref.shape[2]
    for h in range(PEER_HEADS):
        for j in range(2):
            c0 = (2 * h + j) * half
            qh = qp[:, c0:c0 + half].astype(BF16)
            st_ref[(2 * h + j) * N_KEYS:(2 * h + j + 1) * N_KEYS, :] = _nt_dot(sk_ref[j], qh)


def _merge(ca, cb, x, woa, wob, g, b, wq, sk, alpha):
    t, d = x.shape
    tm = min(ROW_TILE, t)
    assert t % tm == 0
    nscore = PEER_HEADS * 2 * N_KEYS
    row = lambda i: (i, 0)
    colb = lambda i: (0, i)
    full = lambda a: pl.BlockSpec(a.shape, lambda i: (0,) * a.ndim)
    return pl.pallas_call(
        functools.partial(_merge_kernel, alpha=alpha),
        grid=(t // tm,),
        in_specs=[pl.BlockSpec((tm, WIDTH), row), pl.BlockSpec((tm, WIDTH), row), pl.BlockSpec((tm, d), row),
                  full(woa), full(wob), full(g), full(b), full(wq), full(sk)],
        out_specs=[pl.BlockSpec((tm, d), row), pl.BlockSpec((d, tm), colb), pl.BlockSpec((nscore, tm), colb)],
        out_shape=[jax.ShapeDtypeStruct((t, d), F32), jax.ShapeDtypeStruct((d, t), BF16),
                   jax.ShapeDtypeStruct((nscore, t), F32)],
        compiler_params=_params("arbitrary"),
        name="merge",
    )(ca, cb, x, woa, wob, g, b, wq, sk)


def _cmp_exchange(v, i, j):
    a, b = v[i], v[j]
    v[i] = jnp.maximum(a, b)
    v[j] = jnp.minimum(a, b)


_SORT16_NETWORK = (
    (0, 13), (1, 12), (2, 15), (3, 14), (4, 8), (5, 6), (7, 11), (9, 10),
    (0, 5), (1, 7), (2, 9), (3, 4), (6, 13), (8, 14), (10, 15), (11, 12),
    (0, 1), (2, 3), (4, 5), (6, 8), (7, 9), (10, 11), (12, 13), (14, 15),
    (0, 2), (1, 3), (4, 10), (5, 11), (6, 7), (8, 9), (12, 14), (13, 15),
    (1, 2), (3, 12), (4, 6), (5, 7), (8, 10), (9, 11), (13, 14),
    (1, 4), (2, 6), (5, 8), (7, 10), (9, 13), (11, 14),
    (2, 4), (3, 6), (9, 12), (11, 13),
    (3, 5), (6, 8), (7, 9), (10, 12),
    (3, 4), (5, 6), (7, 8), (9, 10), (11, 12),
    (6, 7), (8, 9),
)


def _sort16_desc(v):
    v = list(v)
    assert len(v) == PEER_TOPK
    for i, j in _SORT16_NETWORK:
        _cmp_exchange(v, i, j)
    return v


def _count_true_suffix(test, rows):
    sel = jnp.where
    m1 = test(rows[8])
    m2 = test(sel(m1, rows[4], rows[12]))
    m3 = test(sel(m1, sel(m2, rows[2], rows[6]), sel(m2, rows[10], rows[14])))
    m4 = test(sel(m1, sel(m2, sel(m3, rows[1], rows[3]), sel(m3, rows[5], rows[7])),
                  sel(m2, sel(m3, rows[9], rows[11]), sel(m3, rows[13], rows[15]))))
    return sel(m1, 8.0, 0.0) + sel(m2, 4.0, 0.0) + sel(m3, 2.0, 0.0) + sel(m4, 1.0, 0.0) + sel(test(rows[0]), 1.0, 0.0)


def _merge_top16(a, b):
    n = len(a)
    top = [jnp.maximum(a[i], b[n - 1 - i]) for i in range(n)]
    out = [jnp.minimum(a[i], b[n - 1 - i]) for i in range(n)]
    while len(out) > 1:
        out = [jnp.maximum(out[2 * i], out[2 * i + 1]) for i in range(len(out) // 2)]
    j = n // 2
    while j >= 1:
        for i in range(n):
            if (i & j) == 0:
                _cmp_exchange(top, i, i + j)
        j //= 2
    return top, out[0]


def _top16_and_next(vals):
    groups = [_sort16_desc(vals[g:g + PEER_TOPK]) for g in range(0, len(vals), PEER_TOPK)]
    nxt = None
    while len(groups) > 1:
        merged = []
        for g in range(0, len(groups), 2):
            top, left = _merge_top16(groups[g], groups[g + 1])
            merged.append(top)
            nxt = left if nxt is None else jnp.maximum(nxt, left)
        groups = merged
    return groups[0], nxt


def _gelu_tanh(x):
    return 0.5 * x * (1.0 + jnp.tanh(0.7978845608028654 * (x + 0.044715 * (x * x * x))))


def _bf16_bits(x):
    return pltpu.bitcast(x.astype(BF16).astype(F32), U32)


def _pack_row_pairs(x, pair_s):
    half = x.shape[0] // 2
    pair_s[...] = x
    lo = pair_s[pl.ds(0, half, stride=2), :]
    hi = pair_s[pl.ds(1, half, stride=2), :]
    return _bf16_bits(hi) | (_bf16_bits(lo) >> 16)


def _both_halves(x):
    b = _bf16_bits(x)
    return b | (b >> 16)


def _peer_gate_kernel(st_ref, kk_ref, ea_ref, cw_ref, ebw_ref, row_s, pair_s, *, peer_tile):
    tt = st_ref.shape[1]
    ng = tt // LANES
    k1 = PEER_TOPK + 1
    tiles_per_peer = peer_tile // LANES

    def head(h, carry):
        base = pl.multiple_of(h * 2 * N_KEYS, 2 * N_KEYS)
        s1 = st_ref[pl.ds(base, N_KEYS), :]
        s2 = st_ref[pl.ds(base + N_KEYS, N_KEYS), :]
        s1r = s1.reshape(N_KEYS, ng, LANES)
        s2r = s2.reshape(N_KEYS, ng, LANES)
        top_a, next_a = _top16_and_next([s1r[a] for a in range(N_KEYS)])
        top_b, next_b = _top16_and_next([s2r[a] for a in range(N_KEYS)])
        la = top_a + [next_a]
        lb = top_b + [next_b]
        cands = [la[i - 1] + lb[j - 1] for i in range(1, k1 + 1) for j in range(1, k1 + 1) if i * j <= k1]
        pad = jnp.full_like(cands[0], PAD_SCORE)
        cands = cands + [pad] * (-len(cands) % PEER_TOPK)
        top_c, next_c = _top16_and_next(cands)
        tau = 0.5 * (top_c[PEER_TOPK - 1] + next_c)
        den = jnp.ones_like(tau)
        for cv in top_c[1:]:
            den = den + jnp.exp(cv - top_c[0])
        vals = [tau, la[0], 1.0 / den] + lb
        for r, val in enumerate(vals):
            for g in range(ng):
                row_s[r:r + 1, g * LANES:(g + 1) * LANES] = val[g:g + 1, :]
        rows = pl.ds(pl.multiple_of(h * (N_KEYS // 2), N_KEYS // 2), N_KEYS // 2)
        kk_words, ea_words = [], []
        for g in range(ng):
            lg = slice(g * LANES, (g + 1) * LANES)
            s1g = st_ref[pl.ds(base, N_KEYS), lg]
            s2g = st_ref[pl.ds(base + N_KEYS, N_KEYS), lg]
            tau_r = row_s[0:1, lg]
            m1_r = row_s[1:2, lg]
            iz_r = row_s[2:3, lg]
            lb_r = [row_s[3 + j:4 + j, lg] for j in range(k1)]
            code = _count_true_suffix(lambda row: s2g >= row, lb_r[1:]) + jnp.where(s2g >= lb_r[0], 1.0, 0.0)
            th = tau_r - s1g
            n = _count_true_suffix(lambda row: row >= th, lb_r[PEER_TOPK - 1::-1])
            kk_words.append(_both_halves(float(k1 + 1) - n))
            ea_words.append(_both_halves(jnp.exp(s1g - m1_r) * iz_r))
            cw_ref[rows, lg] = _pack_row_pairs(code, pair_s)
            gr = g // tiles_per_peer * tiles_per_peer + (g + 1) % tiles_per_peer
            ebw_ref[rows, gr * LANES:(gr + 1) * LANES] = _pack_row_pairs(jnp.exp(s2g - lb_r[0]), pair_s)
        kk_ref[:, pl.ds(h, 1), :] = jnp.concatenate(kk_words, axis=1)[:, None, :]
        ea_ref[:, pl.ds(h, 1), :] = jnp.concatenate(ea_words, axis=1)[:, None, :]
        return carry

    lax.fori_loop(0, PEER_HEADS, head, 0)


def _peer_gates(st, peer_tile):
    nscore, t = st.shape
    tt = PEER_GATE_TILE if t % PEER_GATE_TILE == 0 else peer_tile
    assert t % tt == 0 and tt % peer_tile == 0 and peer_tile > LANES and peer_tile % LANES == 0
    rows = jax.ShapeDtypeStruct((N_KEYS, PEER_HEADS, t), U32)
    packed = jax.ShapeDtypeStruct((PEER_HEADS * N_KEYS // 2, t), U32)
    return pl.pallas_call(
        functools.partial(_peer_gate_kernel, peer_tile=peer_tile),
        grid=(t // tt,),
        in_specs=[pl.BlockSpec((nscore, tt), lambda i: (0, i))],
        out_specs=[pl.BlockSpec((N_KEYS, PEER_HEADS, tt), lambda i: (0, 0, i)),
                   pl.BlockSpec((N_KEYS, PEER_HEADS, tt), lambda i: (0, 0, i)),
                   pl.BlockSpec((PEER_HEADS * N_KEYS // 2, tt), lambda i: (0, i)),
                   pl.BlockSpec((PEER_HEADS * N_KEYS // 2, tt), lambda i: (0, i))],
        out_shape=[rows, rows, packed, packed],
        scratch_shapes=[pltpu.VMEM((3 * SUBLANES, tt), F32), pltpu.VMEM((N_KEYS, LANES), F32)],
        compiler_params=_params("arbitrary"),
        name="peer_gates",
    )(st)


def _peer_kernel(cw_ref, ebw_ref, kk_ref, ea_ref, xt_ref, u_ref, vt_ref, x1_ref, g_ref, b_ref, o_ref,
                 ht0_s, ht1_s, wt0_s, wt1_s, acc_s, *, alpha, nblk):
    c = pl.program_id(0)
    tt = xt_ref.shape[1]
    ch = u_ref.shape[0]

    @pl.when(c == 0)
    def _clear():
        ht1_s[...] = jnp.zeros(ht1_s.shape, F32)
        wt0_s[...] = jnp.zeros(wt0_s.shape, BF16)
        wt1_s[...] = jnp.zeros(wt1_s.shape, BF16)
        acc_s[...] = jnp.zeros(acc_s.shape, F32)

    ablocks = ch // N_KEYS
    lt = PEER_LANE_TILE

    def stages(ht_new, ht_old, wt_new, wt_old):
        d = vt_ref.shape[0]
        npiece = ablocks // 2
        nlv = lt // LANES
        nr2 = N_KEYS // (2 * SUBLANES)
        assert nlv == 2
        for ts in range(tt // lt):
            ls = slice(ts * lt, (ts + 1) * lt)
            for unit in range(npiece * nlv):
                ap, tv = unit // nlv, unit % nlv
                piece = unit // 2
                if unit % 2 == 0:
                    rows = slice(piece * (ch // npiece), (piece + 1) * (ch // npiece))
                    lhs_ref, rhs_ref, kdim = u_ref, xt_ref, u_ref.shape[1]
                else:
                    rows = slice(piece * (d // npiece), (piece + 1) * (d // npiece))
                    lhs_ref, rhs_ref, kdim = vt_ref, wt_old, ch
                nk = kdim // MXU_DEPTH
                part = None
                g = ts * nlv + tv
                lv = slice(g * LANES, (g + 1) * LANES)
                ge = (g + 1) % (tt // LANES)
                le = slice(ge * LANES, (ge + 1) * LANES)
                als = (2 * ap, 2 * ap + 1)
                spread = lambda ref, al, h: pltpu.bitcast(jnp.broadcast_to(ref[al, h:h + 1, lv], (SUBLANES, LANES)), BF16)
                kk = [[spread(kk_ref, al, h) for h in range(PEER_HEADS)] for al in als]
                ea = [[spread(ea_ref, al, h) for h in range(PEER_HEADS)] for al in als]
                for r2 in range(nr2):
                    if r2 % (nr2 // nk) == (1 if unit % 2 == 0 else 2):
                        kb = slice(r2 // (nr2 // nk) * MXU_DEPTH, (r2 // (nr2 // nk) + 1) * MXU_DEPTH)
                        dk = _dot(lhs_ref[rows, kb], rhs_ref[kb, ls])
                        part = dk if part is None else part + dk
                    gates = [jnp.zeros((2 * SUBLANES, LANES), BF16) for _ in als]
                    for h in range(PEER_HEADS):
                        wrows = slice(h * (N_KEYS // 2) + r2 * SUBLANES, h * (N_KEYS // 2) + (r2 + 1) * SUBLANES)
                        code = pltpu.bitcast(cw_ref[wrows, lv], BF16)
                        eb = pltpu.bitcast(ebw_ref[wrows, le], BF16)
                        for i in range(2):
                            gates[i] = gates[i] + jnp.where(code >= kk[i][h], eb, jnp.zeros_like(eb)) * ea[i][h]
                    for i, al in enumerate(als):
                        rs = slice(al * N_KEYS + r2 * 2 * SUBLANES, al * N_KEYS + (r2 + 1) * 2 * SUBLANES)
                        wt_new[rs, lv] = gates[i] * _gelu_tanh(ht_old[rs, lv].astype(BF16))
                if unit % 2 == 0:
                    ht_new[rows, ls] = part
                else:
                    acc_s[rows, ls] += part

    @pl.when(c % 2 == 0)
    def _even():
        stages(ht0_s, ht1_s, wt1_s, wt0_s)

    @pl.when(c % 2 == 1)
    def _odd():
        stages(ht1_s, ht0_s, wt0_s, wt1_s)

    @pl.when(jnp.logical_and(c >= PEER_PIPE_FILL, (c - PEER_PIPE_FILL) % nblk == nblk - 1))
    def _finish():
        y = alpha * x1_ref[...] + acc_s[...].T
        o_ref[...] = _layer_norm(y, g_ref[...], b_ref[...])
        acc_s[...] = jnp.zeros(acc_s.shape, F32)


def _peer(cw, ebw, kk, ea, xt, u_bf, vt_bf, x1, g, b, alpha, tt):
    t, d = x1.shape
    ne = u_bf.shape[0]
    ch = PEER_EXPERT_BLOCK
    assert t % tt == 0 and tt % PEER_LANE_TILE == 0 and ne % ch == 0 and ne == N_KEYS * N_KEYS
    assert vt_bf.shape == (ne // ch, d, ch)
    nblk = ne // ch
    ablocks = ch // N_KEYS
    npairs = (t // tt) * nblk
    tile = lambda c, lag: jnp.clip(c - lag, 0, npairs - 1) // nblk
    block = lambda c, lag: jnp.clip(c - lag, 0, npairs - 1) % nblk
    return pl.pallas_call(
        functools.partial(_peer_kernel, alpha=alpha, nblk=nblk),
        grid=(npairs + PEER_PIPE_FILL,),
        in_specs=[
            pl.BlockSpec((PEER_HEADS * N_KEYS // 2, tt), lambda c: (0, tile(c, 1))),
            pl.BlockSpec((PEER_HEADS * N_KEYS // 2, tt), lambda c: (0, tile(c, 1))),
            pl.BlockSpec((ablocks, PEER_HEADS, tt), lambda c: (block(c, 1), 0, tile(c, 1))),
            pl.BlockSpec((ablocks, PEER_HEADS, tt), lambda c: (block(c, 1), 0, tile(c, 1))),
            pl.BlockSpec((d, tt), lambda c: (0, tile(c, 0))),
            pl.BlockSpec((ch, d), lambda c: (block(c, 0), 0)),
            pl.BlockSpec((None, d, ch), lambda c: (block(c, PEER_PIPE_FILL), 0, 0)),
            pl.BlockSpec((tt, d), lambda c: (tile(c, PEER_PIPE_FILL), 0)),
            pl.BlockSpec((1, d), lambda c: (0, 0)),
            pl.BlockSpec((1, d), lambda c: (0, 0)),
        ],
        out_specs=pl.BlockSpec((tt, d), lambda c: (tile(c, PEER_PIPE_FILL), 0)),
        out_shape=jax.ShapeDtypeStruct((t, d), F32),
        scratch_shapes=[
            pltpu.VMEM((ch, tt), F32),
            pltpu.VMEM((ch, tt), F32),
            pltpu.VMEM((ch, tt), BF16),
            pltpu.VMEM((ch, tt), BF16),
            pltpu.VMEM((d, tt), F32),
        ],
        compiler_params=_params("arbitrary"),
        name="peer",
    )(cw, ebw, kk, ea, xt, u_bf, vt_bf, x1, g, b)


def kernel(x_prompt, x_sample, cache_sb_k, cache_sb_v, cache_band_k, cache_band_v, w_in, w_out, gn_a, gn_b,
           rel_bias, ln1_g, ln1_b, peer_query, peer_subkeys, peer_u, peer_v, ln2_g, ln2_b):
    batch, seq, d = x_prompt.shape
    dec_batch, dec_seq, _ = x_sample.shape
    depth = w_in.shape[0]
    past = cache_sb_k.shape[2]
    nb = cache_band_k.shape[2]
    assert w_in.shape[2] == 6 * WIDTH and w_out.shape[1] == 2 * WIDTH
    assert seq >= BAND_PAST and nb == BAND_PAST
    alpha = float((2 * depth) ** 0.25)
    tp = batch * seq
    ts = dec_batch * dec_seq

    xp = x_prompt.reshape(tp, d)
    xs = x_sample.reshape(ts, d)
    row2 = lambda a: a.reshape(1, -1)
    heads = lambda a, n, s: a.reshape(n, s, N_HEADS, HEAD_DIM)
    to_slab = lambda a: jnp.transpose(a, (0, 1, 3, 4, 2)).reshape(a.shape[0], a.shape[1], WIDTH, a.shape[2])
    caches = [to_slab(c) for c in (cache_sb_k, cache_sb_v, cache_band_k, cache_band_v)]
    from_slab = lambda a: jnp.transpose(a.reshape(a.shape[0], N_HEADS, HEAD_DIM, a.shape[2]), (0, 3, 1, 2))
    outs = [[] for _ in range(8)]
    for l in range(depth):
        w_in_bf = w_in[l].astype(BF16)
        woa = w_out[l, :WIDTH].astype(BF16)
        wob = w_out[l, WIDTH:].astype(BF16)
        wq = peer_query[l].astype(BF16)
        sk = peer_subkeys[l].astype(BF16)
        u_bf = peer_u[l].astype(BF16)
        vt_bf = peer_v[l].reshape(-1, PEER_EXPERT_BLOCK, d).transpose(0, 2, 1).astype(BF16)
        ga, gb = row2(gn_a[l]), row2(gn_b[l])
        g1, b1, g2, b2 = row2(ln1_g[l]), row2(ln1_b[l]), row2(ln2_g[l]), row2(ln2_b[l])

        qa, kab, vab, qb, kbb, vbb, sbk, sbv, bdk, bdv = _project_prompt(xp, w_in_bf, batch, seq)
        ca = _sb_prompt(qa, kab, vab, ga, batch, seq)
        cb = _band_prompt(qb, kbb, vbb, _band_prompt_bias(rel_bias[l]), gb, batch, seq)
        x1, x1t, st = _merge(ca, cb, xp, woa, wob, g1, b1, wq, sk, alpha)
        tt = min(PEER_TOKEN_TILE, tp)
        kk, ea, cw, ebw = _peer_gates(st, tt)
        xp = _peer(cw, ebw, kk, ea, x1t, u_bf, vt_bf, x1, g2, b2, alpha, tt)
        outs[0].append(from_slab(sbk))
        outs[1].append(from_slab(sbv))
        outs[2].append(from_slab(bdk))
        outs[3].append(from_slab(bdv))

        qa, ka, va, kab, vab, qb, kb, vb, kbb, vbb = _project(xs, w_in_bf)
        biasp, biasn = _sample_bias(rel_bias[l], dec_seq, nb)
        ca, cb, nbk, nbv = _sample_attn(
            qa, kab, vab, caches[0], caches[1], qb, kbb, vbb, kb, vb, caches[2], caches[3], biasp, biasn, ga, gb, l)
        x1, x1t, st = _merge(ca, cb, xs, woa, wob, g1, b1, wq, sk, alpha)
        tt = min(PEER_TOKEN_TILE, ts)
        kk, ea, cw, ebw = _peer_gates(st, tt)
        xs = _peer(cw, ebw, kk, ea, x1t, u_bf, vt_bf, x1, g2, b2, alpha, tt)
        outs[4].append(heads(ka, dec_batch, dec_seq))
        outs[5].append(heads(va, dec_batch, dec_seq))
        outs[6].append(from_slab(nbk))
        outs[7].append(from_slab(nbv))

    return (xp.reshape(batch, seq, d), xs.reshape(dec_batch, dec_seq, d)) + tuple(jnp.stack(o) for o in outs)
```

```python
import functools

import jax
import jax.numpy as jnp
import numpy as np
from jax import lax
from jax.experimental import pallas as pl
from jax.experimental.pallas import tpu as pltpu

F32 = jnp.float32
BF16 = jnp.bfloat16
U32 = jnp.uint32

HEAD_DIM = 64
N_HEADS = 8
WIDTH = N_HEADS * HEAD_DIM
CHUNK = 64
BAND_CHUNKS = 8
BAND_PAST = BAND_CHUNKS * CHUNK
MAX_REL = 128
N_KEYS = 128
PEER_HEADS = 8
PEER_TOPK = 16
NORM_EPS = 1e-5
NEG_INF = -1e30

LANES = 128
SUBLANES = 8
MXU_DEPTH = 256
VMEM_LIMIT_BYTES = 56 * 1024 * 1024

SB_DEAD_LOG = -104.0
SB_BLOCK = 128
BAND_QBLOCK = 256
BAND_HEAD_GROUP = 4
ROW_TILE = 512
PEER_TOKEN_TILE = 1024
PEER_GATE_TILE = 1024
PEER_EXPERT_BLOCK = 512
PEER_PIPE_FILL = 2
PEER_LANE_TILE = 256
PAD_SCORE = -3.0e38


def _params(*sem):
    return pltpu.CompilerParams(dimension_semantics=sem, vmem_limit_bytes=VMEM_LIMIT_BYTES)


def _nt_dot(a, b):
    return lax.dot_general(a, b, (((1,), (1,)), ((), ())), preferred_element_type=F32)


def _dot(a, b):
    return jnp.dot(a, b, preferred_element_type=F32)


def _layer_norm(y, g, b):
    mu = jnp.mean(y, axis=-1, keepdims=True)
    d = y - mu
    var = jnp.mean(d * d, axis=-1, keepdims=True)
    return d * lax.rsqrt(var + NORM_EPS) * g + b


def _head_rms(o, gain):
    ms = jnp.mean(o * o, axis=-1, keepdims=True)
    return o * lax.rsqrt(ms + NORM_EPS) * gain


def _proj_kernel(x_ref, w_ref, qa_ref, ka_ref, va_ref, kab_ref, vab_ref,
                 qb_ref, kb_ref, vb_ref, kbb_ref, vbb_ref):
    xb = x_ref[...].astype(BF16)
    scale = HEAD_DIM ** -0.5

    def group(g):
        return _dot(xb, w_ref[:, g * WIDTH:(g + 1) * WIDTH])

    qa_ref[...] = (group(0) * scale).astype(BF16)
    k = group(1)
    ka_ref[...] = k
    kab_ref[...] = k.astype(BF16)
    v = group(2)
    va_ref[...] = v
    vab_ref[...] = v.astype(BF16)
    qb_ref[...] = (group(3) * scale).astype(BF16)
    k = group(4)
    kb_ref[...] = k
    kbb_ref[...] = k.astype(BF16)
    v = group(5)
    vb_ref[...] = v
    vbb_ref[...] = v.astype(BF16)


def _project(x, w_bf):
    t, d = x.shape
    tm = min(ROW_TILE, t)
    assert t % tm == 0
    row = lambda i: (i, 0)
    f32o = jax.ShapeDtypeStruct((t, WIDTH), F32)
    bf16o = jax.ShapeDtypeStruct((t, WIDTH), BF16)
    blk = pl.BlockSpec((tm, WIDTH), row)
    return pl.pallas_call(
        _proj_kernel,
        grid=(t // tm,),
        in_specs=[pl.BlockSpec((tm, d), row), pl.BlockSpec(w_bf.shape, lambda i: (0, 0))],
        out_specs=[blk] * 10,
        out_shape=[bf16o, f32o, f32o, bf16o, bf16o, bf16o, f32o, f32o, bf16o, bf16o],
        compiler_params=_params("arbitrary"),
        name="proj",
    )(x, w_bf)


def _proj_prompt_kernel(x_ref, w_ref, qa_ref, kab_ref, vab_ref, qb_ref, kbb_ref, vbb_ref,
                        sbk_ref, sbv_ref, bdk_ref, bdv_ref, *, tiles_per_stream):
    j = pl.program_id(0) % tiles_per_stream
    xb = x_ref[...].astype(BF16)
    scale = HEAD_DIM ** -0.5

    def group(g):
        return _dot(xb, w_ref[:, g * WIDTH:(g + 1) * WIDTH])

    qa_ref[...] = (group(0) * scale).astype(BF16)
    k = group(1)
    kab_ref[...] = k.astype(BF16)
    sbk_ref[0] = k.T
    v = group(2)
    vab_ref[...] = v.astype(BF16)
    sbv_ref[0] = v.T
    qb_ref[...] = (group(3) * scale).astype(BF16)
    kb = group(4)
    kbb_ref[...] = kb.astype(BF16)
    vb = group(5)
    vbb_ref[...] = vb.astype(BF16)

    @pl.when(j == tiles_per_stream - 1)
    def _band_rows():
        bdk_ref[0] = kb.T
        bdv_ref[0] = vb.T


def _project_prompt(x, w_bf, batch, seq):
    t, d = x.shape
    tm = ROW_TILE
    assert seq % tm == 0 and tm == BAND_PAST and t == batch * seq
    nj = seq // tm
    row = lambda i: (i, 0)
    bf16o = jax.ShapeDtypeStruct((t, WIDTH), BF16)
    blk = pl.BlockSpec((tm, WIDTH), row)
    return pl.pallas_call(
        functools.partial(_proj_prompt_kernel, tiles_per_stream=nj),
        grid=(t // tm,),
        in_specs=[pl.BlockSpec((tm, d), row), pl.BlockSpec(w_bf.shape, lambda i: (0, 0))],
        out_specs=[blk] * 6 + [pl.BlockSpec((1, WIDTH, tm), lambda i: (i // nj, 0, i % nj))] * 2
                  + [pl.BlockSpec((1, WIDTH, tm), lambda i: (i // nj, 0, 0))] * 2,
        out_shape=[bf16o] * 6 + [jax.ShapeDtypeStruct((batch, WIDTH, seq), F32)] * 2
                  + [jax.ShapeDtypeStruct((batch, WIDTH, tm), F32)] * 2,
        compiler_params=_params("arbitrary"),
        name="proj_prompt",
    )(x, w_bf)


def _suffix_matrix(kb):
    kp = np.arange(kb)
    m = (kp[:, None] > kp[None, :]).astype(np.float32)
    one = np.concatenate([np.ones((kb, LANES), np.float32), m], axis=1)
    return jnp.asarray(np.concatenate([one, one], axis=0), dtype=BF16)


def _sb_log_terms(z, mask):
    sp = jnp.maximum(z, 0.0) + jnp.log(1.0 + jnp.exp(-jnp.abs(z)))
    log_keep = -sp
    if mask is not None:
        log_keep = jnp.where(mask, log_keep, 0.0)
    hi = log_keep.astype(BF16)
    lo = (log_keep - hi.astype(F32)).astype(BF16)
    return z - sp, jnp.concatenate([hi, lo], axis=1)


def _sb_weights(log_sig, sums, carry, mask):
    kb = log_sig.shape[1]
    log_w = log_sig + sums[:, LANES:LANES + kb]
    if carry is not None:
        log_w = log_w + carry
    w = jnp.exp(log_w)
    if mask is not None:
        w = jnp.where(mask, w, 0.0)
    return w, sums[:, :LANES]


def _sb_prompt_kernel(q_ref, k_ref, v_ref, g_ref, mm_ref, o_ref, carry_s, acc_s):
    i = pl.program_id(1)
    r = q_ref.shape[0]
    row = lax.broadcasted_iota(jnp.int32, (r, r), 0)
    col = lax.broadcasted_iota(jnp.int32, (r, r), 1)
    diag = col < row
    heads = [slice(h * HEAD_DIM, (h + 1) * HEAD_DIM) for h in range(N_HEADS)]

    def key_block(j):
        off = pl.multiple_of(j * r, r)
        q = q_ref[...]
        kj = k_ref[pl.ds(off, r), :]
        vj = v_ref[pl.ds(off, r), :]
        mm = mm_ref[...]
        zs = [_nt_dot(q[:, hs], kj[:, hs]) for hs in heads]
        terms = [_sb_log_terms(z, None) for z in zs]
        sums = [_dot(halves, mm) for _, halves in terms]
        ws = [_sb_weights(terms[h][0], sums[h], carry_s[h], None) for h in range(N_HEADS)]
        pvs = [_dot(ws[h][0].astype(BF16), vj[:, hs]) for h, hs in enumerate(heads)]
        alive = None
        for h in range(N_HEADS):
            carry = carry_s[h] + ws[h][1]
            carry_s[h] = carry
            acc_s[h] = acc_s[h] + pvs[h]
            alive = carry if alive is None else jnp.maximum(alive, carry)
        return jnp.max(alive)

    def first_blocks():
        has_prev = i >= 1
        q = q_ref[...]
        mm = mm_ref[...]
        offs = (pl.multiple_of(i * r, r), pl.multiple_of(jnp.maximum(i - 1, 0) * r, r))
        ks = [k_ref[pl.ds(off, r), :] for off in offs]
        vs = [v_ref[pl.ds(off, r), :] for off in offs]
        masks = (diag, has_prev)
        zs = [[_nt_dot(q[:, hs], kb[:, hs]) for hs in heads] for kb in ks]
        terms = [[_sb_log_terms(z, m) for z in zb] for zb, m in zip(zs, masks)]
        sums = [[_dot(halves, mm) for _, halves in tb] for tb in terms]
        alive = None
        for h, hs in enumerate(heads):
            w_diag, carry = _sb_weights(terms[0][h][0], sums[0][h], None, diag)
            w_prev, total = _sb_weights(terms[1][h][0], sums[1][h], carry, has_prev)
            carry = carry + total
            carry_s[h] = carry
            acc_s[h] = _dot(w_diag.astype(BF16), vs[0][:, hs]) + _dot(w_prev.astype(BF16), vs[1][:, hs])
            alive = carry if alive is None else jnp.maximum(alive, carry)
        return jnp.max(alive)

    def cond(s):
        j, cmax = s
        return jnp.logical_and(j >= 0, cmax > SB_DEAD_LOG)

    def body(s):
        j, _ = s
        return j - 1, key_block(j)

    lax.while_loop(cond, body, (i - 2, first_blocks()))
    outs = [_head_rms(acc_s[h], g_ref[:, hs]) for h, hs in enumerate(heads)]
    o_ref[...] = jnp.concatenate(outs, axis=1).astype(BF16)


def _sb_prompt(q, k, v, gain, batch, seq):
    r = SB_BLOCK
    assert seq % r == 0
    nq = seq // r
    return pl.pallas_call(
        _sb_prompt_kernel,
        grid=(batch, nq),
        in_specs=[
            pl.BlockSpec((r, WIDTH), lambda b, i: (b * nq + i, 0)),
            pl.BlockSpec((seq, WIDTH), lambda b, i: (b, 0)),
            pl.BlockSpec((seq, WIDTH), lambda b, i: (b, 0)),
            pl.BlockSpec((1, WIDTH), lambda b, i: (0, 0)),
            pl.BlockSpec((2 * r, LANES + r), lambda b, i: (0, 0)),
        ],
        out_specs=pl.BlockSpec((r, WIDTH), lambda b, i: (b * nq + i, 0)),
        out_shape=jax.ShapeDtypeStruct((batch * seq, WIDTH), BF16),
        scratch_shapes=[pltpu.VMEM((N_HEADS, r, LANES), F32),
                        pltpu.VMEM((N_HEADS, r, HEAD_DIM), F32)],
        compiler_params=_params("arbitrary", "arbitrary"),
        name="sb_prompt",
    )(q, k, v, gain, _suffix_matrix(r))


def _band_prompt_kernel(q_ref, k0_ref, k1_ref, k2_ref, v0_ref, v1_ref, v2_ref, bias_ref, g_ref, o_ref):
    i = pl.program_id(1)
    k_refs = (k0_ref, k1_ref, k2_ref)
    v_refs = (v0_ref, v1_ref, v2_ref)
    heads = [slice(h * HEAD_DIM, (h + 1) * HEAD_DIM) for h in range(N_HEADS)]
    q = q_ref[...]
    ks = [k_ref[...] for k_ref in k_refs]
    vs = [v_ref[...] for v_ref in v_refs]
    outs = []
    for g0 in range(0, N_HEADS, BAND_HEAD_GROUP):
        group = range(g0, g0 + BAND_HEAD_GROUP)
        scores = []
        for h in group:
            hs = heads[h]
            scs = []
            for w in range(3):
                sc = _nt_dot(q[:, hs], ks[w][:, hs]) + bias_ref[h, w]
                if w < 2:
                    sc = jnp.where(i >= 2 - w, sc, NEG_INF)
                scs.append(sc)
            scores.append(scs)
        weights = []
        for scs in scores:
            m = jnp.max(jnp.maximum(jnp.maximum(scs[0], scs[1]), scs[2]), axis=-1, keepdims=True)
            es = [jnp.exp(sc - m) for sc in scs]
            den = jnp.sum(es[0] + es[1] + es[2], axis=-1, keepdims=True)
            weights.append(([e.astype(BF16) for e in es], den))
        for (es, den), h in zip(weights, group):
            hs = heads[h]
            acc = _dot(es[0], vs[0][:, hs]) + _dot(es[1], vs[1][:, hs]) + _dot(es[2], vs[2][:, hs])
            outs.append(_head_rms(acc / den, g_ref[:, hs]))
    o_ref[...] = jnp.concatenate(outs, axis=1).astype(BF16)


def _toeplitz_bias(rel_bias, n, ncols, offset):
    period = n + ncols + 1
    m = jnp.arange(period)
    shift = jnp.where(m < ncols, m, m - period)
    vec = rel_bias[:, jnp.clip(offset - shift, -MAX_REL, MAX_REL) + MAX_REL].astype(F32)
    rows = jnp.tile(vec, (1, n))[:, :n * (period - 1)].reshape(rel_bias.shape[0], n, period - 1)
    return rows[:, :, :ncols]


def _band_prompt_bias(rel_bias):
    qb = BAND_QBLOCK
    bias = _toeplitz_bias(rel_bias, qb, 3 * qb, 2 * qb)
    kc = jnp.arange(3 * qb)[None, :] // CHUNK - (2 * qb // CHUNK)
    qc = jnp.arange(qb)[:, None] // CHUNK
    valid = (kc <= qc) & (kc >= qc - BAND_CHUNKS)
    bias = jnp.where(valid[None], bias, NEG_INF)
    return bias.reshape(N_HEADS, qb, 3, qb).transpose(0, 2, 1, 3)


def _band_prompt(q, k, v, bias, gain, batch, seq):
    qb = BAND_QBLOCK
    assert seq % qb == 0 and 2 * qb == BAND_PAST and qb % CHUNK == 0
    nq = seq // qb
    kspec = lambda back: pl.BlockSpec((qb, WIDTH), lambda b, i: (b * nq + jnp.maximum(i - back, 0), 0))
    return pl.pallas_call(
        _band_prompt_kernel,
        grid=(batch, nq),
        in_specs=[
            pl.BlockSpec((qb, WIDTH), lambda b, i: (b * nq + i, 0)),
            kspec(2), kspec(1), kspec(0), kspec(2), kspec(1), kspec(0),
            pl.BlockSpec(bias.shape, lambda b, i: (0, 0, 0, 0)),
            pl.BlockSpec((1, WIDTH), lambda b, i: (0, 0)),
        ],
        out_specs=pl.BlockSpec((qb, WIDTH), lambda b, i: (b * nq + i, 0)),
        out_shape=jax.ShapeDtypeStruct((batch * seq, WIDTH), BF16),
        compiler_params=_params("arbitrary", "arbitrary"),
        name="band_prompt",
    )(q, k, k, k, v, v, v, bias, gain)


def _sample_attn_kernel(qa_ref, kan_ref, van_ref, cak_ref, cav_ref,
                        qb_ref, kbn_ref, vbn_ref, kbf_ref, vbf_ref, cbk_ref, cbv_ref,
                        biasp_ref, biasn_ref, ga_ref, gb_ref, mmn_ref, mmp_ref,
                        oa_ref, ob_ref, nbk_ref, nbv_ref):
    sd = qa_ref.shape[0]
    past = cak_ref.shape[2]
    nb = cbk_ref.shape[2]
    row = lax.broadcasted_iota(jnp.int32, (sd, sd), 0)
    col = lax.broadcasted_iota(jnp.int32, (sd, sd), 1)
    diag = col < row
    cak = cak_ref[0].astype(BF16)
    cav = cav_ref[0].astype(BF16)
    cbk = cbk_ref[0].astype(BF16)
    cbv = cbv_ref[0].astype(BF16)
    mmn = mmn_ref[...]
    mmp = mmp_ref[...]
    heads = [slice(h * HEAD_DIM, (h + 1) * HEAD_DIM) for h in range(N_HEADS)]
    blocks = [slice(j * SB_BLOCK, (j + 1) * SB_BLOCK) for j in range(past // SB_BLOCK)]
    qa = qa_ref[...]
    z_new = [_nt_dot(qa[:, hs], kan_ref[:, hs]) for hs in heads]
    z_past = [_dot(qa[:, hs], cak[hs, :]) for hs in heads]
    t_new = [_sb_log_terms(z, diag) for z in z_new]
    t_past = [[_sb_log_terms(z[:, ks], None) for ks in blocks] for z in z_past]
    s_new = [_dot(halves, mmn) for _, halves in t_new]
    s_past = [[_dot(halves, mmp) for _, halves in t] for t in t_past]
    outs_a = []
    for h, hs in enumerate(heads):
        w_new, carry = _sb_weights(t_new[h][0], s_new[h], None, diag)
        w_past = [None] * len(blocks)
        for j in reversed(range(len(blocks))):
            w_past[j], total = _sb_weights(t_past[h][j][0], s_past[h][j], carry, None)
            carry = carry + total
        acc = _dot(w_new.astype(BF16), van_ref[:, hs]) + _nt_dot(jnp.concatenate(w_past, axis=1).astype(BF16), cav[hs, :])
        outs_a.append(_head_rms(acc, ga_ref[:, hs]))
    qb = qb_ref[...]
    sc_p = [_dot(qb[:, hs], cbk[hs, :]) + biasp_ref[h] for h, hs in enumerate(heads)]
    sc_n = [_nt_dot(qb[:, hs], kbn_ref[:, hs]) + biasn_ref[h] for h, hs in enumerate(heads)]
    outs_b = []
    for h, hs in enumerate(heads):
        m = jnp.maximum(jnp.max(sc_p[h], axis=-1, keepdims=True), jnp.max(sc_n[h], axis=-1, keepdims=True))
        ep = jnp.exp(sc_p[h] - m)
        en = jnp.exp(sc_n[h] - m)
        den = jnp.sum(ep, axis=-1, keepdims=True) + jnp.sum(en, axis=-1, keepdims=True)
        acc = _nt_dot(ep.astype(BF16), cbv[hs, :]) + _dot(en.astype(BF16), vbn_ref[:, hs])
        outs_b.append(_head_rms(acc / den, gb_ref[:, hs]))
    oa_ref[...] = jnp.concatenate(outs_a, axis=1).astype(BF16)
    ob_ref[...] = jnp.concatenate(outs_b, axis=1).astype(BF16)
    nbk_ref[0, :, :nb - sd] = cbk_ref[0, :, sd:]
    nbk_ref[0, :, nb - sd:] = kbf_ref[...].T
    nbv_ref[0, :, :nb - sd] = cbv_ref[0, :, sd:]
    nbv_ref[0, :, nb - sd:] = vbf_ref[...].T


def _sample_bias(rel_bias, sd, nb):
    bias = _toeplitz_bias(rel_bias, sd, nb + sd, nb)
    return bias[:, :, :nb], bias[:, :, nb:]


def _sample_attn(qa, kan, van, cak, cav, qb, kbn, vbn, kbf, vbf, cbk, cbv, biasp, biasn, ga, gb, layer):
    _, nbatch, _, past = cak.shape
    nb = cbk.shape[3]
    sd = qa.shape[0] // nbatch
    assert past % SB_BLOCK == 0 and sd % 8 == 0 and sd <= nb and sd <= LANES
    rows = pl.BlockSpec((sd, WIDTH), lambda b: (b, 0))
    cache = lambda n: pl.BlockSpec((None, 1, WIDTH, n), lambda b: (layer, b, 0, 0))
    rolled = pl.BlockSpec((1, WIDTH, nb), lambda b: (b, 0, 0))
    full = lambda a: pl.BlockSpec(a.shape, lambda b: (0,) * a.ndim)
    mmn = _suffix_matrix(sd)
    mmp = _suffix_matrix(SB_BLOCK)
    return pl.pallas_call(
        _sample_attn_kernel,
        grid=(nbatch,),
        in_specs=[rows, rows, rows, cache(past), cache(past),
                  rows, rows, rows, rows, rows, cache(nb), cache(nb),
                  full(biasp), full(biasn), full(ga), full(gb), full(mmn), full(mmp)],
        out_specs=[rows, rows, rolled, rolled],
        out_shape=[jax.ShapeDtypeStruct((nbatch * sd, WIDTH), BF16)] * 2
                  + [jax.ShapeDtypeStruct((nbatch, WIDTH, nb), F32)] * 2,
        compiler_params=_params("arbitrary"),
        name="sample_attn",
    )(qa, kan, van, cak, cav, qb, kbn, vbn, kbf, vbf, cbk, cbv, biasp, biasn, ga, gb, mmn, mmp)


def _merge_kernel(ca_ref, cb_ref, x_ref, woa_ref, wob_ref, g_ref, b_ref, wq_ref, sk_ref,
                  x1_ref, x1t_ref, st_ref, *, alpha):
    mix = _dot(ca_ref[...], woa_ref[...]) + _dot(cb_ref[...], wob_ref[...])
    x1 = _layer_norm(alpha * x_ref[...] + mix, g_ref[...], b_ref[...])
    x1_ref[...] = x1
    x1t_ref[...] = x1.T.astype(BF16)
    qp = _dot(x1.astype(BF16), wq_ref[...])
    half = sk_ref.shape[2]
    for h in range(PEER_HEADS):
        for j in range(2):
            c0 = (2 * h + j) * half
            qh = qp[:, c0:c0 + half].astype(BF16)
            st_ref[(2 * h + j) * N_KEYS:(2 * h + j + 1) * N_KEYS, :] = _nt_dot(sk_ref[j], qh)


def _merge(ca, cb, x, woa, wob, g, b, wq, sk, alpha):
    t, d = x.shape
    tm = min(ROW_TILE, t)
    assert t % tm == 0
    nscore = PEER_HEADS * 2 * N_KEYS
    row = lambda i: (i, 0)
    colb = lambda i: (0, i)
    full = lambda a: pl.BlockSpec(a.shape, lambda i: (0,) * a.ndim)
    return pl.pallas_call(
        functools.partial(_merge_kernel, alpha=alpha),
        grid=(t // tm,),
        in_specs=[pl.BlockSpec((tm, WIDTH), row), pl.BlockSpec((tm, WIDTH), row), pl.BlockSpec((tm, d), row),
                  full(woa), full(wob), full(g), full(b), full(wq), full(sk)],
        out_specs=[pl.BlockSpec((tm, d), row), pl.BlockSpec((d, tm), colb), pl.BlockSpec((nscore, tm), colb)],
        out_shape=[jax.ShapeDtypeStruct((t, d), F32), jax.ShapeDtypeStruct((d, t), BF16),
                   jax.ShapeDtypeStruct((nscore, t), F32)],
        compiler_params=_params("arbitrary"),
        name="merge",
    )(ca, cb, x, woa, wob, g, b, wq, sk)


def _cmp_exchange(v, i, j):
    a, b = v[i], v[j]
    v[i] = jnp.maximum(a, b)
    v[j] = jnp.minimum(a, b)


_SORT16_NETWORK = (
    (0, 13), (1, 12), (2, 15), (3, 14), (4, 8), (5, 6), (7, 11), (9, 10),
    (0, 5), (1, 7), (2, 9), (3, 4), (6, 13), (8, 14), (10, 15), (11, 12),
    (0, 1), (2, 3), (4, 5), (6, 8), (7, 9), (10, 11), (12, 13), (14, 15),
    (0, 2), (1, 3), (4, 10), (5, 11), (6, 7), (8, 9), (12, 14), (13, 15),
    (1, 2), (3, 12), (4, 6), (5, 7), (8, 10), (9, 11), (13, 14),
    (1, 4), (2, 6), (5, 8), (7, 10), (9, 13), (11, 14),
    (2, 4), (3, 6), (9, 12), (11, 13),
    (3, 5), (6, 8), (7, 9), (10, 12),
    (3, 4), (5, 6), (7, 8), (9, 10), (11, 12),
    (6, 7), (8, 9),
)


def _sort16_desc(v):
    v = list(v)
    assert len(v) == PEER_TOPK
    for i, j in _SORT16_NETWORK:
        _cmp_exchange(v, i, j)
    return v


def _count_true_suffix(test, rows):
    sel = jnp.where
    m1 = test(rows[8])
    m2 = test(sel(m1, rows[4], rows[12]))
    m3 = test(sel(m1, sel(m2, rows[2], rows[6]), sel(m2, rows[10], rows[14])))
    m4 = test(sel(m1, sel(m2, sel(m3, rows[1], rows[3]), sel(m3, rows[5], rows[7])),
                  sel(m2, sel(m3, rows[9], rows[11]), sel(m3, rows[13], rows[15]))))
    return sel(m1, 8.0, 0.0) + sel(m2, 4.0, 0.0) + sel(m3, 2.0, 0.0) + sel(m4, 1.0, 0.0) + sel(test(rows[0]), 1.0, 0.0)


def _merge_top16(a, b):
    n = len(a)
    top = [jnp.maximum(a[i], b[n - 1 - i]) for i in range(n)]
    out = [jnp.minimum(a[i], b[n - 1 - i]) for i in range(n)]
    while len(out) > 1:
        out = [jnp.maximum(out[2 * i], out[2 * i + 1]) for i in range(len(out) // 2)]
    j = n // 2
    while j >= 1:
        for i in range(n):
            if (i & j) == 0:
                _cmp_exchange(top, i, i + j)
        j //= 2
    return top, out[0]


def _top16_and_next(vals):
    groups = [_sort16_desc(vals[g:g + PEER_TOPK]) for g in range(0, len(vals), PEER_TOPK)]
    nxt = None
    while len(groups) > 1:
        merged = []
        for g in range(0, len(groups), 2):
            top, left = _merge_top16(groups[g], groups[g + 1])
            merged.append(top)
            nxt = left if nxt is None else jnp.maximum(nxt, left)
        groups = merged
    return groups[0], nxt


def _gelu_tanh(x):
    return 0.5 * x * (1.0 + jnp.tanh(0.7978845608028654 * (x + 0.044715 * (x * x * x))))


def _bf16_bits(x):
    return pltpu.bitcast(x.astype(BF16).astype(F32), U32)


def _pack_row_pairs(x, pair_s):
    half = x.shape[0] // 2
    pair_s[...] = x
    lo = pair_s[pl.ds(0, half, stride=2), :]
    hi = pair_s[pl.ds(1, half, stride=2), :]
    return _bf16_bits(hi) | (_bf16_bits(lo) >> 16)


def _both_halves(x):
    b = _bf16_bits(x)
    return b | (b >> 16)


def _peer_gate_kernel(st_ref, kk_ref, ea_ref, cw_ref, ebw_ref, row_s, pair_s, *, peer_tile):
    tt = st_ref.shape[1]
    ng = tt // LANES
    k1 = PEER_TOPK + 1
    tiles_per_peer = peer_tile // LANES

    def head(h, carry):
        base = pl.multiple_of(h * 2 * N_KEYS, 2 * N_KEYS)
        s1 = st_ref[pl.ds(base, N_KEYS), :]
        s2 = st_ref[pl.ds(base + N_KEYS, N_KEYS), :]
        s1r = s1.reshape(N_KEYS, ng, LANES)
        s2r = s2.reshape(N_KEYS, ng, LANES)
        top_a, next_a = _top16_and_next([s1r[a] for a in range(N_KEYS)])
        top_b, next_b = _top16_and_next([s2r[a] for a in range(N_KEYS)])
        la = top_a + [next_a]
        lb = top_b + [next_b]
        cands = [la[i - 1] + lb[j - 1] for i in range(1, k1 + 1) for j in range(1, k1 + 1) if i * j <= k1]
        pad = jnp.full_like(cands[0], PAD_SCORE)
        cands = cands + [pad] * (-len(cands) % PEER_TOPK)
        top_c, next_c = _top16_and_next(cands)
        tau = 0.5 * (top_c[PEER_TOPK - 1] + next_c)
        den = jnp.ones_like(tau)
        for cv in top_c[1:]:
            den = den + jnp.exp(cv - top_c[0])
        vals = [tau, la[0], 1.0 / den] + lb
        for r, val in enumerate(vals):
            for g in range(ng):
                row_s[r:r + 1, g * LANES:(g + 1) * LANES] = val[g:g + 1, :]
        rows = pl.ds(pl.multiple_of(h * (N_KEYS // 2), N_KEYS // 2), N_KEYS // 2)
        kk_words, ea_words = [], []
        for g in range(ng):
            lg = slice(g * LANES, (g + 1) * LANES)
            s1g = st_ref[pl.ds(base, N_KEYS), lg]
            s2g = st_ref[pl.ds(base + N_KEYS, N_KEYS), lg]
            tau_r = row_s[0:1, lg]
            m1_r = row_s[1:2, lg]
            iz_r = row_s[2:3, lg]
            lb_r = [row_s[3 + j:4 + j, lg] for j in range(k1)]
            code = _count_true_suffix(lambda row: s2g >= row, lb_r[1:]) + jnp.where(s2g >= lb_r[0], 1.0, 0.0)
            th = tau_r - s1g
            n = _count_true_suffix(lambda row: row >= th, lb_r[PEER_TOPK - 1::-1])
            kk_words.append(_both_halves(float(k1 + 1) - n))
            ea_words.append(_both_halves(jnp.exp(s1g - m1_r) * iz_r))
            cw_ref[rows, lg] = _pack_row_pairs(code, pair_s)
            gr = g // tiles_per_peer * tiles_per_peer + (g + 1) % tiles_per_peer
            ebw_ref[rows, gr * LANES:(gr + 1) * LANES] = _pack_row_pairs(jnp.exp(s2g - lb_r[0]), pair_s)
        kk_ref[:, pl.ds(h, 1), :] = jnp.concatenate(kk_words, axis=1)[:, None, :]
        ea_ref[:, pl.ds(h, 1), :] = jnp.concatenate(ea_words, axis=1)[:, None, :]
        return carry

    lax.fori_loop(0, PEER_HEADS, head, 0)


def _peer_gates(st, peer_tile):
    nscore, t = st.shape
    tt = PEER_GATE_TILE if t % PEER_GATE_TILE == 0 else peer_tile
    assert t % tt == 0 and tt % peer_tile == 0 and peer_tile > LANES and peer_tile % LANES == 0
    rows = jax.ShapeDtypeStruct((N_KEYS, PEER_HEADS, t), U32)
    packed = jax.ShapeDtypeStruct((PEER_HEADS * N_KEYS // 2, t), U32)
    return pl.pallas_call(
        functools.partial(_peer_gate_kernel, peer_tile=peer_tile),
        grid=(t // tt,),
        in_specs=[pl.BlockSpec((nscore, tt), lambda i: (0, i))],
        out_specs=[pl.BlockSpec((N_KEYS, PEER_HEADS, tt), lambda i: (0, 0, i)),
                   pl.BlockSpec((N_KEYS, PEER_HEADS, tt), lambda i: (0, 0, i)),
                   pl.BlockSpec((PEER_HEADS * N_KEYS // 2, tt), lambda i: (0, i)),
                   pl.BlockSpec((PEER_HEADS * N_KEYS // 2, tt), lambda i: (0, i))],
        out_shape=[rows, rows, packed, packed],
        scratch_shapes=[pltpu.VMEM((3 * SUBLANES, tt), F32), pltpu.VMEM((N_KEYS, LANES), F32)],
        compiler_params=_params("arbitrary"),
        name="peer_gates",
    )(st)


def _peer_kernel(cw_ref, ebw_ref, kk_ref, ea_ref, xt_ref, u_ref, vt_ref, x1_ref, g_ref, b_ref, o_ref,
                 ht0_s, ht1_s, wt0_s, wt1_s, acc_s, *, alpha, nblk):
    c = pl.program_id(0)
    tt = xt_ref.shape[1]
    ch = u_ref.shape[0]

    @pl.when(c == 0)
    def _clear():
        ht1_s[...] = jnp.zeros(ht1_s.shape, F32)
        wt0_s[...] = jnp.zeros(wt0_s.shape, BF16)
        wt1_s[...] = jnp.zeros(wt1_s.shape, BF16)
        acc_s[...] = jnp.zeros(acc_s.shape, F32)

    ablocks = ch // N_KEYS
    lt = PEER_LANE_TILE

    def stages(ht_new, ht_old, wt_new, wt_old):
        d = vt_ref.shape[0]
        npiece = ablocks // 2
        nlv = lt // LANES
        nr2 = N_KEYS // (2 * SUBLANES)
        assert nlv == 2
        for ts in range(tt // lt):
            ls = slice(ts * lt, (ts + 1) * lt)
            for unit in range(npiece * nlv):
                ap, tv = unit // nlv, unit % nlv
                piece = unit // 2
                if unit % 2 == 0:
                    rows = slice(piece * (ch // npiece), (piece + 1) * (ch // npiece))
                    lhs_ref, rhs_ref, kdim = u_ref, xt_ref, u_ref.shape[1]
                else:
                    rows = slice(piece * (d // npiece), (piece + 1) * (d // npiece))
                    lhs_ref, rhs_ref, kdim = vt_ref, wt_old, ch
                nk = kdim // MXU_DEPTH
                part = None
                g = ts * nlv + tv
                lv = slice(g * LANES, (g + 1) * LANES)
                ge = (g + 1) % (tt // LANES)
                le = slice(ge * LANES, (ge + 1) * LANES)
                als = (2 * ap, 2 * ap + 1)
                spread = lambda ref, al, h: pltpu.bitcast(jnp.broadcast_to(ref[al, h:h + 1, lv], (SUBLANES, LANES)), BF16)
                kk = [[spread(kk_ref, al, h) for h in range(PEER_HEADS)] for al in als]
                ea = [[spread(ea_ref, al, h) for h in range(PEER_HEADS)] for al in als]
                for r2 in range(nr2):
                    if r2 % (nr2 // nk) == (1 if unit % 2 == 0 else 2):
                        kb = slice(r2 // (nr2 // nk) * MXU_DEPTH, (r2 // (nr2 // nk) + 1) * MXU_DEPTH)
                        dk = _dot(lhs_ref[rows, kb], rhs_ref[kb, ls])
                        part = dk if part is None else part + dk
                    gates = [jnp.zeros((2 * SUBLANES, LANES), BF16) for _ in als]
                    for h in range(PEER_HEADS):
                        wrows = slice(h * (N_KEYS // 2) + r2 * SUBLANES, h * (N_KEYS // 2) + (r2 + 1) * SUBLANES)
                        code = pltpu.bitcast(cw_ref[wrows, lv], BF16)
                        eb = pltpu.bitcast(ebw_ref[wrows, le], BF16)
                        for i in range(2):
                            gates[i] = gates[i] + jnp.where(code >= kk[i][h], eb, jnp.zeros_like(eb)) * ea[i][h]
                    for i, al in enumerate(als):
                        rs = slice(al * N_KEYS + r2 * 2 * SUBLANES, al * N_KEYS + (r2 + 1) * 2 * SUBLANES)
                        wt_new[rs, lv] = gates[i] * _gelu_tanh(ht_old[rs, lv].astype(BF16))
                if unit % 2 == 0:
                    ht_new[rows, ls] = part
                else:
                    acc_s[rows, ls] += part

    @pl.when(c % 2 == 0)
    def _even():
        stages(ht0_s, ht1_s, wt1_s, wt0_s)

    @pl.when(c % 2 == 1)
    def _odd():
        stages(ht1_s, ht0_s, wt0_s, wt1_s)

    @pl.when(jnp.logical_and(c >= PEER_PIPE_FILL, (c - PEER_PIPE_FILL) % nblk == nblk - 1))
    def _finish():
        y = alpha * x1_ref[...] + acc_s[...].T
        o_ref[...] = _layer_norm(y, g_ref[...], b_ref[...])
        acc_s[...] = jnp.zeros(acc_s.shape, F32)


def _peer(cw, ebw, kk, ea, xt, u_bf, vt_bf, x1, g, b, alpha, tt):
    t, d = x1.shape
    ne = u_bf.shape[0]
    ch = PEER_EXPERT_BLOCK
    assert t % tt == 0 and tt % PEER_LANE_TILE == 0 and ne % ch == 0 and ne == N_KEYS * N_KEYS
    assert vt_bf.shape == (ne // ch, d, ch)
    nblk = ne // ch
    ablocks = ch // N_KEYS
    npairs = (t // tt) * nblk
    tile = lambda c, lag: jnp.clip(c - lag, 0, npairs - 1) // nblk
    block = lambda c, lag: jnp.clip(c - lag, 0, npairs - 1) % nblk
    return pl.pallas_call(
        functools.partial(_peer_kernel, alpha=alpha, nblk=nblk),
        grid=(npairs + PEER_PIPE_FILL,),
        in_specs=[
            pl.BlockSpec((PEER_HEADS * N_KEYS // 2, tt), lambda c: (0, tile(c, 1))),
            pl.BlockSpec((PEER_HEADS * N_KEYS // 2, tt), lambda c: (0, tile(c, 1))),
            pl.BlockSpec((ablocks, PEER_HEADS, tt), lambda c: (block(c, 1), 0, tile(c, 1))),
            pl.BlockSpec((ablocks, PEER_HEADS, tt), lambda c: (block(c, 1), 0, tile(c, 1))),
            pl.BlockSpec((d, tt), lambda c: (0, tile(c, 0))),
            pl.BlockSpec((ch, d), lambda c: (block(c, 0), 0)),
            pl.BlockSpec((None, d, ch), lambda c: (block(c, PEER_PIPE_FILL), 0, 0)),
            pl.BlockSpec((tt, d), lambda c: (tile(c, PEER_PIPE_FILL), 0)),
            pl.BlockSpec((1, d), lambda c: (0, 0)),
            pl.BlockSpec((1, d), lambda c: (0, 0)),
        ],
        out_specs=pl.BlockSpec((tt, d), lambda c: (tile(c, PEER_PIPE_FILL), 0)),
        out_shape=jax.ShapeDtypeStruct((t, d), F32),
        scratch_shapes=[
            pltpu.VMEM((ch, tt), F32),
            pltpu.VMEM((ch, tt), F32),
            pltpu.VMEM((ch, tt), BF16),
            pltpu.VMEM((ch, tt), BF16),
            pltpu.VMEM((d, tt), F32),
        ],
        compiler_params=_params("arbitrary"),
        name="peer",
    )(cw, ebw, kk, ea, xt, u_bf, vt_bf, x1, g, b)


def kernel(x_prompt, x_sample, cache_sb_k, cache_sb_v, cache_band_k, cache_band_v, w_in, w_out, gn_a, gn_b,
           rel_bias, ln1_g, ln1_b, peer_query, peer_subkeys, peer_u, peer_v, ln2_g, ln2_b):
    batch, seq, d = x_prompt.shape
    dec_batch, dec_seq, _ = x_sample.shape
    depth = w_in.shape[0]
    past = cache_sb_k.shape[2]
    nb = cache_band_k.shape[2]
    assert w_in.shape[2] == 6 * WIDTH and w_out.shape[1] == 2 * WIDTH
    assert seq >= BAND_PAST and nb == BAND_PAST
    alpha = float((2 * depth) ** 0.25)
    tp = batch * seq
    ts = dec_batch * dec_seq

    xp = x_prompt.reshape(tp, d)
    xs = x_sample.reshape(ts, d)
    row2 = lambda a: a.reshape(1, -1)
    heads = lambda a, n, s: a.reshape(n, s, N_HEADS, HEAD_DIM)
    to_slab = lambda a: jnp.transpose(a, (0, 1, 3, 4, 2)).reshape(a.shape[0], a.shape[1], WIDTH, a.shape[2])
    caches = [to_slab(c) for c in (cache_sb_k, cache_sb_v, cache_band_k, cache_band_v)]
    from_slab = lambda a: jnp.transpose(a.reshape(a.shape[0], N_HEADS, HEAD_DIM, a.shape[2]), (0, 3, 1, 2))
    outs = [[] for _ in range(8)]
    for l in range(depth):
        w_in_bf = w_in[l].astype(BF16)
        woa = w_out[l, :WIDTH].astype(BF16)
        wob = w_out[l, WIDTH:].astype(BF16)
        wq = peer_query[l].astype(BF16)
        sk = peer_subkeys[l].astype(BF16)
        u_bf = peer_u[l].astype(BF16)
        vt_bf = peer_v[l].reshape(-1, PEER_EXPERT_BLOCK, d).transpose(0, 2, 1).astype(BF16)
        ga, gb = row2(gn_a[l]), row2(gn_b[l])
        g1, b1, g2, b2 = row2(ln1_g[l]), row2(ln1_b[l]), row2(ln2_g[l]), row2(ln2_b[l])

        qa, kab, vab, qb, kbb, vbb, sbk, sbv, bdk, bdv = _project_prompt(xp, w_in_bf, batch, seq)
        ca = _sb_prompt(qa, kab, vab, ga, batch, seq)
        cb = _band_prompt(qb, kbb, vbb, _band_prompt_bias(rel_bias[l]), gb, batch, seq)
        x1, x1t, st = _merge(ca, cb, xp, woa, wob, g1, b1, wq, sk, alpha)
        tt = min(PEER_TOKEN_TILE, tp)
        kk, ea, cw, ebw = _peer_gates(st, tt)
        xp = _peer(cw, ebw, kk, ea, x1t, u_bf, vt_bf, x1, g2, b2, alpha, tt)
        outs[0].append(from_slab(sbk))
        outs[1].append(from_slab(sbv))
        outs[2].append(from_slab(bdk))
        outs[3].append(from_slab(bdv))

        qa, ka, va, kab, vab, qb, kb, vb, kbb, vbb = _project(xs, w_in_bf)
        biasp, biasn = _sample_bias(rel_bias[l], dec_seq, nb)
        ca, cb, nbk, nbv = _sample_attn(
            qa, kab, vab, caches[0], caches[1], qb, kbb, vbb, kb, vb, caches[2], caches[3], biasp, biasn, ga, gb, l)
        x1, x1t, st = _merge(ca, cb, xs, woa, wob, g1, b1, wq, sk, alpha)
        tt = min(PEER_TOKEN_TILE, ts)
        kk, ea, cw, ebw = _peer_gates(st, tt)
        xs = _peer(cw, ebw, kk, ea, x1t, u_bf, vt_bf, x1, g2, b2, alpha, tt)
        outs[4].append(heads(ka, dec_batch, dec_seq))
        outs[5].append(heads(va, dec_batch, dec_seq))
        outs[6].append(from_slab(nbk))
        outs[7].append(from_slab(nbv))

    return (xp.reshape(batch, seq, d), xs.reshape(dec_batch, dec_seq, d)) + tuple(jnp.stack(o) for o in outs)
```

```python
import functools

import jax
import jax.numpy as jnp
import numpy as np
from jax import lax
from jax.experimental import pallas as pl
from jax.experimental.pallas import tpu as pltpu

F32 = jnp.float32
BF16 = jnp.bfloat16
U32 = jnp.uint32

HEAD_DIM = 64
N_HEADS = 8
WIDTH = N_HEADS * HEAD_DIM
CHUNK = 64
BAND_CHUNKS = 8
BAND_PAST = BAND_CHUNKS * CHUNK
MAX_REL = 128
N_KEYS = 128
PEER_HEADS = 8
PEER_TOPK = 16
NORM_EPS = 1e-5
NEG_INF = -1e30

LANES = 128
SUBLANES = 8
MXU_DEPTH = 256
VMEM_LIMIT_BYTES = 56 * 1024 * 1024

SB_DEAD_LOG = -104.0
SB_BLOCK = 128
SB_FIRST_BLOCKS = 3
BAND_QBLOCK = 256
BAND_HEAD_GROUP = 4
ROW_TILE = 512
PEER_TOKEN_TILE = 1024
PEER_GATE_TILE = 1024
PEER_EXPERT_BLOCK = 512
PEER_PIPE_FILL = 2
PEER_LANE_TILE = 256
PAD_SCORE = -3.0e38


def _params(*sem):
    return pltpu.CompilerParams(dimension_semantics=sem, vmem_limit_bytes=VMEM_LIMIT_BYTES)


def _nt_dot(a, b):
    return lax.dot_general(a, b, (((1,), (1,)), ((), ())), preferred_element_type=F32)


def _dot(a, b):
    return jnp.dot(a, b, preferred_element_type=F32)


def _layer_norm(y, g, b):
    mu = jnp.mean(y, axis=-1, keepdims=True)
    d = y - mu
    var = jnp.mean(d * d, axis=-1, keepdims=True)
    return d * lax.rsqrt(var + NORM_EPS) * g + b


def _head_rms(o, gain):
    ms = jnp.mean(o * o, axis=-1, keepdims=True)
    return o * lax.rsqrt(ms + NORM_EPS) * gain


def _proj_kernel(x_ref, w_ref, qa_ref, ka_ref, va_ref, kab_ref, vab_ref,
                 qb_ref, kb_ref, vb_ref, kbb_ref, vbb_ref):
    xb = x_ref[...].astype(BF16)
    scale = HEAD_DIM ** -0.5

    def group(g):
        return _dot(xb, w_ref[:, g * WIDTH:(g + 1) * WIDTH])

    qa_ref[...] = (group(0) * scale).astype(BF16)
    k = group(1)
    ka_ref[...] = k
    kab_ref[...] = k.astype(BF16)
    v = group(2)
    va_ref[...] = v
    vab_ref[...] = v.astype(BF16)
    qb_ref[...] = (group(3) * scale).astype(BF16)
    k = group(4)
    kb_ref[...] = k
    kbb_ref[...] = k.astype(BF16)
    v = group(5)
    vb_ref[...] = v
    vbb_ref[...] = v.astype(BF16)


def _project(x, w_bf):
    t, d = x.shape
    tm = min(ROW_TILE, t)
    assert t % tm == 0
    row = lambda i: (i, 0)
    f32o = jax.ShapeDtypeStruct((t, WIDTH), F32)
    bf16o = jax.ShapeDtypeStruct((t, WIDTH), BF16)
    blk = pl.BlockSpec((tm, WIDTH), row)
    return pl.pallas_call(
        _proj_kernel,
        grid=(t // tm,),
        in_specs=[pl.BlockSpec((tm, d), row), pl.BlockSpec(w_bf.shape, lambda i: (0, 0))],
        out_specs=[blk] * 10,
        out_shape=[bf16o, f32o, f32o, bf16o, bf16o, bf16o, f32o, f32o, bf16o, bf16o],
        compiler_params=_params("arbitrary"),
        name="proj",
    )(x, w_bf)


def _proj_prompt_kernel(x_ref, w_ref, qa_ref, kab_ref, vab_ref, qb_ref, kbb_ref, vbb_ref,
                        sbk_ref, sbv_ref, bdk_ref, bdv_ref, *, tiles_per_stream):
    j = pl.program_id(0) % tiles_per_stream
    xb = x_ref[...].astype(BF16)
    scale = HEAD_DIM ** -0.5

    def group(g):
        return _dot(xb, w_ref[:, g * WIDTH:(g + 1) * WIDTH])

    qa_ref[...] = (group(0) * scale).astype(BF16)
    k = group(1)
    kab_ref[...] = k.astype(BF16)
    sbk_ref[0] = k.T
    v = group(2)
    vab_ref[...] = v.astype(BF16)
    sbv_ref[0] = v.T
    qb_ref[...] = (group(3) * scale).astype(BF16)
    kb = group(4)
    kbb_ref[...] = kb.astype(BF16)
    vb = group(5)
    vbb_ref[...] = vb.astype(BF16)

    @pl.when(j == tiles_per_stream - 1)
    def _band_rows():
        bdk_ref[0] = kb.T
        bdv_ref[0] = vb.T


def _project_prompt(x, w_bf, batch, seq):
    t, d = x.shape
    tm = ROW_TILE
    assert seq % tm == 0 and tm == BAND_PAST and t == batch * seq
    nj = seq // tm
    row = lambda i: (i, 0)
    bf16o = jax.ShapeDtypeStruct((t, WIDTH), BF16)
    blk = pl.BlockSpec((tm, WIDTH), row)
    return pl.pallas_call(
        functools.partial(_proj_prompt_kernel, tiles_per_stream=nj),
        grid=(t // tm,),
        in_specs=[pl.BlockSpec((tm, d), row), pl.BlockSpec(w_bf.shape, lambda i: (0, 0))],
        out_specs=[blk] * 6 + [pl.BlockSpec((1, WIDTH, tm), lambda i: (i // nj, 0, i % nj))] * 2
                  + [pl.BlockSpec((1, WIDTH, tm), lambda i: (i // nj, 0, 0))] * 2,
        out_shape=[bf16o] * 6 + [jax.ShapeDtypeStruct((batch, WIDTH, seq), F32)] * 2
                  + [jax.ShapeDtypeStruct((batch, WIDTH, tm), F32)] * 2,
        compiler_params=_params("arbitrary"),
        name="proj_prompt",
    )(x, w_bf)


def _suffix_matrix(kb):
    kp = np.arange(kb)
    m = (kp[:, None] > kp[None, :]).astype(np.float32)
    one = np.concatenate([np.ones((kb, LANES), np.float32), m], axis=1)
    return jnp.asarray(np.concatenate([one, one], axis=0), dtype=BF16)


def _sb_log_terms(z, mask):
    sp = jnp.maximum(z, 0.0) + jnp.log(1.0 + jnp.exp(-jnp.abs(z)))
    log_keep = -sp
    if mask is not None:
        log_keep = jnp.where(mask, log_keep, 0.0)
    hi = log_keep.astype(BF16)
    lo = (log_keep - hi.astype(F32)).astype(BF16)
    return z - sp, jnp.concatenate([hi, lo], axis=1)


def _sb_weights(log_sig, sums, carry, mask):
    kb = log_sig.shape[1]
    log_w = log_sig + sums[:, LANES:LANES + kb]
    if carry is not None:
        log_w = log_w + carry
    w = jnp.exp(log_w)
    if mask is not None:
        w = jnp.where(mask, w, 0.0)
    return w, sums[:, :LANES]


def _sb_prompt_kernel(q_ref, k_ref, v_ref, g_ref, mm_ref, o_ref, carry_s, acc_s):
    i = pl.program_id(1)
    r = q_ref.shape[0]
    row = lax.broadcasted_iota(jnp.int32, (r, r), 0)
    col = lax.broadcasted_iota(jnp.int32, (r, r), 1)
    diag = col < row
    heads = [slice(h * HEAD_DIM, (h + 1) * HEAD_DIM) for h in range(N_HEADS)]

    def key_block(j):
        off = pl.multiple_of(j * r, r)
        q = q_ref[...]
        kj = k_ref[pl.ds(off, r), :]
        vj = v_ref[pl.ds(off, r), :]
        mm = mm_ref[...]
        zs = [_nt_dot(q[:, hs], kj[:, hs]) for hs in heads]
        terms = [_sb_log_terms(z, None) for z in zs]
        sums = [_dot(halves, mm) for _, halves in terms]
        ws = [_sb_weights(terms[h][0], sums[h], carry_s[h], None) for h in range(N_HEADS)]
        pvs = [_dot(ws[h][0].astype(BF16), vj[:, hs]) for h, hs in enumerate(heads)]
        alive = None
        for h in range(N_HEADS):
            carry = carry_s[h] + ws[h][1]
            carry_s[h] = carry
            acc_s[h] = acc_s[h] + pvs[h]
            alive = carry if alive is None else jnp.maximum(alive, carry)
        return jnp.max(alive)

    def first_blocks():
        q = q_ref[...]
        mm = mm_ref[...]
        offs = [pl.multiple_of(jnp.maximum(i - b, 0) * r, r) for b in range(SB_FIRST_BLOCKS)]
        masks = [diag] + [i >= b for b in range(1, SB_FIRST_BLOCKS)]
        ks = [k_ref[pl.ds(off, r), :] for off in offs]
        vs = [v_ref[pl.ds(off, r), :] for off in offs]
        zs = [[_nt_dot(q[:, hs], kb[:, hs]) for hs in heads] for kb in ks]
        terms = [[_sb_log_terms(z, m) for z in zb] for zb, m in zip(zs, masks)]
        sums = [[_dot(halves, mm) for _, halves in tb] for tb in terms]
        alive = None
        for h, hs in enumerate(heads):
            carry, acc = None, None
            for b in range(SB_FIRST_BLOCKS):
                w, total = _sb_weights(terms[b][h][0], sums[b][h], carry, masks[b])
                pv = _dot(w.astype(BF16), vs[b][:, hs])
                carry = total if carry is None else carry + total
                acc = pv if acc is None else acc + pv
            carry_s[h] = carry
            acc_s[h] = acc
            alive = carry if alive is None else jnp.maximum(alive, carry)
        return jnp.max(alive)

    def cond(s):
        j, cmax = s
        return jnp.logical_and(j >= 0, cmax > SB_DEAD_LOG)

    def body(s):
        j, _ = s
        return j - 1, key_block(j)

    lax.while_loop(cond, body, (i - SB_FIRST_BLOCKS, first_blocks()))
    outs = [_head_rms(acc_s[h], g_ref[:, hs]) for h, hs in enumerate(heads)]
    o_ref[...] = jnp.concatenate(outs, axis=1).astype(BF16)


def _sb_prompt(q, k, v, gain, batch, seq):
    r = SB_BLOCK
    assert seq % r == 0
    nq = seq // r
    return pl.pallas_call(
        _sb_prompt_kernel,
        grid=(batch, nq),
        in_specs=[
            pl.BlockSpec((r, WIDTH), lambda b, i: (b * nq + i, 0)),
            pl.BlockSpec((seq, WIDTH), lambda b, i: (b, 0)),
            pl.BlockSpec((seq, WIDTH), lambda b, i: (b, 0)),
            pl.BlockSpec((1, WIDTH), lambda b, i: (0, 0)),
            pl.BlockSpec((2 * r, LANES + r), lambda b, i: (0, 0)),
        ],
        out_specs=pl.BlockSpec((r, WIDTH), lambda b, i: (b * nq + i, 0)),
        out_shape=jax.ShapeDtypeStruct((batch * seq, WIDTH), BF16),
        scratch_shapes=[pltpu.VMEM((N_HEADS, r, LANES), F32),
                        pltpu.VMEM((N_HEADS, r, HEAD_DIM), F32)],
        compiler_params=_params("arbitrary", "arbitrary"),
        name="sb_prompt",
    )(q, k, v, gain, _suffix_matrix(r))


def _band_prompt_kernel(q_ref, k0_ref, k1_ref, k2_ref, v0_ref, v1_ref, v2_ref, bias_ref, g_ref, o_ref):
    i = pl.program_id(1)
    k_refs = (k0_ref, k1_ref, k2_ref)
    v_refs = (v0_ref, v1_ref, v2_ref)
    heads = [slice(h * HEAD_DIM, (h + 1) * HEAD_DIM) for h in range(N_HEADS)]
    q = q_ref[...]
    ks = [k_ref[...] for k_ref in k_refs]
    vs = [v_ref[...] for v_ref in v_refs]
    outs = []
    for g0 in range(0, N_HEADS, BAND_HEAD_GROUP):
        group = range(g0, g0 + BAND_HEAD_GROUP)
        scores = []
        for h in group:
            hs = heads[h]
            scs = []
            for w in range(3):
                sc = _nt_dot(q[:, hs], ks[w][:, hs]) + bias_ref[h, w]
                if w < 2:
                    sc = jnp.where(i >= 2 - w, sc, NEG_INF)
                scs.append(sc)
            scores.append(scs)
        weights = []
        for scs in scores:
            m = jnp.max(jnp.maximum(jnp.maximum(scs[0], scs[1]), scs[2]), axis=-1, keepdims=True)
            es = [jnp.exp(sc - m) for sc in scs]
            den = jnp.sum(es[0] + es[1] + es[2], axis=-1, keepdims=True)
            weights.append(([e.astype(BF16) for e in es], den))
        for (es, den), h in zip(weights, group):
            hs = heads[h]
            acc = _dot(es[0], vs[0][:, hs]) + _dot(es[1], vs[1][:, hs]) + _dot(es[2], vs[2][:, hs])
            outs.append(_head_rms(acc / den, g_ref[:, hs]))
    o_ref[...] = jnp.concatenate(outs, axis=1).astype(BF16)


def _toeplitz_bias(rel_bias, n, ncols, offset):
    period = n + ncols + 1
    m = jnp.arange(period)
    shift = jnp.where(m < ncols, m, m - period)
    vec = rel_bias[:, jnp.clip(offset - shift, -MAX_REL, MAX_REL) + MAX_REL].astype(F32)
    rows = jnp.tile(vec, (1, n))[:, :n * (period - 1)].reshape(rel_bias.shape[0], n, period - 1)
    return rows[:, :, :ncols]


def _band_prompt_bias(rel_bias):
    qb = BAND_QBLOCK
    bias = _toeplitz_bias(rel_bias, qb, 3 * qb, 2 * qb)
    kc = jnp.arange(3 * qb)[None, :] // CHUNK - (2 * qb // CHUNK)
    qc = jnp.arange(qb)[:, None] // CHUNK
    valid = (kc <= qc) & (kc >= qc - BAND_CHUNKS)
    bias = jnp.where(valid[None], bias, NEG_INF)
    return bias.reshape(N_HEADS, qb, 3, qb).transpose(0, 2, 1, 3)


def _band_prompt(q, k, v, bias, gain, batch, seq):
    qb = BAND_QBLOCK
    assert seq % qb == 0 and 2 * qb == BAND_PAST and qb % CHUNK == 0
    nq = seq // qb
    kspec = lambda back: pl.BlockSpec((qb, WIDTH), lambda b, i: (b * nq + jnp.maximum(i - back, 0), 0))
    return pl.pallas_call(
        _band_prompt_kernel,
        grid=(batch, nq),
        in_specs=[
            pl.BlockSpec((qb, WIDTH), lambda b, i: (b * nq + i, 0)),
            kspec(2), kspec(1), kspec(0), kspec(2), kspec(1), kspec(0),
            pl.BlockSpec(bias.shape, lambda b, i: (0, 0, 0, 0)),
            pl.BlockSpec((1, WIDTH), lambda b, i: (0, 0)),
        ],
        out_specs=pl.BlockSpec((qb, WIDTH), lambda b, i: (b * nq + i, 0)),
        out_shape=jax.ShapeDtypeStruct((batch * seq, WIDTH), BF16),
        compiler_params=_params("arbitrary", "arbitrary"),
        name="band_prompt",
    )(q, k, k, k, v, v, v, bias, gain)


def _sample_attn_kernel(qa_ref, kan_ref, van_ref, cak_ref, cav_ref,
                        qb_ref, kbn_ref, vbn_ref, kbf_ref, vbf_ref, cbk_ref, cbv_ref,
                        biasp_ref, biasn_ref, ga_ref, gb_ref, mmn_ref, mmp_ref,
                        oa_ref, ob_ref, nbk_ref, nbv_ref):
    sd = qa_ref.shape[0]
    past = cak_ref.shape[2]
    nb = cbk_ref.shape[2]
    row = lax.broadcasted_iota(jnp.int32, (sd, sd), 0)
    col = lax.broadcasted_iota(jnp.int32, (sd, sd), 1)
    diag = col < row
    cak = cak_ref[0].astype(BF16)
    cav = cav_ref[0].astype(BF16)
    cbk = cbk_ref[0].astype(BF16)
    cbv = cbv_ref[0].astype(BF16)
    mmn = mmn_ref[...]
    mmp = mmp_ref[...]
    heads = [slice(h * HEAD_DIM, (h + 1) * HEAD_DIM) for h in range(N_HEADS)]
    blocks = [slice(j * SB_BLOCK, (j + 1) * SB_BLOCK) for j in range(past // SB_BLOCK)]
    qa = qa_ref[...]
    z_new = [_nt_dot(qa[:, hs], kan_ref[:, hs]) for hs in heads]
    z_past = [_dot(qa[:, hs], cak[hs, :]) for hs in heads]
    t_new = [_sb_log_terms(z, diag) for z in z_new]
    t_past = [[_sb_log_terms(z[:, ks], None) for ks in blocks] for z in z_past]
    s_new = [_dot(halves, mmn) for _, halves in t_new]
    s_past = [[_dot(halves, mmp) for _, halves in t] for t in t_past]
    outs_a = []
    for h, hs in enumerate(heads):
        w_new, carry = _sb_weights(t_new[h][0], s_new[h], None, diag)
        w_past = [None] * len(blocks)
        for j in reversed(range(len(blocks))):
            w_past[j], total = _sb_weights(t_past[h][j][0], s_past[h][j], carry, None)
            carry = carry + total
        acc = _dot(w_new.astype(BF16), van_ref[:, hs]) + _nt_dot(jnp.concatenate(w_past, axis=1).astype(BF16), cav[hs, :])
        outs_a.append(_head_rms(acc, ga_ref[:, hs]))
    qb = qb_ref[...]
    sc_p = [_dot(qb[:, hs], cbk[hs, :]) + biasp_ref[h] for h, hs in enumerate(heads)]
    sc_n = [_nt_dot(qb[:, hs], kbn_ref[:, hs]) + biasn_ref[h] for h, hs in enumerate(heads)]
    outs_b = []
    for h, hs in enumerate(heads):
        m = jnp.maximum(jnp.max(sc_p[h], axis=-1, keepdims=True), jnp.max(sc_n[h], axis=-1, keepdims=True))
        ep = jnp.exp(sc_p[h] - m)
        en = jnp.exp(sc_n[h] - m)
        den = jnp.sum(ep, axis=-1, keepdims=True) + jnp.sum(en, axis=-1, keepdims=True)
        acc = _nt_dot(ep.astype(BF16), cbv[hs, :]) + _dot(en.astype(BF16), vbn_ref[:, hs])
        outs_b.append(_head_rms(acc / den, gb_ref[:, hs]))
    oa_ref[...] = jnp.concatenate(outs_a, axis=1).astype(BF16)
    ob_ref[...] = jnp.concatenate(outs_b, axis=1).astype(BF16)
    nbk_ref[0, :, :nb - sd] = cbk_ref[0, :, sd:]
    nbk_ref[0, :, nb - sd:] = kbf_ref[...].T
    nbv_ref[0, :, :nb - sd] = cbv_ref[0, :, sd:]
    nbv_ref[0, :, nb - sd:] = vbf_ref[...].T


def _sample_bias(rel_bias, sd, nb):
    bias = _toeplitz_bias(rel_bias, sd, nb + sd, nb)
    return bias[:, :, :nb], bias[:, :, nb:]


def _sample_attn(qa, kan, van, cak, cav, qb, kbn, vbn, kbf, vbf, cbk, cbv, biasp, biasn, ga, gb, layer):
    _, nbatch, _, past = cak.shape
    nb = cbk.shape[3]
    sd = qa.shape[0] // nbatch
    assert past % SB_BLOCK == 0 and sd % 8 == 0 and sd <= nb and sd <= LANES
    rows = pl.BlockSpec((sd, WIDTH), lambda b: (b, 0))
    cache = lambda n: pl.BlockSpec((None, 1, WIDTH, n), lambda b: (layer, b, 0, 0))
    rolled = pl.BlockSpec((1, WIDTH, nb), lambda b: (b, 0, 0))
    full = lambda a: pl.BlockSpec(a.shape, lambda b: (0,) * a.ndim)
    mmn = _suffix_matrix(sd)
    mmp = _suffix_matrix(SB_BLOCK)
    return pl.pallas_call(
        _sample_attn_kernel,
        grid=(nbatch,),
        in_specs=[rows, rows, rows, cache(past), cache(past),
                  rows, rows, rows, rows, rows, cache(nb), cache(nb),
                  full(biasp), full(biasn), full(ga), full(gb), full(mmn), full(mmp)],
        out_specs=[rows, rows, rolled, rolled],
        out_shape=[jax.ShapeDtypeStruct((nbatch * sd, WIDTH), BF16)] * 2
                  + [jax.ShapeDtypeStruct((nbatch, WIDTH, nb), F32)] * 2,
        compiler_params=_params("arbitrary"),
        name="sample_attn",
    )(qa, kan, van, cak, cav, qb, kbn, vbn, kbf, vbf, cbk, cbv, biasp, biasn, ga, gb, mmn, mmp)


def _merge_kernel(ca_ref, cb_ref, x_ref, woa_ref, wob_ref, g_ref, b_ref, wq_ref, sk_ref,
                  x1_ref, x1t_ref, st_ref, *, alpha):
    mix = _dot(ca_ref[...], woa_ref[...]) + _dot(cb_ref[...], wob_ref[...])
    x1 = _layer_norm(alpha * x_ref[...] + mix, g_ref[...], b_ref[...])
    x1_ref[...] = x1
    x1t_ref[...] = x1.T.astype(BF16)
    qp = _dot(x1.astype(BF16), wq_ref[...])
    half = sk_ref.shape[2]
    for h in range(PEER_HEADS):
        for j in range(2):
            c0 = (2 * h + j) * half
            qh = qp[:, c0:c0 + half].astype(BF16)
            st_ref[(2 * h + j) * N_KEYS:(2 * h + j + 1) * N_KEYS, :] = _nt_dot(sk_ref[j], qh)


def _merge(ca, cb, x, woa, wob, g, b, wq, sk, alpha):
    t, d = x.shape
    tm = min(ROW_TILE, t)
    assert t % tm == 0
    nscore = PEER_HEADS * 2 * N_KEYS
    row = lambda i: (i, 0)
    colb = lambda i: (0, i)
    full = lambda a: pl.BlockSpec(a.shape, lambda i: (0,) * a.ndim)
    return pl.pallas_call(
        functools.partial(_merge_kernel, alpha=alpha),
        grid=(t // tm,),
        in_specs=[pl.BlockSpec((tm, WIDTH), row), pl.BlockSpec((tm, WIDTH), row), pl.BlockSpec((tm, d), row),
                  full(woa), full(wob), full(g), full(b), full(wq), full(sk)],
        out_specs=[pl.BlockSpec((tm, d), row), pl.BlockSpec((d, tm), colb), pl.BlockSpec((nscore, tm), colb)],
        out_shape=[jax.ShapeDtypeStruct((t, d), F32), jax.ShapeDtypeStruct((d, t), BF16),
                   jax.ShapeDtypeStruct((nscore, t), F32)],
        compiler_params=_params("arbitrary"),
        name="merge",
    )(ca, cb, x, woa, wob, g, b, wq, sk)


def _cmp_exchange(v, i, j):
    a, b = v[i], v[j]
    v[i] = jnp.maximum(a, b)
    v[j] = jnp.minimum(a, b)


_SORT16_NETWORK = (
    (0, 13), (1, 12), (2, 15), (3, 14), (4, 8), (5, 6), (7, 11), (9, 10),
    (0, 5), (1, 7), (2, 9), (3, 4), (6, 13), (8, 14), (10, 15), (11, 12),
    (0, 1), (2, 3), (4, 5), (6, 8), (7, 9), (10, 11), (12, 13), (14, 15),
    (0, 2), (1, 3), (4, 10), (5, 11), (6, 7), (8, 9), (12, 14), (13, 15),
    (1, 2), (3, 12), (4, 6), (5, 7), (8, 10), (9, 11), (13, 14),
    (1, 4), (2, 6), (5, 8), (7, 10), (9, 13), (11, 14),
    (2, 4), (3, 6), (9, 12), (11, 13),
    (3, 5), (6, 8), (7, 9), (10, 12),
    (3, 4), (5, 6), (7, 8), (9, 10), (11, 12),
    (6, 7), (8, 9),
)


def _sort16_desc(v):
    v = list(v)
    assert len(v) == PEER_TOPK
    for i, j in _SORT16_NETWORK:
        _cmp_exchange(v, i, j)
    return v


def _count_true_suffix(test, rows):
    sel = jnp.where
    m1 = test(rows[8])
    m2 = test(sel(m1, rows[4], rows[12]))
    m3 = test(sel(m1, sel(m2, rows[2], rows[6]), sel(m2, rows[10], rows[14])))
    m4 = test(sel(m1, sel(m2, sel(m3, rows[1], rows[3]), sel(m3, rows[5], rows[7])),
                  sel(m2, sel(m3, rows[9], rows[11]), sel(m3, rows[13], rows[15]))))
    return sel(m1, 8.0, 0.0) + sel(m2, 4.0, 0.0) + sel(m3, 2.0, 0.0) + sel(m4, 1.0, 0.0) + sel(test(rows[0]), 1.0, 0.0)


def _merge_top16(a, b):
    n = len(a)
    top = [jnp.maximum(a[i], b[n - 1 - i]) for i in range(n)]
    out = [jnp.minimum(a[i], b[n - 1 - i]) for i in range(n)]
    while len(out) > 1:
        out = [jnp.maximum(out[2 * i], out[2 * i + 1]) for i in range(len(out) // 2)]
    j = n // 2
    while j >= 1:
        for i in range(n):
            if (i & j) == 0:
                _cmp_exchange(top, i, i + j)
        j //= 2
    return top, out[0]


def _top16_and_next(vals):
    groups = [_sort16_desc(vals[g:g + PEER_TOPK]) for g in range(0, len(vals), PEER_TOPK)]
    nxt = None
    while len(groups) > 1:
        merged = []
        for g in range(0, len(groups), 2):
            top, left = _merge_top16(groups[g], groups[g + 1])
            merged.append(top)
            nxt = left if nxt is None else jnp.maximum(nxt, left)
        groups = merged
    return groups[0], nxt


def _gelu_tanh(x):
    return 0.5 * x * (1.0 + jnp.tanh(0.7978845608028654 * (x + 0.044715 * (x * x * x))))


def _bf16_bits(x):
    return pltpu.bitcast(x.astype(BF16).astype(F32), U32)


def _pack_row_pairs(x, pair_s):
    half = x.shape[0] // 2
    pair_s[...] = x
    lo = pair_s[pl.ds(0, half, stride=2), :]
    hi = pair_s[pl.ds(1, half, stride=2), :]
    return _bf16_bits(hi) | (_bf16_bits(lo) >> 16)


def _both_halves(x):
    b = _bf16_bits(x)
    return b | (b >> 16)


def _peer_gate_kernel(st_ref, kk_ref, ea_ref, cw_ref, ebw_ref, row_s, pair_s, *, peer_tile):
    tt = st_ref.shape[1]
    ng = tt // LANES
    k1 = PEER_TOPK + 1
    tiles_per_peer = peer_tile // LANES

    def head(h, carry):
        base = pl.multiple_of(h * 2 * N_KEYS, 2 * N_KEYS)
        s1 = st_ref[pl.ds(base, N_KEYS), :]
        s2 = st_ref[pl.ds(base + N_KEYS, N_KEYS), :]
        s1r = s1.reshape(N_KEYS, ng, LANES)
        s2r = s2.reshape(N_KEYS, ng, LANES)
        top_a, next_a = _top16_and_next([s1r[a] for a in range(N_KEYS)])
        top_b, next_b = _top16_and_next([s2r[a] for a in range(N_KEYS)])
        la = top_a + [next_a]
        lb = top_b + [next_b]
        cands = [la[i - 1] + lb[j - 1] for i in range(1, k1 + 1) for j in range(1, k1 + 1) if i * j <= k1]
        pad = jnp.full_like(cands[0], PAD_SCORE)
        cands = cands + [pad] * (-len(cands) % PEER_TOPK)
        top_c, next_c = _top16_and_next(cands)
        tau = 0.5 * (top_c[PEER_TOPK - 1] + next_c)
        den = jnp.ones_like(tau)
        for cv in top_c[1:]:
            den = den + jnp.exp(cv - top_c[0])
        vals = [tau, la[0], 1.0 / den] + lb
        for r, val in enumerate(vals):
            for g in range(ng):
                row_s[r:r + 1, g * LANES:(g + 1) * LANES] = val[g:g + 1, :]
        rows = pl.ds(pl.multiple_of(h * (N_KEYS // 2), N_KEYS // 2), N_KEYS // 2)
        kk_words, ea_words = [], []
        for g in range(ng):
            lg = slice(g * LANES, (g + 1) * LANES)
            s1g = st_ref[pl.ds(base, N_KEYS), lg]
            s2g = st_ref[pl.ds(base + N_KEYS, N_KEYS), lg]
            tau_r = row_s[0:1, lg]
            m1_r = row_s[1:2, lg]
            iz_r = row_s[2:3, lg]
            lb_r = [row_s[3 + j:4 + j, lg] for j in range(k1)]
            code = _count_true_suffix(lambda row: s2g >= row, lb_r[1:]) + jnp.where(s2g >= lb_r[0], 1.0, 0.0)
            th = tau_r - s1g
            n = _count_true_suffix(lambda row: row >= th, lb_r[PEER_TOPK - 1::-1])
            kk_words.append(_both_halves(float(k1 + 1) - n))
            ea_words.append(_both_halves(jnp.exp(s1g - m1_r) * iz_r))
            cw_ref[rows, lg] = _pack_row_pairs(code, pair_s)
            gr = g // tiles_per_peer * tiles_per_peer + (g + 1) % tiles_per_peer
            ebw_ref[rows, gr * LANES:(gr + 1) * LANES] = _pack_row_pairs(jnp.exp(s2g - lb_r[0]), pair_s)
        kk_ref[:, pl.ds(h, 1), :] = jnp.concatenate(kk_words, axis=1)[:, None, :]
        ea_ref[:, pl.ds(h, 1), :] = jnp.concatenate(ea_words, axis=1)[:, None, :]
        return carry

    lax.fori_loop(0, PEER_HEADS, head, 0)


def _peer_gates(st, peer_tile):
    nscore, t = st.shape
    tt = PEER_GATE_TILE if t % PEER_GATE_TILE == 0 else peer_tile
    assert t % tt == 0 and tt % peer_tile == 0 and peer_tile > LANES and peer_tile % LANES == 0
    rows = jax.ShapeDtypeStruct((N_KEYS, PEER_HEADS, t), U32)
    packed = jax.ShapeDtypeStruct((PEER_HEADS * N_KEYS // 2, t), U32)
    return pl.pallas_call(
        functools.partial(_peer_gate_kernel, peer_tile=peer_tile),
        grid=(t // tt,),
        in_specs=[pl.BlockSpec((nscore, tt), lambda i: (0, i))],
        out_specs=[pl.BlockSpec((N_KEYS, PEER_HEADS, tt), lambda i: (0, 0, i)),
                   pl.BlockSpec((N_KEYS, PEER_HEADS, tt), lambda i: (0, 0, i)),
                   pl.BlockSpec((PEER_HEADS * N_KEYS // 2, tt), lambda i: (0, i)),
                   pl.BlockSpec((PEER_HEADS * N_KEYS // 2, tt), lambda i: (0, i))],
        out_shape=[rows, rows, packed, packed],
        scratch_shapes=[pltpu.VMEM((3 * SUBLANES, tt), F32), pltpu.VMEM((N_KEYS, LANES), F32)],
        compiler_params=_params("arbitrary"),
        name="peer_gates",
    )(st)


def _peer_kernel(cw_ref, ebw_ref, kk_ref, ea_ref, xt_ref, u_ref, vt_ref, x1_ref, g_ref, b_ref, o_ref,
                 ht0_s, ht1_s, wt0_s, wt1_s, acc_s, *, alpha, nblk):
    c = pl.program_id(0)
    tt = xt_ref.shape[1]
    ch = u_ref.shape[0]

    @pl.when(c == 0)
    def _clear():
        ht1_s[...] = jnp.zeros(ht1_s.shape, F32)
        wt0_s[...] = jnp.zeros(wt0_s.shape, BF16)
        wt1_s[...] = jnp.zeros(wt1_s.shape, BF16)
        acc_s[...] = jnp.zeros(acc_s.shape, F32)

    ablocks = ch // N_KEYS
    lt = PEER_LANE_TILE

    def stages(ht_new, ht_old, wt_new, wt_old):
        d = vt_ref.shape[0]
        npiece = ablocks // 2
        nlv = lt // LANES
        nr2 = N_KEYS // (2 * SUBLANES)
        assert nlv == 2
        for ts in range(tt // lt):
            ls = slice(ts * lt, (ts + 1) * lt)
            for unit in range(npiece * nlv):
                ap, tv = unit // nlv, unit % nlv
                piece = unit // 2
                if unit % 2 == 0:
                    rows = slice(piece * (ch // npiece), (piece + 1) * (ch // npiece))
                    lhs_ref, rhs_ref, kdim = u_ref, xt_ref, u_ref.shape[1]
                else:
                    rows = slice(piece * (d // npiece), (piece + 1) * (d // npiece))
                    lhs_ref, rhs_ref, kdim = vt_ref, wt_old, ch
                nk = kdim // MXU_DEPTH
                part = None
                g = ts * nlv + tv
                lv = slice(g * LANES, (g + 1) * LANES)
                ge = (g + 1) % (tt // LANES)
                le = slice(ge * LANES, (ge + 1) * LANES)
                als = (2 * ap, 2 * ap + 1)
                spread = lambda ref, al, h: pltpu.bitcast(jnp.broadcast_to(ref[al, h:h + 1, lv], (SUBLANES, LANES)), BF16)
                kk = [[spread(kk_ref, al, h) for h in range(PEER_HEADS)] for al in als]
                ea = [[spread(ea_ref, al, h) for h in range(PEER_HEADS)] for al in als]
                for r2 in range(nr2):
                    if r2 % (nr2 // nk) == (1 if unit % 2 == 0 else 2):
                        kb = slice(r2 // (nr2 // nk) * MXU_DEPTH, (r2 // (nr2 // nk) + 1) * MXU_DEPTH)
                        dk = _dot(lhs_ref[rows, kb], rhs_ref[kb, ls])
                        part = dk if part is None else part + dk
                    gates = [jnp.zeros((2 * SUBLANES, LANES), BF16) for _ in als]
                    for h in range(PEER_HEADS):
                        wrows = slice(h * (N_KEYS // 2) + r2 * SUBLANES, h * (N_KEYS // 2) + (r2 + 1) * SUBLANES)
                        code = pltpu.bitcast(cw_ref[wrows, lv], BF16)
                        eb = pltpu.bitcast(ebw_ref[wrows, le], BF16)
                        for i in range(2):
                            gates[i] = gates[i] + jnp.where(code >= kk[i][h], eb, jnp.zeros_like(eb)) * ea[i][h]
                    for i, al in enumerate(als):
                        rs = slice(al * N_KEYS + r2 * 2 * SUBLANES, al * N_KEYS + (r2 + 1) * 2 * SUBLANES)
                        wt_new[rs, lv] = gates[i] * _gelu_tanh(ht_old[rs, lv].astype(BF16))
                if unit % 2 == 0:
                    ht_new[rows, ls] = part
                else:
                    acc_s[rows, ls] += part

    @pl.when(c % 2 == 0)
    def _even():
        stages(ht0_s, ht1_s, wt1_s, wt0_s)

    @pl.when(c % 2 == 1)
    def _odd():
        stages(ht1_s, ht0_s, wt0_s, wt1_s)

    @pl.when(jnp.logical_and(c >= PEER_PIPE_FILL, (c - PEER_PIPE_FILL) % nblk == nblk - 1))
    def _finish():
        y = alpha * x1_ref[...] + acc_s[...].T
        o_ref[...] = _layer_norm(y, g_ref[...], b_ref[...])
        acc_s[...] = jnp.zeros(acc_s.shape, F32)


def _peer(cw, ebw, kk, ea, xt, u_bf, vt_bf, x1, g, b, alpha, tt):
    t, d = x1.shape
    ne = u_bf.shape[0]
    ch = PEER_EXPERT_BLOCK
    assert t % tt == 0 and tt % PEER_LANE_TILE == 0 and ne % ch == 0 and ne == N_KEYS * N_KEYS
    assert vt_bf.shape == (ne // ch, d, ch)
    nblk = ne // ch
    ablocks = ch // N_KEYS
    npairs = (t // tt) * nblk
    tile = lambda c, lag: jnp.clip(c - lag, 0, npairs - 1) // nblk
    block = lambda c, lag: jnp.clip(c - lag, 0, npairs - 1) % nblk
    return pl.pallas_call(
        functools.partial(_peer_kernel, alpha=alpha, nblk=nblk),
        grid=(npairs + PEER_PIPE_FILL,),
        in_specs=[
            pl.BlockSpec((PEER_HEADS * N_KEYS // 2, tt), lambda c: (0, tile(c, 1))),
            pl.BlockSpec((PEER_HEADS * N_KEYS // 2, tt), lambda c: (0, tile(c, 1))),
            pl.BlockSpec((ablocks, PEER_HEADS, tt), lambda c: (block(c, 1), 0, tile(c, 1))),
            pl.BlockSpec((ablocks, PEER_HEADS, tt), lambda c: (block(c, 1), 0, tile(c, 1))),
            pl.BlockSpec((d, tt), lambda c: (0, tile(c, 0))),
            pl.BlockSpec((ch, d), lambda c: (block(c, 0), 0)),
            pl.BlockSpec((None, d, ch), lambda c: (block(c, PEER_PIPE_FILL), 0, 0)),
            pl.BlockSpec((tt, d), lambda c: (tile(c, PEER_PIPE_FILL), 0)),
            pl.BlockSpec((1, d), lambda c: (0, 0)),
            pl.BlockSpec((1, d), lambda c: (0, 0)),
        ],
        out_specs=pl.BlockSpec((tt, d), lambda c: (tile(c, PEER_PIPE_FILL), 0)),
        out_shape=jax.ShapeDtypeStruct((t, d), F32),
        scratch_shapes=[
            pltpu.VMEM((ch, tt), F32),
            pltpu.VMEM((ch, tt), F32),
            pltpu.VMEM((ch, tt), BF16),
            pltpu.VMEM((ch, tt), BF16),
            pltpu.VMEM((d, tt), F32),
        ],
        compiler_params=_params("arbitrary"),
        name="peer",
    )(cw, ebw, kk, ea, xt, u_bf, vt_bf, x1, g, b)


def kernel(x_prompt, x_sample, cache_sb_k, cache_sb_v, cache_band_k, cache_band_v, w_in, w_out, gn_a, gn_b,
           rel_bias, ln1_g, ln1_b, peer_query, peer_subkeys, peer_u, peer_v, ln2_g, ln2_b):
    batch, seq, d = x_prompt.shape
    dec_batch, dec_seq, _ = x_sample.shape
    depth = w_in.shape[0]
    past = cache_sb_k.shape[2]
    nb = cache_band_k.shape[2]
    assert w_in.shape[2] == 6 * WIDTH and w_out.shape[1] == 2 * WIDTH
    assert seq >= BAND_PAST and nb == BAND_PAST
    alpha = float((2 * depth) ** 0.25)
    tp = batch * seq
    ts = dec_batch * dec_seq

    xp = x_prompt.reshape(tp, d)
    xs = x_sample.reshape(ts, d)
    row2 = lambda a: a.reshape(1, -1)
    heads = lambda a, n, s: a.reshape(n, s, N_HEADS, HEAD_DIM)
    to_slab = lambda a: jnp.transpose(a, (0, 1, 3, 4, 2)).reshape(a.shape[0], a.shape[1], WIDTH, a.shape[2])
    caches = [to_slab(c) for c in (cache_sb_k, cache_sb_v, cache_band_k, cache_band_v)]
    from_slab = lambda a: jnp.transpose(a.reshape(a.shape[0], N_HEADS, HEAD_DIM, a.shape[2]), (0, 3, 1, 2))
    outs = [[] for _ in range(8)]
    for l in range(depth):
        w_in_bf = w_in[l].astype(BF16)
        woa = w_out[l, :WIDTH].astype(BF16)
        wob = w_out[l, WIDTH:].astype(BF16)
        wq = peer_query[l].astype(BF16)
        sk = peer_subkeys[l].astype(BF16)
        u_bf = peer_u[l].astype(BF16)
        vt_bf = peer_v[l].reshape(-1, PEER_EXPERT_BLOCK, d).transpose(0, 2, 1).astype(BF16)
        ga, gb = row2(gn_a[l]), row2(gn_b[l])
        g1, b1, g2, b2 = row2(ln1_g[l]), row2(ln1_b[l]), row2(ln2_g[l]), row2(ln2_b[l])

        qa, kab, vab, qb, kbb, vbb, sbk, sbv, bdk, bdv = _project_prompt(xp, w_in_bf, batch, seq)
        ca = _sb_prompt(qa, kab, vab, ga, batch, seq)
        cb = _band_prompt(qb, kbb, vbb, _band_prompt_bias(rel_bias[l]), gb, batch, seq)
        x1, x1t, st = _merge(ca, cb, xp, woa, wob, g1, b1, wq, sk, alpha)
        tt = min(PEER_TOKEN_TILE, tp)
        kk, ea, cw, ebw = _peer_gates(st, tt)
        xp = _peer(cw, ebw, kk, ea, x1t, u_bf, vt_bf, x1, g2, b2, alpha, tt)
        outs[0].append(from_slab(sbk))
        outs[1].append(from_slab(sbv))
        outs[2].append(from_slab(bdk))
        outs[3].append(from_slab(bdv))

        qa, ka, va, kab, vab, qb, kb, vb, kbb, vbb = _project(xs, w_in_bf)
        biasp, biasn = _sample_bias(rel_bias[l], dec_seq, nb)
        ca, cb, nbk, nbv = _sample_attn(
            qa, kab, vab, caches[0], caches[1], qb, kbb, vbb, kb, vb, caches[2], caches[3], biasp, biasn, ga, gb, l)
        x1, x1t, st = _merge(ca, cb, xs, woa, wob, g1, b1, wq, sk, alpha)
        tt = min(PEER_TOKEN_TILE, ts)
        kk, ea, cw, ebw = _peer_gates(st, tt)
        xs = _peer(cw, ebw, kk, ea, x1t, u_bf, vt_bf, x1, g2, b2, alpha, tt)
        outs[4].append(heads(ka, dec_batch, dec_seq))
        outs[5].append(heads(va, dec_batch, dec_seq))
        outs[6].append(from_slab(nbk))
        outs[7].append(from_slab(nbv))

    return (xp.reshape(batch, seq, d), xs.reshape(dec_batch, dec_seq, d)) + tuple(jnp.stack(o) for o in outs)
```

```python
import functools

import jax
import jax.numpy as jnp
import numpy as np
from jax import lax
from jax.experimental import pallas as pl
from jax.experimental.pallas import tpu as pltpu

F32 = jnp.float32
BF16 = jnp.bfloat16
U32 = jnp.uint32

HEAD_DIM = 64
N_HEADS = 8
WIDTH = N_HEADS * HEAD_DIM
CHUNK = 64
BAND_CHUNKS = 8
BAND_PAST = BAND_CHUNKS * CHUNK
MAX_REL = 128
N_KEYS = 128
PEER_HEADS = 8
PEER_TOPK = 16
NORM_EPS = 1e-5
NEG_INF = -1e30

LANES = 128
SUBLANES = 8
MXU_DEPTH = 256
VMEM_LIMIT_BYTES = 56 * 1024 * 1024

SB_DEAD_LOG = -104.0
SB_BLOCK = 128
SB_FIRST_BLOCKS = 3
BAND_QBLOCK = 256
BAND_HEAD_GROUP = 4
ROW_TILE = 512
PEER_TOKEN_TILE = 1024
PEER_GATE_TILE = 1024
PEER_EXPERT_BLOCK = 512
PEER_PIPE_FILL = 2
PEER_LANE_TILE = 256
PAD_SCORE = -3.0e38


def _params(*sem):
    return pltpu.CompilerParams(dimension_semantics=sem, vmem_limit_bytes=VMEM_LIMIT_BYTES)


def _nt_dot(a, b):
    return lax.dot_general(a, b, (((1,), (1,)), ((), ())), preferred_element_type=F32)


def _dot(a, b):
    return jnp.dot(a, b, preferred_element_type=F32)


def _layer_norm(y, g, b):
    mu = jnp.mean(y, axis=-1, keepdims=True)
    d = y - mu
    var = jnp.mean(d * d, axis=-1, keepdims=True)
    return d * lax.rsqrt(var + NORM_EPS) * g + b


def _head_rms(o, gain):
    ms = jnp.mean(o * o, axis=-1, keepdims=True)
    return o * lax.rsqrt(ms + NORM_EPS) * gain


def _proj_kernel(x_ref, w_ref, qa_ref, ka_ref, va_ref, kab_ref, vab_ref,
                 qb_ref, kb_ref, vb_ref, kbb_ref, vbb_ref):
    xb = x_ref[...].astype(BF16)
    scale = HEAD_DIM ** -0.5

    def group(g):
        return _dot(xb, w_ref[:, g * WIDTH:(g + 1) * WIDTH])

    qa_ref[...] = (group(0) * scale).astype(BF16)
    k = group(1)
    ka_ref[...] = k
    kab_ref[...] = k.astype(BF16)
    v = group(2)
    va_ref[...] = v
    vab_ref[...] = v.astype(BF16)
    qb_ref[...] = (group(3) * scale).astype(BF16)
    k = group(4)
    kb_ref[...] = k
    kbb_ref[...] = k.astype(BF16)
    v = group(5)
    vb_ref[...] = v
    vbb_ref[...] = v.astype(BF16)


def _project(x, w_bf):
    t, d = x.shape
    tm = min(ROW_TILE, t)
    assert t % tm == 0
    row = lambda i: (i, 0)
    f32o = jax.ShapeDtypeStruct((t, WIDTH), F32)
    bf16o = jax.ShapeDtypeStruct((t, WIDTH), BF16)
    blk = pl.BlockSpec((tm, WIDTH), row)
    return pl.pallas_call(
        _proj_kernel,
        grid=(t // tm,),
        in_specs=[pl.BlockSpec((tm, d), row), pl.BlockSpec(w_bf.shape, lambda i: (0, 0))],
        out_specs=[blk] * 10,
        out_shape=[bf16o, f32o, f32o, bf16o, bf16o, bf16o, f32o, f32o, bf16o, bf16o],
        compiler_params=_params("arbitrary"),
        name="proj",
    )(x, w_bf)


def _proj_prompt_kernel(x_ref, w_ref, *refs, tiles_per_stream):
    qa_ref, kab_ref, vab_ref, qb_ref, kbb_ref, vbb_ref, sbk_ref, sbv_ref, bdk_ref, bdv_ref = refs[-10:]
    j = pl.program_id(0) % tiles_per_stream
    xb = x_ref[...].astype(BF16)
    scale = HEAD_DIM ** -0.5

    def group(g):
        return _dot(xb, w_ref[:, g * WIDTH:(g + 1) * WIDTH])

    qa_ref[...] = (group(0) * scale).astype(BF16)
    k = group(1)
    kab_ref[...] = k.astype(BF16)
    sbk_ref[0] = k.T
    v = group(2)
    vab_ref[...] = v.astype(BF16)
    sbv_ref[0] = v.T
    qb_ref[...] = (group(3) * scale).astype(BF16)
    kb = group(4)
    kbb_ref[...] = kb.astype(BF16)
    vb = group(5)
    vbb_ref[...] = vb.astype(BF16)

    @pl.when(j == tiles_per_stream - 1)
    def _band_rows():
        bdk_ref[0] = kb.T
        bdv_ref[0] = vb.T


def _project_prompt(x, w_bf, batch, seq, layer, depth, stacked):
    t, d = x.shape
    tm = ROW_TILE
    assert seq % tm == 0 and tm == BAND_PAST and t == batch * seq
    nj = seq // tm
    row = lambda i: (i, 0)
    bf16o = jax.ShapeDtypeStruct((t, WIDTH), BF16)
    blk = pl.BlockSpec((tm, WIDTH), row)
    carried = () if stacked is None else tuple(stacked)
    return pl.pallas_call(
        functools.partial(_proj_prompt_kernel, tiles_per_stream=nj),
        grid=(t // tm,),
        in_specs=[pl.BlockSpec((tm, d), row), pl.BlockSpec(w_bf.shape, lambda i: (0, 0))]
                 + [pl.BlockSpec(memory_space=pl.ANY)] * len(carried),
        out_specs=[blk] * 6 + [pl.BlockSpec((None, 1, WIDTH, tm), lambda i: (layer, i // nj, 0, i % nj))] * 2
                  + [pl.BlockSpec((None, 1, WIDTH, tm), lambda i: (layer, i // nj, 0, 0))] * 2,
        out_shape=[bf16o] * 6 + [jax.ShapeDtypeStruct((depth, batch, WIDTH, seq), F32)] * 2
                  + [jax.ShapeDtypeStruct((depth, batch, WIDTH, tm), F32)] * 2,
        input_output_aliases={2 + k: 6 + k for k in range(len(carried))},
        compiler_params=_params("arbitrary"),
        name="proj_prompt",
    )(x, w_bf, *carried)


def _suffix_matrix(kb):
    kp = np.arange(kb)
    m = (kp[:, None] > kp[None, :]).astype(np.float32)
    one = np.concatenate([np.ones((kb, LANES), np.float32), m], axis=1)
    return jnp.asarray(np.concatenate([one, one], axis=0), dtype=BF16)


def _sb_log_terms(z, mask):
    sp = jnp.maximum(z, 0.0) + jnp.log(1.0 + jnp.exp(-jnp.abs(z)))
    log_keep = -sp
    if mask is not None:
        log_keep = jnp.where(mask, log_keep, 0.0)
    hi = log_keep.astype(BF16)
    lo = (log_keep - hi.astype(F32)).astype(BF16)
    return z - sp, jnp.concatenate([hi, lo], axis=1)


def _sb_weights(log_sig, sums, carry, mask):
    kb = log_sig.shape[1]
    log_w = log_sig + sums[:, LANES:LANES + kb]
    if carry is not None:
        log_w = log_w + carry
    w = jnp.exp(log_w)
    if mask is not None:
        w = jnp.where(mask, w, 0.0)
    return w, sums[:, :LANES]


def _sb_prompt_kernel(q_ref, k_ref, v_ref, g_ref, mm_ref, o_ref, carry_s, acc_s):
    i = pl.program_id(1)
    r = q_ref.shape[0]
    row = lax.broadcasted_iota(jnp.int32, (r, r), 0)
    col = lax.broadcasted_iota(jnp.int32, (r, r), 1)
    diag = col < row
    heads = [slice(h * HEAD_DIM, (h + 1) * HEAD_DIM) for h in range(N_HEADS)]

    def key_block(j):
        off = pl.multiple_of(j * r, r)
        q = q_ref[...]
        kj = k_ref[pl.ds(off, r), :]
        vj = v_ref[pl.ds(off, r), :]
        mm = mm_ref[...]
        zs = [_nt_dot(q[:, hs], kj[:, hs]) for hs in heads]
        terms = [_sb_log_terms(z, None) for z in zs]
        sums = [_dot(halves, mm) for _, halves in terms]
        ws = [_sb_weights(terms[h][0], sums[h], carry_s[h], None) for h in range(N_HEADS)]
        pvs = [_dot(ws[h][0].astype(BF16), vj[:, hs]) for h, hs in enumerate(heads)]
        alive = None
        for h in range(N_HEADS):
            carry = carry_s[h] + ws[h][1]
            carry_s[h] = carry
            acc_s[h] = acc_s[h] + pvs[h]
            alive = carry if alive is None else jnp.maximum(alive, carry)
        return jnp.max(alive)

    def first_blocks():
        q = q_ref[...]
        mm = mm_ref[...]
        offs = [pl.multiple_of(jnp.maximum(i - b, 0) * r, r) for b in range(SB_FIRST_BLOCKS)]
        masks = [diag] + [i >= b for b in range(1, SB_FIRST_BLOCKS)]
        ks = [k_ref[pl.ds(off, r), :] for off in offs]
        vs = [v_ref[pl.ds(off, r), :] for off in offs]
        zs = [[_nt_dot(q[:, hs], kb[:, hs]) for hs in heads] for kb in ks]
        terms = [[_sb_log_terms(z, m) for z in zb] for zb, m in zip(zs, masks)]
        sums = [[_dot(halves, mm) for _, halves in tb] for tb in terms]
        alive = None
        for h, hs in enumerate(heads):
            carry, acc = None, None
            for b in range(SB_FIRST_BLOCKS):
                w, total = _sb_weights(terms[b][h][0], sums[b][h], carry, masks[b])
                pv = _dot(w.astype(BF16), vs[b][:, hs])
                carry = total if carry is None else carry + total
                acc = pv if acc is None else acc + pv
            carry_s[h] = carry
            acc_s[h] = acc
            alive = carry if alive is None else jnp.maximum(alive, carry)
        return jnp.max(alive)

    def cond(s):
        j, cmax = s
        return jnp.logical_and(j >= 0, cmax > SB_DEAD_LOG)

    def body(s):
        j, _ = s
        return j - 1, key_block(j)

    lax.while_loop(cond, body, (i - SB_FIRST_BLOCKS, first_blocks()))
    outs = [_head_rms(acc_s[h], g_ref[:, hs]) for h, hs in enumerate(heads)]
    o_ref[...] = jnp.concatenate(outs, axis=1).astype(BF16)


def _sb_prompt(q, k, v, gain, batch, seq):
    r = SB_BLOCK
    assert seq % r == 0
    nq = seq // r
    return pl.pallas_call(
        _sb_prompt_kernel,
        grid=(batch, nq),
        in_specs=[
            pl.BlockSpec((r, WIDTH), lambda b, i: (b * nq + i, 0)),
            pl.BlockSpec((seq, WIDTH), lambda b, i: (b, 0)),
            pl.BlockSpec((seq, WIDTH), lambda b, i: (b, 0)),
            pl.BlockSpec((1, WIDTH), lambda b, i: (0, 0)),
            pl.BlockSpec((2 * r, LANES + r), lambda b, i: (0, 0)),
        ],
        out_specs=pl.BlockSpec((r, WIDTH), lambda b, i: (b * nq + i, 0)),
        out_shape=jax.ShapeDtypeStruct((batch * seq, WIDTH), BF16),
        scratch_shapes=[pltpu.VMEM((N_HEADS, r, LANES), F32),
                        pltpu.VMEM((N_HEADS, r, HEAD_DIM), F32)],
        compiler_params=_params("arbitrary", "arbitrary"),
        name="sb_prompt",
    )(q, k, v, gain, _suffix_matrix(r))


def _band_prompt_kernel(q_ref, k0_ref, k1_ref, k2_ref, v0_ref, v1_ref, v2_ref, bias_ref, g_ref, o_ref):
    i = pl.program_id(1)
    k_refs = (k0_ref, k1_ref, k2_ref)
    v_refs = (v0_ref, v1_ref, v2_ref)
    heads = [slice(h * HEAD_DIM, (h + 1) * HEAD_DIM) for h in range(N_HEADS)]
    q = q_ref[...]
    ks = [k_ref[...] for k_ref in k_refs]
    vs = [v_ref[...] for v_ref in v_refs]
    outs = []
    for g0 in range(0, N_HEADS, BAND_HEAD_GROUP):
        group = range(g0, g0 + BAND_HEAD_GROUP)
        scores = []
        for h in group:
            hs = heads[h]
            scs = []
            for w in range(3):
                sc = _nt_dot(q[:, hs], ks[w][:, hs]) + bias_ref[h, w]
                if w < 2:
                    sc = jnp.where(i >= 2 - w, sc, NEG_INF)
                scs.append(sc)
            scores.append(scs)
        weights = []
        for scs in scores:
            m = jnp.max(jnp.maximum(jnp.maximum(scs[0], scs[1]), scs[2]), axis=-1, keepdims=True)
            es = [jnp.exp(sc - m) for sc in scs]
            den = jnp.sum(es[0] + es[1] + es[2], axis=-1, keepdims=True)
            weights.append(([e.astype(BF16) for e in es], den))
        for (es, den), h in zip(weights, group):
            hs = heads[h]
            acc = _dot(es[0], vs[0][:, hs]) + _dot(es[1], vs[1][:, hs]) + _dot(es[2], vs[2][:, hs])
            outs.append(_head_rms(acc / den, g_ref[:, hs]))
    o_ref[...] = jnp.concatenate(outs, axis=1).astype(BF16)


def _toeplitz_bias(rel_bias, n, ncols, offset):
    period = n + ncols + 1
    m = jnp.arange(period)
    shift = jnp.where(m < ncols, m, m - period)
    vec = rel_bias[:, jnp.clip(offset - shift, -MAX_REL, MAX_REL) + MAX_REL].astype(F32)
    rows = jnp.tile(vec, (1, n))[:, :n * (period - 1)].reshape(rel_bias.shape[0], n, period - 1)
    return rows[:, :, :ncols]


def _band_prompt_bias(rel_bias):
    qb = BAND_QBLOCK
    bias = _toeplitz_bias(rel_bias, qb, 3 * qb, 2 * qb)
    kc = jnp.arange(3 * qb)[None, :] // CHUNK - (2 * qb // CHUNK)
    qc = jnp.arange(qb)[:, None] // CHUNK
    valid = (kc <= qc) & (kc >= qc - BAND_CHUNKS)
    bias = jnp.where(valid[None], bias, NEG_INF)
    return bias.reshape(N_HEADS, qb, 3, qb).transpose(0, 2, 1, 3)


def _band_prompt(q, k, v, bias, gain, batch, seq):
    qb = BAND_QBLOCK
    assert seq % qb == 0 and 2 * qb == BAND_PAST and qb % CHUNK == 0
    nq = seq // qb
    kspec = lambda back: pl.BlockSpec((qb, WIDTH), lambda b, i: (b * nq + jnp.maximum(i - back, 0), 0))
    return pl.pallas_call(
        _band_prompt_kernel,
        grid=(batch, nq),
        in_specs=[
            pl.BlockSpec((qb, WIDTH), lambda b, i: (b * nq + i, 0)),
            kspec(2), kspec(1), kspec(0), kspec(2), kspec(1), kspec(0),
            pl.BlockSpec(bias.shape, lambda b, i: (0, 0, 0, 0)),
            pl.BlockSpec((1, WIDTH), lambda b, i: (0, 0)),
        ],
        out_specs=pl.BlockSpec((qb, WIDTH), lambda b, i: (b * nq + i, 0)),
        out_shape=jax.ShapeDtypeStruct((batch * seq, WIDTH), BF16),
        compiler_params=_params("arbitrary", "arbitrary"),
        name="band_prompt",
    )(q, k, k, k, v, v, v, bias, gain)


def _sample_attn_kernel(qa_ref, kan_ref, van_ref, cak_ref, cav_ref,
                        qb_ref, kbn_ref, vbn_ref, kbf_ref, vbf_ref, cbk_ref, cbv_ref,
                        biasp_ref, biasn_ref, ga_ref, gb_ref, mmn_ref, mmp_ref, *refs):
    oa_ref, ob_ref, nbk_ref, nbv_ref = refs[-4:]
    sd = qa_ref.shape[0]
    past = cak_ref.shape[2]
    nb = cbk_ref.shape[2]
    row = lax.broadcasted_iota(jnp.int32, (sd, sd), 0)
    col = lax.broadcasted_iota(jnp.int32, (sd, sd), 1)
    diag = col < row
    cak = cak_ref[0].astype(BF16)
    cav = cav_ref[0].astype(BF16)
    cbk = cbk_ref[0].astype(BF16)
    cbv = cbv_ref[0].astype(BF16)
    mmn = mmn_ref[...]
    mmp = mmp_ref[...]
    heads = [slice(h * HEAD_DIM, (h + 1) * HEAD_DIM) for h in range(N_HEADS)]
    blocks = [slice(j * SB_BLOCK, (j + 1) * SB_BLOCK) for j in range(past // SB_BLOCK)]
    qa = qa_ref[...]
    z_new = [_nt_dot(qa[:, hs], kan_ref[:, hs]) for hs in heads]
    z_past = [_dot(qa[:, hs], cak[hs, :]) for hs in heads]
    t_new = [_sb_log_terms(z, diag) for z in z_new]
    t_past = [[_sb_log_terms(z[:, ks], None) for ks in blocks] for z in z_past]
    s_new = [_dot(halves, mmn) for _, halves in t_new]
    s_past = [[_dot(halves, mmp) for _, halves in t] for t in t_past]
    outs_a = []
    for h, hs in enumerate(heads):
        w_new, carry = _sb_weights(t_new[h][0], s_new[h], None, diag)
        w_past = [None] * len(blocks)
        for j in reversed(range(len(blocks))):
            w_past[j], total = _sb_weights(t_past[h][j][0], s_past[h][j], carry, None)
            carry = carry + total
        acc = _dot(w_new.astype(BF16), van_ref[:, hs]) + _nt_dot(jnp.concatenate(w_past, axis=1).astype(BF16), cav[hs, :])
        outs_a.append(_head_rms(acc, ga_ref[:, hs]))
    qb = qb_ref[...]
    sc_p = [_dot(qb[:, hs], cbk[hs, :]) + biasp_ref[h] for h, hs in enumerate(heads)]
    sc_n = [_nt_dot(qb[:, hs], kbn_ref[:, hs]) + biasn_ref[h] for h, hs in enumerate(heads)]
    outs_b = []
    for h, hs in enumerate(heads):
        m = jnp.maximum(jnp.max(sc_p[h], axis=-1, keepdims=True), jnp.max(sc_n[h], axis=-1, keepdims=True))
        ep = jnp.exp(sc_p[h] - m)
        en = jnp.exp(sc_n[h] - m)
        den = jnp.sum(ep, axis=-1, keepdims=True) + jnp.sum(en, axis=-1, keepdims=True)
        acc = _nt_dot(ep.astype(BF16), cbv[hs, :]) + _dot(en.astype(BF16), vbn_ref[:, hs])
        outs_b.append(_head_rms(acc / den, gb_ref[:, hs]))
    oa_ref[...] = jnp.concatenate(outs_a, axis=1).astype(BF16)
    ob_ref[...] = jnp.concatenate(outs_b, axis=1).astype(BF16)
    nbk_ref[0, :, :nb - sd] = cbk_ref[0, :, sd:]
    nbk_ref[0, :, nb - sd:] = kbf_ref[...].T
    nbv_ref[0, :, :nb - sd] = cbv_ref[0, :, sd:]
    nbv_ref[0, :, nb - sd:] = vbf_ref[...].T


def _sample_bias(rel_bias, sd, nb):
    bias = _toeplitz_bias(rel_bias, sd, nb + sd, nb)
    return bias[:, :, :nb], bias[:, :, nb:]


def _sample_attn(qa, kan, van, cak, cav, qb, kbn, vbn, kbf, vbf, cbk, cbv, biasp, biasn, ga, gb, layer, stacked):
    _, nbatch, _, past = cak.shape
    nb = cbk.shape[3]
    sd = qa.shape[0] // nbatch
    assert past % SB_BLOCK == 0 and sd % 8 == 0 and sd <= nb and sd <= LANES
    rows = pl.BlockSpec((sd, WIDTH), lambda b: (b, 0))
    cache = lambda n: pl.BlockSpec((None, 1, WIDTH, n), lambda b: (layer, b, 0, 0))
    rolled = pl.BlockSpec((None, 1, WIDTH, nb), lambda b: (layer, b, 0, 0))
    carried = () if stacked is None else tuple(stacked)
    full = lambda a: pl.BlockSpec(a.shape, lambda b: (0,) * a.ndim)
    mmn = _suffix_matrix(sd)
    mmp = _suffix_matrix(SB_BLOCK)
    return pl.pallas_call(
        _sample_attn_kernel,
        grid=(nbatch,),
        in_specs=[rows, rows, rows, cache(past), cache(past),
                  rows, rows, rows, rows, rows, cache(nb), cache(nb),
                  full(biasp), full(biasn), full(ga), full(gb), full(mmn), full(mmp)]
                 + [pl.BlockSpec(memory_space=pl.ANY)] * len(carried),
        out_specs=[rows, rows, rolled, rolled],
        out_shape=[jax.ShapeDtypeStruct((nbatch * sd, WIDTH), BF16)] * 2
                  + [jax.ShapeDtypeStruct(cbk.shape, F32)] * 2,
        input_output_aliases={18 + k: 2 + k for k in range(len(carried))},
        compiler_params=_params("arbitrary"),
        name="sample_attn",
    )(qa, kan, van, cak, cav, qb, kbn, vbn, kbf, vbf, cbk, cbv, biasp, biasn, ga, gb, mmn, mmp, *carried)


def _merge_kernel(ca_ref, cb_ref, x_ref, woa_ref, wob_ref, g_ref, b_ref, wq_ref, sk_ref,
                  x1_ref, x1t_ref, st_ref, *, alpha):
    mix = _dot(ca_ref[...], woa_ref[...]) + _dot(cb_ref[...], wob_ref[...])
    x1 = _layer_norm(alpha * x_ref[...] + mix, g_ref[...], b_ref[...])
    x1_ref[...] = x1
    x1t_ref[...] = x1.T.astype(BF16)
    qp = _dot(x1.astype(BF16), wq_ref[...])
    half = sk_ref.shape[2]
    for h in range(PEER_HEADS):
        for j in range(2):
            c0 = (2 * h + j) * half
            qh = qp[:, c0:c0 + half].astype(BF16)
            st_ref[(2 * h + j) * N_KEYS:(2 * h + j + 1) * N_KEYS, :] = _nt_dot(sk_ref[j], qh)


def _merge(ca, cb, x, woa, wob, g, b, wq, sk, alpha):
    t, d = x.shape
    tm = min(ROW_TILE, t)
    assert t % tm == 0
    nscore = PEER_HEADS * 2 * N_KEYS
    row = lambda i: (i, 0)
    colb = lambda i: (0, i)
    full = lambda a: pl.BlockSpec(a.shape, lambda i: (0,) * a.ndim)
    return pl.pallas_call(
        functools.partial(_merge_kernel, alpha=alpha),
        grid=(t // tm,),
        in_specs=[pl.BlockSpec((tm, WIDTH), row), pl.BlockSpec((tm, WIDTH), row), pl.BlockSpec((tm, d), row),
                  full(woa), full(wob), full(g), full(b), full(wq), full(sk)],
        out_specs=[pl.BlockSpec((tm, d), row), pl.BlockSpec((d, tm), colb), pl.BlockSpec((nscore, tm), colb)],
        out_shape=[jax.ShapeDtypeStruct((t, d), F32), jax.ShapeDtypeStruct((d, t), BF16),
                   jax.ShapeDtypeStruct((nscore, t), F32)],
        compiler_params=_params("arbitrary"),
        name="merge",
    )(ca, cb, x, woa, wob, g, b, wq, sk)


def _cmp_exchange(v, i, j):
    a, b = v[i], v[j]
    v[i] = jnp.maximum(a, b)
    v[j] = jnp.minimum(a, b)


_SORT16_NETWORK = (
    (0, 13), (1, 12), (2, 15), (3, 14), (4, 8), (5, 6), (7, 11), (9, 10),
    (0, 5), (1, 7), (2, 9), (3, 4), (6, 13), (8, 14), (10, 15), (11, 12),
    (0, 1), (2, 3), (4, 5), (6, 8), (7, 9), (10, 11), (12, 13), (14, 15),
    (0, 2), (1, 3), (4, 10), (5, 11), (6, 7), (8, 9), (12, 14), (13, 15),
    (1, 2), (3, 12), (4, 6), (5, 7), (8, 10), (9, 11), (13, 14),
    (1, 4), (2, 6), (5, 8), (7, 10), (9, 13), (11, 14),
    (2, 4), (3, 6), (9, 12), (11, 13),
    (3, 5), (6, 8), (7, 9), (10, 12),
    (3, 4), (5, 6), (7, 8), (9, 10), (11, 12),
    (6, 7), (8, 9),
)


def _sort16_desc(v):
    v = list(v)
    assert len(v) == PEER_TOPK
    for i, j in _SORT16_NETWORK:
        _cmp_exchange(v, i, j)
    return v


def _count_true_suffix(test, rows):
    sel = jnp.where
    m1 = test(rows[8])
    m2 = test(sel(m1, rows[4], rows[12]))
    m3 = test(sel(m1, sel(m2, rows[2], rows[6]), sel(m2, rows[10], rows[14])))
    m4 = test(sel(m1, sel(m2, sel(m3, rows[1], rows[3]), sel(m3, rows[5], rows[7])),
                  sel(m2, sel(m3, rows[9], rows[11]), sel(m3, rows[13], rows[15]))))
    return sel(m1, 8.0, 0.0) + sel(m2, 4.0, 0.0) + sel(m3, 2.0, 0.0) + sel(m4, 1.0, 0.0) + sel(test(rows[0]), 1.0, 0.0)


def _merge_top16(a, b):
    n = len(a)
    top = [jnp.maximum(a[i], b[n - 1 - i]) for i in range(n)]
    out = [jnp.minimum(a[i], b[n - 1 - i]) for i in range(n)]
    while len(out) > 1:
        out = [jnp.maximum(out[2 * i], out[2 * i + 1]) for i in range(len(out) // 2)]
    j = n // 2
    while j >= 1:
        for i in range(n):
            if (i & j) == 0:
                _cmp_exchange(top, i, i + j)
        j //= 2
    return top, out[0]


def _top16_and_next(vals):
    groups = [_sort16_desc(vals[g:g + PEER_TOPK]) for g in range(0, len(vals), PEER_TOPK)]
    nxt = None
    while len(groups) > 1:
        merged = []
        for g in range(0, len(groups), 2):
            top, left = _merge_top16(groups[g], groups[g + 1])
            merged.append(top)
            nxt = left if nxt is None else jnp.maximum(nxt, left)
        groups = merged
    return groups[0], nxt


def _gelu_tanh(x):
    return 0.5 * x * (1.0 + jnp.tanh(0.7978845608028654 * (x + 0.044715 * (x * x * x))))


def _bf16_bits(x):
    return pltpu.bitcast(x.astype(BF16).astype(F32), U32)


def _pack_row_pairs(x, pair_s):
    half = x.shape[0] // 2
    pair_s[...] = x
    lo = pair_s[pl.ds(0, half, stride=2), :]
    hi = pair_s[pl.ds(1, half, stride=2), :]
    return _bf16_bits(hi) | (_bf16_bits(lo) >> 16)


def _both_halves(x):
    b = _bf16_bits(x)
    return b | (b >> 16)


def _peer_gate_kernel(st_ref, kk_ref, ea_ref, cw_ref, ebw_ref, row_s, pair_s, *, peer_tile):
    tt = st_ref.shape[1]
    ng = tt // LANES
    k1 = PEER_TOPK + 1
    tiles_per_peer = peer_tile // LANES

    def head(h, carry):
        base = pl.multiple_of(h * 2 * N_KEYS, 2 * N_KEYS)
        s1 = st_ref[pl.ds(base, N_KEYS), :]
        s2 = st_ref[pl.ds(base + N_KEYS, N_KEYS), :]
        s1r = s1.reshape(N_KEYS, ng, LANES)
        s2r = s2.reshape(N_KEYS, ng, LANES)
        top_a, next_a = _top16_and_next([s1r[a] for a in range(N_KEYS)])
        top_b, next_b = _top16_and_next([s2r[a] for a in range(N_KEYS)])
        la = top_a + [next_a]
        lb = top_b + [next_b]
        cands = [la[i - 1] + lb[j - 1] for i in range(1, k1 + 1) for j in range(1, k1 + 1) if i * j <= k1]
        pad = jnp.full_like(cands[0], PAD_SCORE)
        cands = cands + [pad] * (-len(cands) % PEER_TOPK)
        top_c, next_c = _top16_and_next(cands)
        tau = 0.5 * (top_c[PEER_TOPK - 1] + next_c)
        den = jnp.ones_like(tau)
        for cv in top_c[1:]:
            den = den + jnp.exp(cv - top_c[0])
        vals = [tau, la[0], 1.0 / den] + lb
        for r, val in enumerate(vals):
            for g in range(ng):
                row_s[r:r + 1, g * LANES:(g + 1) * LANES] = val[g:g + 1, :]
        rows = pl.ds(pl.multiple_of(h * (N_KEYS // 2), N_KEYS // 2), N_KEYS // 2)
        kk_words, ea_words = [], []
        for g in range(ng):
            lg = slice(g * LANES, (g + 1) * LANES)
            s1g = st_ref[pl.ds(base, N_KEYS), lg]
            s2g = st_ref[pl.ds(base + N_KEYS, N_KEYS), lg]
            tau_r = row_s[0:1, lg]
            m1_r = row_s[1:2, lg]
            iz_r = row_s[2:3, lg]
            lb_r = [row_s[3 + j:4 + j, lg] for j in range(k1)]
            code = _count_true_suffix(lambda row: s2g >= row, lb_r[1:]) + jnp.where(s2g >= lb_r[0], 1.0, 0.0)
            th = tau_r - s1g
            n = _count_true_suffix(lambda row: row >= th, lb_r[PEER_TOPK - 1::-1])
            kk_words.append(_both_halves(float(k1 + 1) - n))
            ea_words.append(_both_halves(jnp.exp(s1g - m1_r) * iz_r))
            cw_ref[rows, lg] = _pack_row_pairs(code, pair_s)
            gr = g // tiles_per_peer * tiles_per_peer + (g + 1) % tiles_per_peer
            ebw_ref[rows, gr * LANES:(gr + 1) * LANES] = _pack_row_pairs(jnp.exp(s2g - lb_r[0]), pair_s)
        kk_ref[:, pl.ds(h, 1), :] = jnp.concatenate(kk_words, axis=1)[:, None, :]
        ea_ref[:, pl.ds(h, 1), :] = jnp.concatenate(ea_words, axis=1)[:, None, :]
        return carry

    lax.fori_loop(0, PEER_HEADS, head, 0)


def _peer_gates(st, peer_tile):
    nscore, t = st.shape
    tt = PEER_GATE_TILE if t % PEER_GATE_TILE == 0 else peer_tile
    assert t % tt == 0 and tt % peer_tile == 0 and peer_tile > LANES and peer_tile % LANES == 0
    rows = jax.ShapeDtypeStruct((N_KEYS, PEER_HEADS, t), U32)
    packed = jax.ShapeDtypeStruct((PEER_HEADS * N_KEYS // 2, t), U32)
    return pl.pallas_call(
        functools.partial(_peer_gate_kernel, peer_tile=peer_tile),
        grid=(t // tt,),
        in_specs=[pl.BlockSpec((nscore, tt), lambda i: (0, i))],
        out_specs=[pl.BlockSpec((N_KEYS, PEER_HEADS, tt), lambda i: (0, 0, i)),
                   pl.BlockSpec((N_KEYS, PEER_HEADS, tt), lambda i: (0, 0, i)),
                   pl.BlockSpec((PEER_HEADS * N_KEYS // 2, tt), lambda i: (0, i)),
                   pl.BlockSpec((PEER_HEADS * N_KEYS // 2, tt), lambda i: (0, i))],
        out_shape=[rows, rows, packed, packed],
        scratch_shapes=[pltpu.VMEM((3 * SUBLANES, tt), F32), pltpu.VMEM((N_KEYS, LANES), F32)],
        compiler_params=_params("arbitrary"),
        name="peer_gates",
    )(st)


def _peer_kernel(cw_ref, ebw_ref, kk_ref, ea_ref, xt_ref, u_ref, vt_ref, x1_ref, g_ref, b_ref, o_ref,
                 ht0_s, ht1_s, wt0_s, wt1_s, acc_s, *, alpha, nblk):
    c = pl.program_id(0)
    tt = xt_ref.shape[1]
    ch = u_ref.shape[0]

    @pl.when(c == 0)
    def _clear():
        ht1_s[...] = jnp.zeros(ht1_s.shape, F32)
        wt0_s[...] = jnp.zeros(wt0_s.shape, BF16)
        wt1_s[...] = jnp.zeros(wt1_s.shape, BF16)
        acc_s[...] = jnp.zeros(acc_s.shape, F32)

    ablocks = ch // N_KEYS
    lt = PEER_LANE_TILE

    def stages(ht_new, ht_old, wt_new, wt_old):
        d = vt_ref.shape[0]
        npiece = ablocks // 2
        nlv = lt // LANES
        nr2 = N_KEYS // (2 * SUBLANES)
        assert nlv == 2
        for ts in range(tt // lt):
            ls = slice(ts * lt, (ts + 1) * lt)
            for unit in range(npiece * nlv):
                ap, tv = unit // nlv, unit % nlv
                piece = unit // 2
                if unit % 2 == 0:
                    rows = slice(piece * (ch // npiece), (piece + 1) * (ch // npiece))
                    lhs_ref, rhs_ref, kdim = u_ref, xt_ref, u_ref.shape[1]
                else:
                    rows = slice(piece * (d // npiece), (piece + 1) * (d // npiece))
                    lhs_ref, rhs_ref, kdim = vt_ref, wt_old, ch
                nk = kdim // MXU_DEPTH
                part = None
                g = ts * nlv + tv
                lv = slice(g * LANES, (g + 1) * LANES)
                ge = (g + 1) % (tt // LANES)
                le = slice(ge * LANES, (ge + 1) * LANES)
                als = (2 * ap, 2 * ap + 1)
                spread = lambda ref, al, h: pltpu.bitcast(jnp.broadcast_to(ref[al, h:h + 1, lv], (SUBLANES, LANES)), BF16)
                kk = [[spread(kk_ref, al, h) for h in range(PEER_HEADS)] for al in als]
                ea = [[spread(ea_ref, al, h) for h in range(PEER_HEADS)] for al in als]
                for r2 in range(nr2):
                    if r2 % (nr2 // nk) == (1 if unit % 2 == 0 else 2):
                        kb = slice(r2 // (nr2 // nk) * MXU_DEPTH, (r2 // (nr2 // nk) + 1) * MXU_DEPTH)
                        dk = _dot(lhs_ref[rows, kb], rhs_ref[kb, ls])
                        part = dk if part is None else part + dk
                    gates = [jnp.zeros((2 * SUBLANES, LANES), BF16) for _ in als]
                    for h in range(PEER_HEADS):
                        wrows = slice(h * (N_KEYS // 2) + r2 * SUBLANES, h * (N_KEYS // 2) + (r2 + 1) * SUBLANES)
                        code = pltpu.bitcast(cw_ref[wrows, lv], BF16)
                        eb = pltpu.bitcast(ebw_ref[wrows, le], BF16)
                        for i in range(2):
                            gates[i] = gates[i] + jnp.where(code >= kk[i][h], eb, jnp.zeros_like(eb)) * ea[i][h]
                    for i, al in enumerate(als):
                        rs = slice(al * N_KEYS + r2 * 2 * SUBLANES, al * N_KEYS + (r2 + 1) * 2 * SUBLANES)
                        wt_new[rs, lv] = gates[i] * _gelu_tanh(ht_old[rs, lv].astype(BF16))
                if unit % 2 == 0:
                    ht_new[rows, ls] = part
                else:
                    acc_s[rows, ls] += part

    @pl.when(c % 2 == 0)
    def _even():
        stages(ht0_s, ht1_s, wt1_s, wt0_s)

    @pl.when(c % 2 == 1)
    def _odd():
        stages(ht1_s, ht0_s, wt0_s, wt1_s)

    @pl.when(jnp.logical_and(c >= PEER_PIPE_FILL, (c - PEER_PIPE_FILL) % nblk == nblk - 1))
    def _finish():
        y = alpha * x1_ref[...] + acc_s[...].T
        o_ref[...] = _layer_norm(y, g_ref[...], b_ref[...])
        acc_s[...] = jnp.zeros(acc_s.shape, F32)


def _peer(cw, ebw, kk, ea, xt, u_bf, vt_bf, x1, g, b, alpha, tt):
    t, d = x1.shape
    ne = u_bf.shape[0]
    ch = PEER_EXPERT_BLOCK
    assert t % tt == 0 and tt % PEER_LANE_TILE == 0 and ne % ch == 0 and ne == N_KEYS * N_KEYS
    assert vt_bf.shape == (ne // ch, d, ch)
    nblk = ne // ch
    ablocks = ch // N_KEYS
    npairs = (t // tt) * nblk
    tile = lambda c, lag: jnp.clip(c - lag, 0, npairs - 1) // nblk
    block = lambda c, lag: jnp.clip(c - lag, 0, npairs - 1) % nblk
    return pl.pallas_call(
        functools.partial(_peer_kernel, alpha=alpha, nblk=nblk),
        grid=(npairs + PEER_PIPE_FILL,),
        in_specs=[
            pl.BlockSpec((PEER_HEADS * N_KEYS // 2, tt), lambda c: (0, tile(c, 1))),
            pl.BlockSpec((PEER_HEADS * N_KEYS // 2, tt), lambda c: (0, tile(c, 1))),
            pl.BlockSpec((ablocks, PEER_HEADS, tt), lambda c: (block(c, 1), 0, tile(c, 1))),
            pl.BlockSpec((ablocks, PEER_HEADS, tt), lambda c: (block(c, 1), 0, tile(c, 1))),
            pl.BlockSpec((d, tt), lambda c: (0, tile(c, 0))),
            pl.BlockSpec((ch, d), lambda c: (block(c, 0), 0)),
            pl.BlockSpec((None, d, ch), lambda c: (block(c, PEER_PIPE_FILL), 0, 0)),
            pl.BlockSpec((tt, d), lambda c: (tile(c, PEER_PIPE_FILL), 0)),
            pl.BlockSpec((1, d), lambda c: (0, 0)),
            pl.BlockSpec((1, d), lambda c: (0, 0)),
        ],
        out_specs=pl.BlockSpec((tt, d), lambda c: (tile(c, PEER_PIPE_FILL), 0)),
        out_shape=jax.ShapeDtypeStruct((t, d), F32),
        scratch_shapes=[
            pltpu.VMEM((ch, tt), F32),
            pltpu.VMEM((ch, tt), F32),
            pltpu.VMEM((ch, tt), BF16),
            pltpu.VMEM((ch, tt), BF16),
            pltpu.VMEM((d, tt), F32),
        ],
        compiler_params=_params("arbitrary"),
        name="peer",
    )(cw, ebw, kk, ea, xt, u_bf, vt_bf, x1, g, b)


def kernel(x_prompt, x_sample, cache_sb_k, cache_sb_v, cache_band_k, cache_band_v, w_in, w_out, gn_a, gn_b,
           rel_bias, ln1_g, ln1_b, peer_query, peer_subkeys, peer_u, peer_v, ln2_g, ln2_b):
    batch, seq, d = x_prompt.shape
    dec_batch, dec_seq, _ = x_sample.shape
    depth = w_in.shape[0]
    past = cache_sb_k.shape[2]
    nb = cache_band_k.shape[2]
    assert w_in.shape[2] == 6 * WIDTH and w_out.shape[1] == 2 * WIDTH
    assert seq >= BAND_PAST and nb == BAND_PAST
    alpha = float((2 * depth) ** 0.25)
    tp = batch * seq
    ts = dec_batch * dec_seq

    xp = x_prompt.reshape(tp, d)
    xs = x_sample.reshape(ts, d)
    row2 = lambda a: a.reshape(1, -1)
    heads = lambda a, n, s: a.reshape(n, s, N_HEADS, HEAD_DIM)
    to_slab = lambda a: jnp.transpose(a, (0, 1, 3, 4, 2)).reshape(a.shape[0], a.shape[1], WIDTH, a.shape[2])
    caches = [to_slab(c) for c in (cache_sb_k, cache_sb_v, cache_band_k, cache_band_v)]
    from_slab = lambda a: jnp.transpose(a.reshape(a.shape[0], a.shape[1], N_HEADS, HEAD_DIM, a.shape[3]), (0, 1, 4, 2, 3))
    prompt_rows, band_rows = None, None
    outs = [[] for _ in range(8)]
    for l in range(depth):
        w_in_bf = w_in[l].astype(BF16)
        woa = w_out[l, :WIDTH].astype(BF16)
        wob = w_out[l, WIDTH:].astype(BF16)
        wq = peer_query[l].astype(BF16)
        sk = peer_subkeys[l].astype(BF16)
        u_bf = peer_u[l].astype(BF16)
        vt_bf = peer_v[l].reshape(-1, PEER_EXPERT_BLOCK, d).transpose(0, 2, 1).astype(BF16)
        ga, gb = row2(gn_a[l]), row2(gn_b[l])
        g1, b1, g2, b2 = row2(ln1_g[l]), row2(ln1_b[l]), row2(ln2_g[l]), row2(ln2_b[l])

        qa, kab, vab, qb, kbb, vbb, *prompt_rows = _project_prompt(xp, w_in_bf, batch, seq, l, depth, prompt_rows)
        ca = _sb_prompt(qa, kab, vab, ga, batch, seq)
        cb = _band_prompt(qb, kbb, vbb, _band_prompt_bias(rel_bias[l]), gb, batch, seq)
        x1, x1t, st = _merge(ca, cb, xp, woa, wob, g1, b1, wq, sk, alpha)
        tt = min(PEER_TOKEN_TILE, tp)
        kk, ea, cw, ebw = _peer_gates(st, tt)
        xp = _peer(cw, ebw, kk, ea, x1t, u_bf, vt_bf, x1, g2, b2, alpha, tt)

        qa, ka, va, kab, vab, qb, kb, vb, kbb, vbb = _project(xs, w_in_bf)
        biasp, biasn = _sample_bias(rel_bias[l], dec_seq, nb)
        ca, cb, *band_rows = _sample_attn(
            qa, kab, vab, caches[0], caches[1], qb, kbb, vbb, kb, vb, caches[2], caches[3], biasp, biasn, ga, gb, l, band_rows)
        x1, x1t, st = _merge(ca, cb, xs, woa, wob, g1, b1, wq, sk, alpha)
        tt = min(PEER_TOKEN_TILE, ts)
        kk, ea, cw, ebw = _peer_gates(st, tt)
        xs = _peer(cw, ebw, kk, ea, x1t, u_bf, vt_bf, x1, g2, b2, alpha, tt)
        outs[4].append(heads(ka, dec_batch, dec_seq))
        outs[5].append(heads(va, dec_batch, dec_seq))

    stacked = ([from_slab(a) for a in prompt_rows] + [jnp.stack(o) for o in outs[4:6]]
               + [from_slab(a) for a in band_rows])
    return (xp.reshape(batch, seq, d), xs.reshape(dec_batch, dec_seq, d)) + tuple(stacked)
```

```python
import functools

import jax
import jax.numpy as jnp
import numpy as np
from jax import lax
from jax.experimental import pallas as pl
from jax.experimental.pallas import tpu as pltpu

F32 = jnp.float32
BF16 = jnp.bfloat16
U32 = jnp.uint32

HEAD_DIM = 64
N_HEADS = 8
WIDTH = N_HEADS * HEAD_DIM
CHUNK = 64
BAND_CHUNKS = 8
BAND_PAST = BAND_CHUNKS * CHUNK
MAX_REL = 128
N_KEYS = 128
PEER_HEADS = 8
PEER_TOPK = 16
NORM_EPS = 1e-5
NEG_INF = -1e30

LANES = 128
SUBLANES = 8
MXU_DEPTH = 256
VMEM_LIMIT_BYTES = 56 * 1024 * 1024

SB_DEAD_LOG = -104.0
SB_BLOCK = 128
SB_FIRST_BLOCKS = 3
BAND_QBLOCK = 256
BAND_HEAD_GROUP = 4
ROW_TILE = 512
PEER_TOKEN_TILE = 1024
PEER_GATE_TILE = 1024
PEER_EXPERT_BLOCK = 512
PEER_PIPE_FILL = 2
PEER_LANE_TILE = 256
PAD_SCORE = -3.0e38


def _params(*sem):
    return pltpu.CompilerParams(dimension_semantics=sem, vmem_limit_bytes=VMEM_LIMIT_BYTES)


def _nt_dot(a, b):
    return lax.dot_general(a, b, (((1,), (1,)), ((), ())), preferred_element_type=F32)


def _dot(a, b):
    return jnp.dot(a, b, preferred_element_type=F32)


def _layer_norm(y, g, b):
    mu = jnp.mean(y, axis=-1, keepdims=True)
    d = y - mu
    var = jnp.mean(d * d, axis=-1, keepdims=True)
    return d * lax.rsqrt(var + NORM_EPS) * g + b


def _head_rms(o, gain):
    ms = jnp.mean(o * o, axis=-1, keepdims=True)
    return o * lax.rsqrt(ms + NORM_EPS) * gain


def _proj_kernel(x_ref, w_ref, qa_ref, ka_ref, va_ref, kab_ref, vab_ref,
                 qb_ref, kb_ref, vb_ref, kbb_ref, vbb_ref):
    xb = x_ref[...].astype(BF16)
    scale = HEAD_DIM ** -0.5

    def group(g):
        return _dot(xb, w_ref[:, g * WIDTH:(g + 1) * WIDTH])

    qa_ref[...] = (group(0) * scale).astype(BF16)
    k = group(1)
    ka_ref[...] = k
    kab_ref[...] = k.astype(BF16)
    v = group(2)
    va_ref[...] = v
    vab_ref[...] = v.astype(BF16)
    qb_ref[...] = (group(3) * scale).astype(BF16)
    k = group(4)
    kb_ref[...] = k
    kbb_ref[...] = k.astype(BF16)
    v = group(5)
    vb_ref[...] = v
    vbb_ref[...] = v.astype(BF16)


def _project(x, w_bf):
    t, d = x.shape
    tm = min(ROW_TILE, t)
    assert t % tm == 0
    row = lambda i: (i, 0)
    f32o = jax.ShapeDtypeStruct((t, WIDTH), F32)
    bf16o = jax.ShapeDtypeStruct((t, WIDTH), BF16)
    blk = pl.BlockSpec((tm, WIDTH), row)
    return pl.pallas_call(
        _proj_kernel,
        grid=(t // tm,),
        in_specs=[pl.BlockSpec((tm, d), row), pl.BlockSpec(w_bf.shape, lambda i: (0, 0))],
        out_specs=[blk] * 10,
        out_shape=[bf16o, f32o, f32o, bf16o, bf16o, bf16o, f32o, f32o, bf16o, bf16o],
        compiler_params=_params("arbitrary"),
        name="proj",
    )(x, w_bf)


def _proj_prompt_kernel(x_ref, w_ref, *refs, tiles_per_stream):
    qa_ref, kab_ref, vab_ref, qb_ref, kbb_ref, vbb_ref, sbk_ref, sbv_ref, bdk_ref, bdv_ref = refs[-10:]
    j = pl.program_id(0) % tiles_per_stream
    xb = x_ref[...].astype(BF16)
    scale = HEAD_DIM ** -0.5

    def group(g):
        return _dot(xb, w_ref[:, g * WIDTH:(g + 1) * WIDTH])

    qa_ref[...] = (group(0) * scale).astype(BF16)
    k = group(1)
    kab_ref[...] = k.astype(BF16)
    sbk_ref[0] = k.T
    v = group(2)
    vab_ref[...] = v.astype(BF16)
    sbv_ref[0] = v.T
    qb_ref[...] = (group(3) * scale).astype(BF16)
    kb = group(4)
    kbb_ref[...] = kb.astype(BF16)
    vb = group(5)
    vbb_ref[...] = vb.astype(BF16)

    @pl.when(j == tiles_per_stream - 1)
    def _band_rows():
        bdk_ref[0] = kb.T
        bdv_ref[0] = vb.T


def _project_prompt(x, w_bf, batch, seq, layer, depth, stacked):
    t, d = x.shape
    tm = ROW_TILE
    assert seq % tm == 0 and tm == BAND_PAST and t == batch * seq
    nj = seq // tm
    row = lambda i: (i, 0)
    bf16o = jax.ShapeDtypeStruct((t, WIDTH), BF16)
    blk = pl.BlockSpec((tm, WIDTH), row)
    carried = () if stacked is None else tuple(stacked)
    return pl.pallas_call(
        functools.partial(_proj_prompt_kernel, tiles_per_stream=nj),
        grid=(t // tm,),
        in_specs=[pl.BlockSpec((tm, d), row), pl.BlockSpec(w_bf.shape, lambda i: (0, 0))]
                 + [pl.BlockSpec(memory_space=pl.ANY)] * len(carried),
        out_specs=[blk] * 6 + [pl.BlockSpec((None, 1, WIDTH, tm), lambda i: (layer, i // nj, 0, i % nj))] * 2
                  + [pl.BlockSpec((None, 1, WIDTH, tm), lambda i: (layer, i // nj, 0, 0))] * 2,
        out_shape=[bf16o] * 6 + [jax.ShapeDtypeStruct((depth, batch, WIDTH, seq), F32)] * 2
                  + [jax.ShapeDtypeStruct((depth, batch, WIDTH, tm), F32)] * 2,
        input_output_aliases={2 + k: 6 + k for k in range(len(carried))},
        compiler_params=_params("arbitrary"),
        name="proj_prompt",
    )(x, w_bf, *carried)


def _suffix_matrix(kb):
    kp = np.arange(kb)
    m = (kp[:, None] > kp[None, :]).astype(np.float32)
    one = np.concatenate([np.ones((kb, LANES), np.float32), m], axis=1)
    return jnp.asarray(np.concatenate([one, one], axis=0), dtype=BF16)


def _sb_log_terms(z, mask):
    sp = jnp.maximum(z, 0.0) + jnp.log(1.0 + jnp.exp(-jnp.abs(z)))
    log_keep = -sp
    if mask is not None:
        log_keep = jnp.where(mask, log_keep, 0.0)
    hi = log_keep.astype(BF16)
    lo = (log_keep - hi.astype(F32)).astype(BF16)
    return z - sp, jnp.concatenate([hi, lo], axis=1)


def _sb_weights(log_sig, sums, carry, mask):
    kb = log_sig.shape[1]
    log_w = log_sig + sums[:, LANES:LANES + kb]
    if carry is not None:
        log_w = log_w + carry
    w = jnp.exp(log_w)
    if mask is not None:
        w = jnp.where(mask, w, 0.0)
    return w, sums[:, :LANES]


def _sb_prompt_kernel(q_ref, k_ref, v_ref, g_ref, mm_ref, o_ref, carry_s, acc_s):
    i = pl.program_id(1)
    r = q_ref.shape[0]
    row = lax.broadcasted_iota(jnp.int32, (r, r), 0)
    col = lax.broadcasted_iota(jnp.int32, (r, r), 1)
    diag = col < row
    heads = [slice(h * HEAD_DIM, (h + 1) * HEAD_DIM) for h in range(N_HEADS)]

    def key_block(j):
        off = pl.multiple_of(j * r, r)
        q = q_ref[...]
        kj = k_ref[pl.ds(off, r), :]
        vj = v_ref[pl.ds(off, r), :]
        mm = mm_ref[...]
        zs = [_nt_dot(q[:, hs], kj[:, hs]) for hs in heads]
        terms = [_sb_log_terms(z, None) for z in zs]
        sums = [_dot(halves, mm) for _, halves in terms]
        ws = [_sb_weights(terms[h][0], sums[h], carry_s[h], None) for h in range(N_HEADS)]
        pvs = [_dot(ws[h][0].astype(BF16), vj[:, hs]) for h, hs in enumerate(heads)]
        alive = None
        for h in range(N_HEADS):
            carry = carry_s[h] + ws[h][1]
            carry_s[h] = carry
            acc_s[h] = acc_s[h] + pvs[h]
            alive = carry if alive is None else jnp.maximum(alive, carry)
        return jnp.max(alive)

    def first_blocks():
        q = q_ref[...]
        mm = mm_ref[...]
        offs = [pl.multiple_of(jnp.maximum(i - b, 0) * r, r) for b in range(SB_FIRST_BLOCKS)]
        masks = [diag] + [i >= b for b in range(1, SB_FIRST_BLOCKS)]
        ks = [k_ref[pl.ds(off, r), :] for off in offs]
        vs = [v_ref[pl.ds(off, r), :] for off in offs]
        zs = [[_nt_dot(q[:, hs], kb[:, hs]) for hs in heads] for kb in ks]
        terms = [[_sb_log_terms(z, m) for z in zb] for zb, m in zip(zs, masks)]
        sums = [[_dot(halves, mm) for _, halves in tb] for tb in terms]
        alive = None
        for h, hs in enumerate(heads):
            carry, acc = None, None
            for b in range(SB_FIRST_BLOCKS):
                w, total = _sb_weights(terms[b][h][0], sums[b][h], carry, masks[b])
                pv = _dot(w.astype(BF16), vs[b][:, hs])
                carry = total if carry is None else carry + total
                acc = pv if acc is None else acc + pv
            carry_s[h] = carry
            acc_s[h] = acc
            alive = carry if alive is None else jnp.maximum(alive, carry)
        return jnp.max(alive)

    def cond(s):
        j, cmax = s
        return jnp.logical_and(j >= 0, cmax > SB_DEAD_LOG)

    def body(s):
        j, _ = s
        return j - 1, key_block(j)

    lax.while_loop(cond, body, (i - SB_FIRST_BLOCKS, first_blocks()))
    outs = [_head_rms(acc_s[h], g_ref[:, hs]) for h, hs in enumerate(heads)]
    o_ref[...] = jnp.concatenate(outs, axis=1).astype(BF16)


def _sb_prompt(q, k, v, gain, batch, seq):
    r = SB_BLOCK
    assert seq % r == 0
    nq = seq // r
    return pl.pallas_call(
        _sb_prompt_kernel,
        grid=(batch, nq),
        in_specs=[
            pl.BlockSpec((r, WIDTH), lambda b, i: (b * nq + i, 0)),
            pl.BlockSpec((seq, WIDTH), lambda b, i: (b, 0)),
            pl.BlockSpec((seq, WIDTH), lambda b, i: (b, 0)),
            pl.BlockSpec((1, WIDTH), lambda b, i: (0, 0)),
            pl.BlockSpec((2 * r, LANES + r), lambda b, i: (0, 0)),
        ],
        out_specs=pl.BlockSpec((r, WIDTH), lambda b, i: (b * nq + i, 0)),
        out_shape=jax.ShapeDtypeStruct((batch * seq, WIDTH), BF16),
        scratch_shapes=[pltpu.VMEM((N_HEADS, r, LANES), F32),
                        pltpu.VMEM((N_HEADS, r, HEAD_DIM), F32)],
        compiler_params=_params("arbitrary", "arbitrary"),
        name="sb_prompt",
    )(q, k, v, gain, _suffix_matrix(r))


def _band_prompt_kernel(q_ref, k0_ref, k1_ref, k2_ref, v0_ref, v1_ref, v2_ref, bias_ref, g_ref, o_ref):
    i = pl.program_id(1)
    k_refs = (k0_ref, k1_ref, k2_ref)
    v_refs = (v0_ref, v1_ref, v2_ref)
    heads = [slice(h * HEAD_DIM, (h + 1) * HEAD_DIM) for h in range(N_HEADS)]
    q = q_ref[...]
    ks = [k_ref[...] for k_ref in k_refs]
    vs = [v_ref[...] for v_ref in v_refs]
    outs = []
    for g0 in range(0, N_HEADS, BAND_HEAD_GROUP):
        group = range(g0, g0 + BAND_HEAD_GROUP)
        scores = []
        for h in group:
            hs = heads[h]
            scs = []
            for w in range(3):
                sc = _nt_dot(q[:, hs], ks[w][:, hs]) + bias_ref[h, w]
                if w < 2:
                    sc = jnp.where(i >= 2 - w, sc, NEG_INF)
                scs.append(sc)
            scores.append(scs)
        weights = []
        for scs in scores:
            m = jnp.max(jnp.maximum(jnp.maximum(scs[0], scs[1]), scs[2]), axis=-1, keepdims=True)
            es = [jnp.exp(sc - m) for sc in scs]
            den = jnp.sum(es[0] + es[1] + es[2], axis=-1, keepdims=True)
            weights.append(([e.astype(BF16) for e in es], den))
        for (es, den), h in zip(weights, group):
            hs = heads[h]
            acc = _dot(es[0], vs[0][:, hs]) + _dot(es[1], vs[1][:, hs]) + _dot(es[2], vs[2][:, hs])
            outs.append(_head_rms(acc / den, g_ref[:, hs]))
    o_ref[...] = jnp.concatenate(outs, axis=1).astype(BF16)


def _toeplitz_bias(rel_bias, n, ncols, offset):
    period = n + ncols + 1
    m = jnp.arange(period)
    shift = jnp.where(m < ncols, m, m - period)
    vec = rel_bias[:, jnp.clip(offset - shift, -MAX_REL, MAX_REL) + MAX_REL].astype(F32)
    rows = jnp.tile(vec, (1, n))[:, :n * (period - 1)].reshape(rel_bias.shape[0], n, period - 1)
    return rows[:, :, :ncols]


def _band_prompt_bias(rel_bias):
    qb = BAND_QBLOCK
    bias = _toeplitz_bias(rel_bias, qb, 3 * qb, 2 * qb)
    kc = jnp.arange(3 * qb)[None, :] // CHUNK - (2 * qb // CHUNK)
    qc = jnp.arange(qb)[:, None] // CHUNK
    valid = (kc <= qc) & (kc >= qc - BAND_CHUNKS)
    bias = jnp.where(valid[None], bias, NEG_INF)
    return bias.reshape(N_HEADS, qb, 3, qb).transpose(0, 2, 1, 3)


def _band_prompt(q, k, v, bias, gain, batch, seq):
    qb = BAND_QBLOCK
    assert seq % qb == 0 and 2 * qb == BAND_PAST and qb % CHUNK == 0
    nq = seq // qb
    kspec = lambda back: pl.BlockSpec((qb, WIDTH), lambda b, i: (b * nq + jnp.maximum(i - back, 0), 0))
    return pl.pallas_call(
        _band_prompt_kernel,
        grid=(batch, nq),
        in_specs=[
            pl.BlockSpec((qb, WIDTH), lambda b, i: (b * nq + i, 0)),
            kspec(2), kspec(1), kspec(0), kspec(2), kspec(1), kspec(0),
            pl.BlockSpec(bias.shape, lambda b, i: (0, 0, 0, 0)),
            pl.BlockSpec((1, WIDTH), lambda b, i: (0, 0)),
        ],
        out_specs=pl.BlockSpec((qb, WIDTH), lambda b, i: (b * nq + i, 0)),
        out_shape=jax.ShapeDtypeStruct((batch * seq, WIDTH), BF16),
        compiler_params=_params("arbitrary", "arbitrary"),
        name="band_prompt",
    )(q, k, k, k, v, v, v, bias, gain)


def _sample_attn_kernel(qa_ref, kan_ref, van_ref, cak_ref, cav_ref,
                        qb_ref, kbn_ref, vbn_ref, kbf_ref, vbf_ref, cbk_ref, cbv_ref,
                        biasp_ref, biasn_ref, ga_ref, gb_ref, mmn_ref, mmp_ref, *refs):
    oa_ref, ob_ref, nbk_ref, nbv_ref = refs[-4:]
    sd = qa_ref.shape[0]
    past = cak_ref.shape[2]
    nb = cbk_ref.shape[2]
    row = lax.broadcasted_iota(jnp.int32, (sd, sd), 0)
    col = lax.broadcasted_iota(jnp.int32, (sd, sd), 1)
    diag = col < row
    cak = cak_ref[0].astype(BF16)
    cav = cav_ref[0].astype(BF16)
    cbk = cbk_ref[0].astype(BF16)
    cbv = cbv_ref[0].astype(BF16)
    mmn = mmn_ref[...]
    mmp = mmp_ref[...]
    heads = [slice(h * HEAD_DIM, (h + 1) * HEAD_DIM) for h in range(N_HEADS)]
    blocks = [slice(j * SB_BLOCK, (j + 1) * SB_BLOCK) for j in range(past // SB_BLOCK)]
    qa = qa_ref[...]
    z_new = [_nt_dot(qa[:, hs], kan_ref[:, hs]) for hs in heads]
    z_past = [_dot(qa[:, hs], cak[hs, :]) for hs in heads]
    t_new = [_sb_log_terms(z, diag) for z in z_new]
    t_past = [[_sb_log_terms(z[:, ks], None) for ks in blocks] for z in z_past]
    s_new = [_dot(halves, mmn) for _, halves in t_new]
    s_past = [[_dot(halves, mmp) for _, halves in t] for t in t_past]
    outs_a = []
    for h, hs in enumerate(heads):
        w_new, carry = _sb_weights(t_new[h][0], s_new[h], None, diag)
        w_past = [None] * len(blocks)
        for j in reversed(range(len(blocks))):
            w_past[j], total = _sb_weights(t_past[h][j][0], s_past[h][j], carry, None)
            carry = carry + total
        acc = _dot(w_new.astype(BF16), van_ref[:, hs]) + _nt_dot(jnp.concatenate(w_past, axis=1).astype(BF16), cav[hs, :])
        outs_a.append(_head_rms(acc, ga_ref[:, hs]))
    qb = qb_ref[...]
    sc_p = [_dot(qb[:, hs], cbk[hs, :]) + biasp_ref[h] for h, hs in enumerate(heads)]
    sc_n = [_nt_dot(qb[:, hs], kbn_ref[:, hs]) + biasn_ref[h] for h, hs in enumerate(heads)]
    outs_b = []
    for h, hs in enumerate(heads):
        m = jnp.maximum(jnp.max(sc_p[h], axis=-1, keepdims=True), jnp.max(sc_n[h], axis=-1, keepdims=True))
        ep = jnp.exp(sc_p[h] - m)
        en = jnp.exp(sc_n[h] - m)
        den = jnp.sum(ep, axis=-1, keepdims=True) + jnp.sum(en, axis=-1, keepdims=True)
        acc = _nt_dot(ep.astype(BF16), cbv[hs, :]) + _dot(en.astype(BF16), vbn_ref[:, hs])
        outs_b.append(_head_rms(acc / den, gb_ref[:, hs]))
    oa_ref[...] = jnp.concatenate(outs_a, axis=1).astype(BF16)
    ob_ref[...] = jnp.concatenate(outs_b, axis=1).astype(BF16)
    nbk_ref[0, :, :nb - sd] = cbk_ref[0, :, sd:]
    nbk_ref[0, :, nb - sd:] = kbf_ref[...].T
    nbv_ref[0, :, :nb - sd] = cbv_ref[0, :, sd:]
    nbv_ref[0, :, nb - sd:] = vbf_ref[...].T


def _sample_bias(rel_bias, sd, nb):
    bias = _toeplitz_bias(rel_bias, sd, nb + sd, nb)
    return bias[:, :, :nb], bias[:, :, nb:]


def _sample_attn(qa, kan, van, cak, cav, qb, kbn, vbn, kbf, vbf, cbk, cbv, biasp, biasn, ga, gb, layer, stacked):
    _, nbatch, _, past = cak.shape
    nb = cbk.shape[3]
    sd = qa.shape[0] // nbatch
    assert past % SB_BLOCK == 0 and sd % 8 == 0 and sd <= nb and sd <= LANES
    rows = pl.BlockSpec((sd, WIDTH), lambda b: (b, 0))
    cache = lambda n: pl.BlockSpec((None, 1, WIDTH, n), lambda b: (layer, b, 0, 0))
    rolled = pl.BlockSpec((None, 1, WIDTH, nb), lambda b: (layer, b, 0, 0))
    carried = () if stacked is None else tuple(stacked)
    full = lambda a: pl.BlockSpec(a.shape, lambda b: (0,) * a.ndim)
    mmn = _suffix_matrix(sd)
    mmp = _suffix_matrix(SB_BLOCK)
    return pl.pallas_call(
        _sample_attn_kernel,
        grid=(nbatch,),
        in_specs=[rows, rows, rows, cache(past), cache(past),
                  rows, rows, rows, rows, rows, cache(nb), cache(nb),
                  full(biasp), full(biasn), full(ga), full(gb), full(mmn), full(mmp)]
                 + [pl.BlockSpec(memory_space=pl.ANY)] * len(carried),
        out_specs=[rows, rows, rolled, rolled],
        out_shape=[jax.ShapeDtypeStruct((nbatch * sd, WIDTH), BF16)] * 2
                  + [jax.ShapeDtypeStruct(cbk.shape, F32)] * 2,
        input_output_aliases={18 + k: 2 + k for k in range(len(carried))},
        compiler_params=_params("arbitrary"),
        name="sample_attn",
    )(qa, kan, van, cak, cav, qb, kbn, vbn, kbf, vbf, cbk, cbv, biasp, biasn, ga, gb, mmn, mmp, *carried)


def _merge_kernel(ca_ref, cb_ref, x_ref, woa_ref, wob_ref, g_ref, b_ref, wq_ref, sk_ref,
                  x1_ref, x1t_ref, st_ref, *, alpha):
    tm = x_ref.shape[0]
    halves = [slice(k * (tm // 2), (k + 1) * (tm // 2)) for k in range(2)]
    mixes = [_dot(ca_ref[rs, :], woa_ref[...]) + _dot(cb_ref[rs, :], wob_ref[...]) for rs in halves]
    x1s = [_layer_norm(alpha * x_ref[rs, :] + mix, g_ref[...], b_ref[...]) for rs, mix in zip(halves, mixes)]
    qps = [_dot(x1.astype(BF16), wq_ref[...]) for x1 in x1s]
    for rs, x1 in zip(halves, x1s):
        x1_ref[rs, :] = x1
        x1t_ref[:, rs] = x1.T.astype(BF16)
    half = sk_ref.shape[2]
    for h in range(PEER_HEADS):
        for j in range(2):
            c0 = (2 * h + j) * half
            for rs, qp in zip(halves, qps):
                qh = qp[:, c0:c0 + half].astype(BF16)
                st_ref[(2 * h + j) * N_KEYS:(2 * h + j + 1) * N_KEYS, rs] = _nt_dot(sk_ref[j], qh)


def _merge(ca, cb, x, woa, wob, g, b, wq, sk, alpha):
    t, d = x.shape
    tm = min(ROW_TILE, t)
    assert t % tm == 0
    nscore = PEER_HEADS * 2 * N_KEYS
    row = lambda i: (i, 0)
    colb = lambda i: (0, i)
    full = lambda a: pl.BlockSpec(a.shape, lambda i: (0,) * a.ndim)
    return pl.pallas_call(
        functools.partial(_merge_kernel, alpha=alpha),
        grid=(t // tm,),
        in_specs=[pl.BlockSpec((tm, WIDTH), row), pl.BlockSpec((tm, WIDTH), row), pl.BlockSpec((tm, d), row),
                  full(woa), full(wob), full(g), full(b), full(wq), full(sk)],
        out_specs=[pl.BlockSpec((tm, d), row), pl.BlockSpec((d, tm), colb), pl.BlockSpec((nscore, tm), colb)],
        out_shape=[jax.ShapeDtypeStruct((t, d), F32), jax.ShapeDtypeStruct((d, t), BF16),
                   jax.ShapeDtypeStruct((nscore, t), F32)],
        compiler_params=_params("arbitrary"),
        name="merge",
    )(ca, cb, x, woa, wob, g, b, wq, sk)


def _cmp_exchange(v, i, j):
    a, b = v[i], v[j]
    v[i] = jnp.maximum(a, b)
    v[j] = jnp.minimum(a, b)


_SORT16_NETWORK = (
    (0, 13), (1, 12), (2, 15), (3, 14), (4, 8), (5, 6), (7, 11), (9, 10),
    (0, 5), (1, 7), (2, 9), (3, 4), (6, 13), (8, 14), (10, 15), (11, 12),
    (0, 1), (2, 3), (4, 5), (6, 8), (7, 9), (10, 11), (12, 13), (14, 15),
    (0, 2), (1, 3), (4, 10), (5, 11), (6, 7), (8, 9), (12, 14), (13, 15),
    (1, 2), (3, 12), (4, 6), (5, 7), (8, 10), (9, 11), (13, 14),
    (1, 4), (2, 6), (5, 8), (7, 10), (9, 13), (11, 14),
    (2, 4), (3, 6), (9, 12), (11, 13),
    (3, 5), (6, 8), (7, 9), (10, 12),
    (3, 4), (5, 6), (7, 8), (9, 10), (11, 12),
    (6, 7), (8, 9),
)


def _sort16_desc(v):
    v = list(v)
    assert len(v) == PEER_TOPK
    for i, j in _SORT16_NETWORK:
        _cmp_exchange(v, i, j)
    return v


def _count_true_suffix(test, rows):
    sel = jnp.where
    m1 = test(rows[8])
    m2 = test(sel(m1, rows[4], rows[12]))
    m3 = test(sel(m1, sel(m2, rows[2], rows[6]), sel(m2, rows[10], rows[14])))
    m4 = test(sel(m1, sel(m2, sel(m3, rows[1], rows[3]), sel(m3, rows[5], rows[7])),
                  sel(m2, sel(m3, rows[9], rows[11]), sel(m3, rows[13], rows[15]))))
    return sel(m1, 8.0, 0.0) + sel(m2, 4.0, 0.0) + sel(m3, 2.0, 0.0) + sel(m4, 1.0, 0.0) + sel(test(rows[0]), 1.0, 0.0)


def _merge_top16(a, b):
    n = len(a)
    top = [jnp.maximum(a[i], b[n - 1 - i]) for i in range(n)]
    out = [jnp.minimum(a[i], b[n - 1 - i]) for i in range(n)]
    while len(out) > 1:
        out = [jnp.maximum(out[2 * i], out[2 * i + 1]) for i in range(len(out) // 2)]
    j = n // 2
    while j >= 1:
        for i in range(n):
            if (i & j) == 0:
                _cmp_exchange(top, i, i + j)
        j //= 2
    return top, out[0]


def _top16_and_next(vals):
    groups = [_sort16_desc(vals[g:g + PEER_TOPK]) for g in range(0, len(vals), PEER_TOPK)]
    nxt = None
    while len(groups) > 1:
        merged = []
        for g in range(0, len(groups), 2):
            top, left = _merge_top16(groups[g], groups[g + 1])
            merged.append(top)
            nxt = left if nxt is None else jnp.maximum(nxt, left)
        groups = merged
    return groups[0], nxt


def _gelu_tanh(x):
    return 0.5 * x * (1.0 + jnp.tanh(0.7978845608028654 * (x + 0.044715 * (x * x * x))))


def _bf16_bits(x):
    return pltpu.bitcast(x.astype(BF16).astype(F32), U32)


def _pack_row_pairs(x, pair_s):
    half = x.shape[0] // 2
    pair_s[...] = x
    lo = pair_s[pl.ds(0, half, stride=2), :]
    hi = pair_s[pl.ds(1, half, stride=2), :]
    return _bf16_bits(hi) | (_bf16_bits(lo) >> 16)


def _both_halves(x):
    b = _bf16_bits(x)
    return b | (b >> 16)


def _peer_gate_kernel(st_ref, kk_ref, ea_ref, cw_ref, ebw_ref, row_s, pair_s, *, peer_tile):
    tt = st_ref.shape[1]
    ng = tt // LANES
    k1 = PEER_TOPK + 1
    tiles_per_peer = peer_tile // LANES

    def head(h, carry):
        base = pl.multiple_of(h * 2 * N_KEYS, 2 * N_KEYS)
        s1 = st_ref[pl.ds(base, N_KEYS), :]
        s2 = st_ref[pl.ds(base + N_KEYS, N_KEYS), :]
        s1r = s1.reshape(N_KEYS, ng, LANES)
        s2r = s2.reshape(N_KEYS, ng, LANES)
        top_a, next_a = _top16_and_next([s1r[a] for a in range(N_KEYS)])
        top_b, next_b = _top16_and_next([s2r[a] for a in range(N_KEYS)])
        la = top_a + [next_a]
        lb = top_b + [next_b]
        cands = [la[i - 1] + lb[j - 1] for i in range(1, k1 + 1) for j in range(1, k1 + 1) if i * j <= k1]
        pad = jnp.full_like(cands[0], PAD_SCORE)
        cands = cands + [pad] * (-len(cands) % PEER_TOPK)
        top_c, next_c = _top16_and_next(cands)
        tau = 0.5 * (top_c[PEER_TOPK - 1] + next_c)
        den = jnp.ones_like(tau)
        for cv in top_c[1:]:
            den = den + jnp.exp(cv - top_c[0])
        vals = [tau, la[0], 1.0 / den] + lb
        for r, val in enumerate(vals):
            for g in range(ng):
                row_s[r:r + 1, g * LANES:(g + 1) * LANES] = val[g:g + 1, :]
        rows = pl.ds(pl.multiple_of(h * (N_KEYS // 2), N_KEYS // 2), N_KEYS // 2)
        kk_words, ea_words = [], []
        for g in range(ng):
            lg = slice(g * LANES, (g + 1) * LANES)
            s1g = st_ref[pl.ds(base, N_KEYS), lg]
            s2g = st_ref[pl.ds(base + N_KEYS, N_KEYS), lg]
            tau_r = row_s[0:1, lg]
            m1_r = row_s[1:2, lg]
            iz_r = row_s[2:3, lg]
            lb_r = [row_s[3 + j:4 + j, lg] for j in range(k1)]
            code = _count_true_suffix(lambda row: s2g >= row, lb_r[1:]) + jnp.where(s2g >= lb_r[0], 1.0, 0.0)
            th = tau_r - s1g
            n = _count_true_suffix(lambda row: row >= th, lb_r[PEER_TOPK - 1::-1])
            kk_words.append(_both_halves(float(k1 + 1) - n))
            ea_words.append(_both_halves(jnp.exp(s1g - m1_r) * iz_r))
            cw_ref[rows, lg] = _pack_row_pairs(code, pair_s)
            gr = g // tiles_per_peer * tiles_per_peer + (g + 1) % tiles_per_peer
            ebw_ref[rows, gr * LANES:(gr + 1) * LANES] = _pack_row_pairs(jnp.exp(s2g - lb_r[0]), pair_s)
        kk_ref[:, pl.ds(h, 1), :] = jnp.concatenate(kk_words, axis=1)[:, None, :]
        ea_ref[:, pl.ds(h, 1), :] = jnp.concatenate(ea_words, axis=1)[:, None, :]
        return carry

    lax.fori_loop(0, PEER_HEADS, head, 0)


def _peer_gates(st, peer_tile):
    nscore, t = st.shape
    tt = PEER_GATE_TILE if t % PEER_GATE_TILE == 0 else peer_tile
    assert t % tt == 0 and tt % peer_tile == 0 and peer_tile > LANES and peer_tile % LANES == 0
    rows = jax.ShapeDtypeStruct((N_KEYS, PEER_HEADS, t), U32)
    packed = jax.ShapeDtypeStruct((PEER_HEADS * N_KEYS // 2, t), U32)
    return pl.pallas_call(
        functools.partial(_peer_gate_kernel, peer_tile=peer_tile),
        grid=(t // tt,),
        in_specs=[pl.BlockSpec((nscore, tt), lambda i: (0, i))],
        out_specs=[pl.BlockSpec((N_KEYS, PEER_HEADS, tt), lambda i: (0, 0, i)),
                   pl.BlockSpec((N_KEYS, PEER_HEADS, tt), lambda i: (0, 0, i)),
                   pl.BlockSpec((PEER_HEADS * N_KEYS // 2, tt), lambda i: (0, i)),
                   pl.BlockSpec((PEER_HEADS * N_KEYS // 2, tt), lambda i: (0, i))],
        out_shape=[rows, rows, packed, packed],
        scratch_shapes=[pltpu.VMEM((3 * SUBLANES, tt), F32), pltpu.VMEM((N_KEYS, LANES), F32)],
        compiler_params=_params("arbitrary"),
        name="peer_gates",
    )(st)


def _peer_kernel(cw_ref, ebw_ref, kk_ref, ea_ref, xt_ref, u_ref, vt_ref, x1_ref, g_ref, b_ref, o_ref,
                 ht0_s, ht1_s, wt0_s, wt1_s, acc_s, *, alpha, nblk):
    c = pl.program_id(0)
    tt = xt_ref.shape[1]
    ch = u_ref.shape[0]

    @pl.when(c == 0)
    def _clear():
        ht1_s[...] = jnp.zeros(ht1_s.shape, F32)
        wt0_s[...] = jnp.zeros(wt0_s.shape, BF16)
        wt1_s[...] = jnp.zeros(wt1_s.shape, BF16)
        acc_s[...] = jnp.zeros(acc_s.shape, F32)

    ablocks = ch // N_KEYS
    lt = PEER_LANE_TILE

    def stages(ht_new, ht_old, wt_new, wt_old):
        d = vt_ref.shape[0]
        npiece = ablocks // 2
        nlv = lt // LANES
        nr2 = N_KEYS // (2 * SUBLANES)
        assert nlv == 2
        for ts in range(tt // lt):
            ls = slice(ts * lt, (ts + 1) * lt)
            for unit in range(npiece * nlv):
                ap, tv = unit // nlv, unit % nlv
                piece = unit // 2
                if unit % 2 == 0:
                    rows = slice(piece * (ch // npiece), (piece + 1) * (ch // npiece))
                    lhs_ref, rhs_ref, kdim = u_ref, xt_ref, u_ref.shape[1]
                else:
                    rows = slice(piece * (d // npiece), (piece + 1) * (d // npiece))
                    lhs_ref, rhs_ref, kdim = vt_ref, wt_old, ch
                nk = kdim // MXU_DEPTH
                part = None
                g = ts * nlv + tv
                lv = slice(g * LANES, (g + 1) * LANES)
                ge = (g + 1) % (tt // LANES)
                le = slice(ge * LANES, (ge + 1) * LANES)
                als = (2 * ap, 2 * ap + 1)
                spread = lambda ref, al, h: pltpu.bitcast(jnp.broadcast_to(ref[al, h:h + 1, lv], (SUBLANES, LANES)), BF16)
                kk = [[spread(kk_ref, al, h) for h in range(PEER_HEADS)] for al in als]
                ea = [[spread(ea_ref, al, h) for h in range(PEER_HEADS)] for al in als]
                for r2 in range(nr2):
                    if r2 % (nr2 // nk) == (1 if unit % 2 == 0 else 2):
                        kb = slice(r2 // (nr2 // nk) * MXU_DEPTH, (r2 // (nr2 // nk) + 1) * MXU_DEPTH)
                        dk = _dot(lhs_ref[rows, kb], rhs_ref[kb, ls])
                        part = dk if part is None else part + dk
                    gates = [jnp.zeros((2 * SUBLANES, LANES), BF16) for _ in als]
                    for h in range(PEER_HEADS):
                        wrows = slice(h * (N_KEYS // 2) + r2 * SUBLANES, h * (N_KEYS // 2) + (r2 + 1) * SUBLANES)
                        code = pltpu.bitcast(cw_ref[wrows, lv], BF16)
                        eb = pltpu.bitcast(ebw_ref[wrows, le], BF16)
                        for i in range(2):
                            gates[i] = gates[i] + jnp.where(code >= kk[i][h], eb, jnp.zeros_like(eb)) * ea[i][h]
                    for i, al in enumerate(als):
                        rs = slice(al * N_KEYS + r2 * 2 * SUBLANES, al * N_KEYS + (r2 + 1) * 2 * SUBLANES)
                        wt_new[rs, lv] = gates[i] * _gelu_tanh(ht_old[rs, lv].astype(BF16))
                if unit % 2 == 0:
                    ht_new[rows, ls] = part
                else:
                    acc_s[rows, ls] += part

    @pl.when(c % 2 == 0)
    def _even():
        stages(ht0_s, ht1_s, wt1_s, wt0_s)

    @pl.when(c % 2 == 1)
    def _odd():
        stages(ht1_s, ht0_s, wt0_s, wt1_s)

    @pl.when(jnp.logical_and(c >= PEER_PIPE_FILL, (c - PEER_PIPE_FILL) % nblk == nblk - 1))
    def _finish():
        y = alpha * x1_ref[...] + acc_s[...].T
        o_ref[...] = _layer_norm(y, g_ref[...], b_ref[...])
        acc_s[...] = jnp.zeros(acc_s.shape, F32)


def _peer(cw, ebw, kk, ea, xt, u_bf, vt_bf, x1, g, b, alpha, tt):
    t, d = x1.shape
    ne = u_bf.shape[0]
    ch = PEER_EXPERT_BLOCK
    assert t % tt == 0 and tt % PEER_LANE_TILE == 0 and ne % ch == 0 and ne == N_KEYS * N_KEYS
    assert vt_bf.shape == (ne // ch, d, ch)
    nblk = ne // ch
    ablocks = ch // N_KEYS
    npairs = (t // tt) * nblk
    tile = lambda c, lag: jnp.clip(c - lag, 0, npairs - 1) // nblk
    block = lambda c, lag: jnp.clip(c - lag, 0, npairs - 1) % nblk
    return pl.pallas_call(
        functools.partial(_peer_kernel, alpha=alpha, nblk=nblk),
        grid=(npairs + PEER_PIPE_FILL,),
        in_specs=[
            pl.BlockSpec((PEER_HEADS * N_KEYS // 2, tt), lambda c: (0, tile(c, 1))),
            pl.BlockSpec((PEER_HEADS * N_KEYS // 2, tt), lambda c: (0, tile(c, 1))),
            pl.BlockSpec((ablocks, PEER_HEADS, tt), lambda c: (block(c, 1), 0, tile(c, 1))),
            pl.BlockSpec((ablocks, PEER_HEADS, tt), lambda c: (block(c, 1), 0, tile(c, 1))),
            pl.BlockSpec((d, tt), lambda c: (0, tile(c, 0))),
            pl.BlockSpec((ch, d), lambda c: (block(c, 0), 0)),
            pl.BlockSpec((None, d, ch), lambda c: (block(c, PEER_PIPE_FILL), 0, 0)),
            pl.BlockSpec((tt, d), lambda c: (tile(c, PEER_PIPE_FILL), 0)),
            pl.BlockSpec((1, d), lambda c: (0, 0)),
            pl.BlockSpec((1, d), lambda c: (0, 0)),
        ],
        out_specs=pl.BlockSpec((tt, d), lambda c: (tile(c, PEER_PIPE_FILL), 0)),
        out_shape=jax.ShapeDtypeStruct((t, d), F32),
        scratch_shapes=[
            pltpu.VMEM((ch, tt), F32),
            pltpu.VMEM((ch, tt), F32),
            pltpu.VMEM((ch, tt), BF16),
            pltpu.VMEM((ch, tt), BF16),
            pltpu.VMEM((d, tt), F32),
        ],
        compiler_params=_params("arbitrary"),
        name="peer",
    )(cw, ebw, kk, ea, xt, u_bf, vt_bf, x1, g, b)


def kernel(x_prompt, x_sample, cache_sb_k, cache_sb_v, cache_band_k, cache_band_v, w_in, w_out, gn_a, gn_b,
           rel_bias, ln1_g, ln1_b, peer_query, peer_subkeys, peer_u, peer_v, ln2_g, ln2_b):
    batch, seq, d = x_prompt.shape
    dec_batch, dec_seq, _ = x_sample.shape
    depth = w_in.shape[0]
    past = cache_sb_k.shape[2]
    nb = cache_band_k.shape[2]
    assert w_in.shape[2] == 6 * WIDTH and w_out.shape[1] == 2 * WIDTH
    assert seq >= BAND_PAST and nb == BAND_PAST
    alpha = float((2 * depth) ** 0.25)
    tp = batch * seq
    ts = dec_batch * dec_seq

    xp = x_prompt.reshape(tp, d)
    xs = x_sample.reshape(ts, d)
    row2 = lambda a: a.reshape(1, -1)
    heads = lambda a, n, s: a.reshape(n, s, N_HEADS, HEAD_DIM)
    to_slab = lambda a: jnp.transpose(a, (0, 1, 3, 4, 2)).reshape(a.shape[0], a.shape[1], WIDTH, a.shape[2])
    caches = [to_slab(c) for c in (cache_sb_k, cache_sb_v, cache_band_k, cache_band_v)]
    from_slab = lambda a: jnp.transpose(a.reshape(a.shape[0], a.shape[1], N_HEADS, HEAD_DIM, a.shape[3]), (0, 1, 4, 2, 3))
    prompt_rows, band_rows = None, None
    outs = [[] for _ in range(8)]
    for l in range(depth):
        w_in_bf = w_in[l].astype(BF16)
        woa = w_out[l, :WIDTH].astype(BF16)
        wob = w_out[l, WIDTH:].astype(BF16)
        wq = peer_query[l].astype(BF16)
        sk = peer_subkeys[l].astype(BF16)
        u_bf = peer_u[l].astype(BF16)
        vt_bf = peer_v[l].reshape(-1, PEER_EXPERT_BLOCK, d).transpose(0, 2, 1).astype(BF16)
        ga, gb = row2(gn_a[l]), row2(gn_b[l])
        g1, b1, g2, b2 = row2(ln1_g[l]), row2(ln1_b[l]), row2(ln2_g[l]), row2(ln2_b[l])

        qa, kab, vab, qb, kbb, vbb, *prompt_rows = _project_prompt(xp, w_in_bf, batch, seq, l, depth, prompt_rows)
        ca = _sb_prompt(qa, kab, vab, ga, batch, seq)
        cb = _band_prompt(qb, kbb, vbb, _band_prompt_bias(rel_bias[l]), gb, batch, seq)
        x1, x1t, st = _merge(ca, cb, xp, woa, wob, g1, b1, wq, sk, alpha)
        tt = min(PEER_TOKEN_TILE, tp)
        kk, ea, cw, ebw = _peer_gates(st, tt)
        xp = _peer(cw, ebw, kk, ea, x1t, u_bf, vt_bf, x1, g2, b2, alpha, tt)

        qa, ka, va, kab, vab, qb, kb, vb, kbb, vbb = _project(xs, w_in_bf)
        biasp, biasn = _sample_bias(rel_bias[l], dec_seq, nb)
        ca, cb, *band_rows = _sample_attn(
            qa, kab, vab, caches[0], caches[1], qb, kbb, vbb, kb, vb, caches[2], caches[3], biasp, biasn, ga, gb, l, band_rows)
        x1, x1t, st = _merge(ca, cb, xs, woa, wob, g1, b1, wq, sk, alpha)
        tt = min(PEER_TOKEN_TILE, ts)
        kk, ea, cw, ebw = _peer_gates(st, tt)
        xs = _peer(cw, ebw, kk, ea, x1t, u_bf, vt_bf, x1, g2, b2, alpha, tt)
        outs[4].append(heads(ka, dec_batch, dec_seq))
        outs[5].append(heads(va, dec_batch, dec_seq))

    stacked = ([from_slab(a) for a in prompt_rows] + [jnp.stack(o) for o in outs[4:6]]
               + [from_slab(a) for a in band_rows])
    return (xp.reshape(batch, seq, d), xs.reshape(dec_batch, dec_seq, d)) + tuple(stacked)
```
